```python
import jax
import jax.numpy as jnp
from jax import lax
import numpy as np

D_MODEL = 2048
BATCH = 1
SEQ = 16384
DEPTH = 2
DEC_BATCH = 16
DEC_SEQ = 64
PAST_LEN = 2048

CHUNK = 64
N_MIXERS = 2
N_CONV_LAYERS = (DEPTH + 1) // 2
N_ATTN_LAYERS = DEPTH // 2
CONV_WIDTH = 3
HEAD_DIM = 64
N_HEADS = D_MODEL // HEAD_DIM
N_KV_HEADS = N_HEADS // 8
GROUP = N_HEADS // N_KV_HEADS
QKV_DIM = (N_HEADS + 2 * N_KV_HEADS) * HEAD_DIM
WINDOW = 128
LOOKBACK_CHUNKS = WINDOW // CHUNK
D_FF = 256 * ((8 * D_MODEL // 3 + 255) // 256)
ROPE_THETA = 10000.0
NORM_EPS = 1e-6
N_NORMS = 6

kernel_name = 'hybrid_stream_conv_swa_macaron_step'


def rms_norm(x, g):
    xf = x.astype(jnp.float32)
    y = xf * lax.rsqrt(jnp.mean(xf * xf, axis=-1, keepdims=True) + NORM_EPS)
    return (y * g.astype(jnp.float32)).astype(x.dtype)


def swiglu(x, w_in, w_out):
    gate, up = jnp.split(x @ w_in, 2, axis=-1)
    return (jax.nn.silu(gate) * up) @ w_out


def rope(x, pos):
    half = HEAD_DIM // 2
    inv_freq = ROPE_THETA ** (-jnp.arange(half, dtype=jnp.float32) / half)
    ang = pos.astype(jnp.float32)[:, None] * inv_freq[None, :]
    cos = jnp.cos(ang)[:, None, :]
    sin = jnp.sin(ang)[:, None, :]
    xf = x.astype(jnp.float32)
    x1, x2 = xf[..., :half], xf[..., half:]
    return jnp.concatenate([x1 * cos - x2 * sin, x2 * cos + x1 * sin], axis=-1).astype(x.dtype)


def sink_softmax(s, sink):
    m = jnp.maximum(jnp.max(s, axis=-1, keepdims=True), sink)
    e = jnp.exp(s - m)
    return e / (jnp.sum(e, axis=-1, keepdims=True) + jnp.exp(sink - m))


def project_qkv(x, w_qkv, pos):
    b, t, _ = x.shape
    qkv = x @ w_qkv
    nq = N_HEADS * HEAD_DIM
    nk = N_KV_HEADS * HEAD_DIM
    q = qkv[..., :nq].reshape(b, t, N_HEADS, HEAD_DIM)
    k = qkv[..., nq:nq + nk].reshape(b, t, N_KV_HEADS, HEAD_DIM)
    v = qkv[..., nq + nk:].reshape(b, t, N_KV_HEADS, HEAD_DIM)
    return rope(q, pos), rope(k, pos), v


def swa_prompt(x, w_qkv, w_o, sinks):
    b, s_len, _ = x.shape
    n_chunks = s_len // CHUNK
    span = (LOOKBACK_CHUNKS + 1) * CHUNK
    q, k, v = project_qkv(x, w_qkv, jnp.arange(s_len))
    pad = ((0, 0), (LOOKBACK_CHUNKS * CHUNK, 0), (0, 0), (0, 0))
    kc = jnp.pad(k, pad).reshape(b, n_chunks + LOOKBACK_CHUNKS, CHUNK, N_KV_HEADS, HEAD_DIM)
    vc = jnp.pad(v, pad).reshape(b, n_chunks + LOOKBACK_CHUNKS, CHUNK, N_KV_HEADS, HEAD_DIM)
    kb = jnp.concatenate([kc[:, j:j + n_chunks] for j in range(LOOKBACK_CHUNKS + 1)], axis=2)
    vb = jnp.concatenate([vc[:, j:j + n_chunks] for j in range(LOOKBACK_CHUNKS + 1)], axis=2)
    qb = q.reshape(b, n_chunks, CHUNK, N_KV_HEADS, GROUP, HEAD_DIM)
    s = jnp.einsum('bcqkgd,bcskd->bckgqs', qb, kb,
                   preferred_element_type=jnp.float32) * (HEAD_DIM ** -0.5)
    key_chunk = (jnp.arange(n_chunks)[:, None] - LOOKBACK_CHUNKS
                 + jnp.arange(span)[None, :] // CHUNK)
    valid = (key_chunk >= 0)[None, :, None, None, None, :]
    s = jnp.where(valid, s, -jnp.inf)
    p = sink_softmax(s, sinks.astype(jnp.float32).reshape(N_KV_HEADS, GROUP, 1, 1))
    o = jnp.einsum('bckgqs,bcskd->bcqkgd', p.astype(vb.dtype), vb)
    o = o.reshape(b, s_len, N_HEADS * HEAD_DIM)
    rows = min(WINDOW, s_len)
    return o @ w_o, k[:, s_len - rows:], v[:, s_len - rows:]


def swa_sample(x, cache_k, cache_v, w_qkv, w_o, sinks):
    b, t, _ = x.shape
    rows = cache_k.shape[1]
    q_pos = PAST_LEN + jnp.arange(t)
    q, k, v = project_qkv(x, w_qkv, q_pos)
    k_all = jnp.concatenate([cache_k.astype(k.dtype), k], axis=1)
    v_all = jnp.concatenate([cache_v.astype(v.dtype), v], axis=1)
    key_pos = PAST_LEN - rows + jnp.arange(rows + t)
    qc = (q_pos // CHUNK)[:, None]
    kc = (key_pos // CHUNK)[None, :]
    valid = (kc >= qc - LOOKBACK_CHUNKS) & (kc <= qc)
    qg = q.reshape(b, t, N_KV_HEADS, GROUP, HEAD_DIM)
    s = jnp.einsum('btkgd,bskd->bkgts', qg, k_all,
                   preferred_element_type=jnp.float32) * (HEAD_DIM ** -0.5)
    s = jnp.where(valid, s, -jnp.inf)
    p = sink_softmax(s, sinks.astype(jnp.float32).reshape(N_KV_HEADS, GROUP, 1, 1))
    o = jnp.einsum('bkgts,bskd->btkgd', p.astype(v_all.dtype), v_all)
    o = o.reshape(b, t, N_HEADS * HEAD_DIM)
    return o @ w_o, k_all[:, -rows:], v_all[:, -rows:]


def short_conv_mixer(x, conv_state, w_in, conv_w, w_out):
    t = x.shape[1]
    b_gate, c_gate, h = jnp.split(x @ w_in, 3, axis=-1)
    u = jnp.concatenate([conv_state.astype(x.dtype), c_gate * h], axis=1)
    conv = sum(u[:, j:j + t] * conv_w[j] for j in range(CONV_WIDTH))
    return (b_gate * conv) @ w_out, u[:, t:]


def trunk(x, conv_state, cache_k, cache_v, norm_g, w_ffn_in, w_ffn_out, w_conv_in,
          w_conv, w_conv_out, w_qkv, w_attn_out, attn_sinks, is_prompt):
    b = x.shape[0]
    conv_new, k_new, v_new = [], [], []
    for i in range(DEPTH):
        g = norm_g[i]
        j = i // N_MIXERS
        x = x + 0.5 * rms_norm(swiglu(rms_norm(x, g[0]), w_ffn_in[i, 0], w_ffn_out[i, 0]), g[1])
        h = rms_norm(x, g[2])
        if i % N_MIXERS == 0:
            if is_prompt:
                st = jnp.zeros((b, CONV_WIDTH - 1, D_MODEL), h.dtype)
            else:
                st = conv_state[j]
            m, st_new = short_conv_mixer(h, st, w_conv_in[j], w_conv[j], w_conv_out[j])
            conv_new.append(st_new)
        else:
            if is_prompt:
                m, kn, vn = swa_prompt(h, w_qkv[j], w_attn_out[j], attn_sinks[j])
            else:
                m, kn, vn = swa_sample(h, cache_k[j], cache_v[j], w_qkv[j], w_attn_out[j], attn_sinks[j])
            k_new.append(kn)
            v_new.append(vn)
        x = x + rms_norm(m, g[3])
        x = x + 0.5 * rms_norm(swiglu(rms_norm(x, g[4]), w_ffn_in[i, 1], w_ffn_out[i, 1]), g[5])
    return x, jnp.stack(conv_new), jnp.stack(k_new), jnp.stack(v_new)


def setup_inputs(seed: int = 0) -> dict:
    key = jax.random.key(seed)
    ks = jax.random.split(key, 14)
    rows = min(WINDOW, PAST_LEN)

    def nrm(k, shape, scale=1.0):
        return jax.random.normal(k, shape, jnp.float32) * scale

    return {
        'x_prompt': nrm(ks[0], (BATCH, SEQ, D_MODEL)),
        'x_sample': nrm(ks[1], (DEC_BATCH, DEC_SEQ, D_MODEL)),
        'state_conv': nrm(ks[2], (N_CONV_LAYERS, DEC_BATCH, CONV_WIDTH - 1, D_MODEL)),
        'cache_k': nrm(ks[3], (N_ATTN_LAYERS, DEC_BATCH, rows, N_KV_HEADS, HEAD_DIM)),
        'cache_v': nrm(ks[4], (N_ATTN_LAYERS, DEC_BATCH, rows, N_KV_HEADS, HEAD_DIM)),
        'norm_g': 1.0 + nrm(ks[5], (DEPTH, N_NORMS, D_MODEL), 0.02),
        'w_ffn_in': nrm(ks[6], (DEPTH, 2, D_MODEL, 2 * D_FF), D_MODEL ** -0.5),
        'w_ffn_out': nrm(ks[7], (DEPTH, 2, D_FF, D_MODEL), D_FF ** -0.5),
        'w_conv_in': nrm(ks[8], (N_CONV_LAYERS, D_MODEL, 3 * D_MODEL), D_MODEL ** -0.5),
        'w_conv': nrm(ks[9], (N_CONV_LAYERS, CONV_WIDTH, D_MODEL), CONV_WIDTH ** -0.5),
        'w_conv_out': nrm(ks[10], (N_CONV_LAYERS, D_MODEL, D_MODEL), D_MODEL ** -0.5),
        'w_qkv': nrm(ks[11], (N_ATTN_LAYERS, D_MODEL, QKV_DIM), D_MODEL ** -0.5),
        'w_attn_out': nrm(ks[12], (N_ATTN_LAYERS, N_HEADS * HEAD_DIM, D_MODEL), (N_HEADS * HEAD_DIM) ** -0.5),
        'attn_sinks': nrm(ks[13], (N_ATTN_LAYERS, N_HEADS)),
    }


def reference(x_prompt, x_sample, state_conv, cache_k, cache_v, norm_g, w_ffn_in, w_ffn_out,
              w_conv_in, w_conv, w_conv_out, w_qkv, w_attn_out, attn_sinks):
    y_prompt, conv_p, k_p, v_p = trunk(x_prompt, None, None, None, norm_g, w_ffn_in, w_ffn_out,
                                       w_conv_in, w_conv, w_conv_out, w_qkv, w_attn_out,
                                       attn_sinks, True)
    y_sample, conv_s, k_s, v_s = trunk(x_sample, state_conv, cache_k, cache_v, norm_g, w_ffn_in,
                                       w_ffn_out, w_conv_in, w_conv, w_conv_out, w_qkv,
                                       w_attn_out, attn_sinks, False)
    return (y_prompt, y_sample, conv_p, conv_s, k_p, v_p, k_s, v_s)
```

```python
import functools

import jax
import jax.numpy as jnp
import numpy as np
from jax import lax
from jax.experimental import pallas as pl
from jax.experimental.pallas import tpu as pltpu

F32 = jnp.float32
BF16 = jnp.bfloat16

CHUNK = 64
LOOKBACK_CHUNKS = 2
WINDOW = LOOKBACK_CHUNKS * CHUNK
HEAD_DIM = 64
PAST_LEN = 2048
ROPE_THETA = 10000.0
NORM_EPS = 1e-6

LANES = 128
HEADS_PER_COL = LANES // HEAD_DIM
VMEM_LIMIT = 56 * 1024 * 1024
ROW_TILE = 512
FFN_TF = 512
CONV_TN = 512


def _rms(x, g):
    ms = jnp.mean(x * x, axis=-1, keepdims=True)
    return x * lax.rsqrt(ms + NORM_EPS) * g


def _params(*sem):
    return pltpu.CompilerParams(dimension_semantics=sem, vmem_limit_bytes=VMEM_LIMIT)


def _ffn_kernel(x_ref, gpre_ref, gpost_ref, wg_ref, wu_ref, wo_ref, o_ref, h_ref, acc_ref):
    j = pl.program_id(1)

    @pl.when(j == 0)
    def _():
        h_ref[...] = _rms(x_ref[...], gpre_ref[...]).astype(BF16)
        acc_ref[...] = jnp.zeros_like(acc_ref)

    h = h_ref[...]
    gate = jnp.dot(h, wg_ref[...], preferred_element_type=F32)
    up = jnp.dot(h, wu_ref[...], preferred_element_type=F32)
    act = (gate * jax.nn.sigmoid(gate) * up).astype(BF16)
    acc_ref[...] += jnp.dot(act, wo_ref[...], preferred_element_type=F32)

    @pl.when(j == pl.num_programs(1) - 1)
    def _():
        o_ref[...] = x_ref[...] + 0.5 * _rms(acc_ref[...], gpost_ref[...])


def _ffn(x, norm_g, w_in, w_out, layer, slot, n_pre, n_post, *, tm):
    m, d = x.shape
    nf, tf = w_in.shape[3], w_in.shape[5]
    g_spec = lambda n: pl.BlockSpec((None, None, 1, d), lambda i, j: (layer, n, 0, 0))
    w_spec = lambda sec: pl.BlockSpec((None, None, None, None, d, tf),
                                      lambda i, j: (layer, slot, sec, j, 0, 0))
    return pl.pallas_call(
        _ffn_kernel,
        grid=(m // tm, nf),
        in_specs=[
            pl.BlockSpec((tm, d), lambda i, j: (i, 0)),
            g_spec(n_pre),
            g_spec(n_post),
            w_spec(0),
            w_spec(1),
            pl.BlockSpec((None, None, tf, d), lambda i, j: (layer, slot, j, 0)),
        ],
        out_specs=pl.BlockSpec((tm, d), lambda i, j: (i, 0)),
        out_shape=jax.ShapeDtypeStruct((m, d), F32),
        scratch_shapes=[pltpu.VMEM((tm, d), BF16), pltpu.VMEM((tm, d), F32)],
        compiler_params=_params("parallel", "arbitrary"),
        name="ffn",
    )(x, norm_g, norm_g, w_in, w_in, w_out)


def _conv_kernel(x_ref, gpre_ref, gpost_ref, wb_ref, wc_ref, wh_ref, cw_ref, wo_ref, st_ref,
                 o_ref, ns_ref, h_ref, acc_ref, carry_ref, *, seg, carried):
    i = pl.program_id(0)
    n = pl.program_id(1)
    tm = x_ref.shape[0]
    tn = wb_ref.shape[1]
    nseg = tm // seg
    cols = pl.ds(pl.multiple_of(n * tn, tn), tn)

    @pl.when(n == 0)
    def _():
        h_ref[...] = _rms(x_ref[...], gpre_ref[...]).astype(BF16)
        acc_ref[...] = jnp.zeros_like(acc_ref)

    if carried:
        @pl.when(i == 0)
        def _():
            carry_ref[:, cols] = st_ref[0, :, cols]

    h = h_ref[...]
    b_gate = jnp.dot(h, wb_ref[...], preferred_element_type=F32)
    c_gate = jnp.dot(h, wc_ref[...], preferred_element_type=F32)
    hh = jnp.dot(h, wh_ref[...], preferred_element_type=F32)
    u = c_gate * hh
    cw = cw_ref[...]
    w0, w1, w2 = cw[0:1], cw[1:2], cw[2:3]
    row = lax.broadcasted_iota(jnp.int32, (seg, tn), 0)
    convs = []
    for s in range(nseg):
        us = u[s * seg:(s + 1) * seg]
        prev = carry_ref[:, cols] if carried else st_ref[s]
        p1 = jnp.where(row == 0, prev[1:2], pltpu.roll(us, 1, 0))
        p2 = jnp.where(row == 0, prev[0:1], jnp.where(row == 1, prev[1:2], pltpu.roll(us, 2, 0)))
        convs.append(p2 * w0 + p1 * w1 + us * w2)
        if carried:
            carry_ref[:, cols] = us[seg - 2:seg]
            ns_ref[0, :, cols] = us[seg - 2:seg]
        else:
            ns_ref[s] = us[seg - 2:seg]
    conv = convs[0] if nseg == 1 else jnp.concatenate(convs, axis=0)
    z = (b_gate * conv).astype(BF16)
    acc_ref[...] += jnp.dot(z, wo_ref[...], preferred_element_type=F32)

    @pl.when(n == pl.num_programs(1) - 1)
    def _():
        o_ref[...] = x_ref[...] + _rms(acc_ref[...], gpost_ref[...])


def _conv_mixer(x, state, norm_g, w_in, conv_w, w_out, layer, j, *, tm, seg, carried):
    m, d = x.shape
    nn, tn = w_in.shape[2], w_in.shape[4]
    nseg = tm // seg
    g_spec = lambda n: pl.BlockSpec((None, None, 1, d), lambda i, c: (layer, n, 0, 0))
    w_spec = lambda sec: pl.BlockSpec((None, None, None, d, tn), lambda i, c: (j, sec, c, 0, 0))
    if carried:
        st_spec = pl.BlockSpec((1, 2, d), lambda i, c: (0, 0, 0))
    else:
        st_spec = pl.BlockSpec((nseg, 2, tn), lambda i, c: (i, 0, c))
    return pl.pallas_call(
        functools.partial(_conv_kernel, seg=seg, carried=carried),
        grid=(m // tm, nn),
        in_specs=[
            pl.BlockSpec((tm, d), lambda i, c: (i, 0)),
            g_spec(2),
            g_spec(3),
            w_spec(0),
            w_spec(1),
            w_spec(2),
            pl.BlockSpec((None, 3, tn), lambda i, c: (j, 0, c)),
            pl.BlockSpec((None, tn, d), lambda i, c: (j, c, 0)),
            st_spec,
        ],
        out_specs=[pl.BlockSpec((tm, d), lambda i, c: (i, 0)), st_spec],
        out_shape=[
            jax.ShapeDtypeStruct((m, d), F32),
            jax.ShapeDtypeStruct(state.shape, F32),
        ],
        scratch_shapes=[
            pltpu.VMEM((tm, d), BF16),
            pltpu.VMEM((tm, d), F32),
            pltpu.VMEM((2, d), F32),
        ],
        compiler_params=_params("arbitrary", "arbitrary"),
        name="conv_mixer",
    )(x, norm_g, norm_g, w_in, w_in, w_in, conv_w, w_out, state)


def _rope_col(xc, cos, sin_signed):
    lane = lax.broadcasted_iota(jnp.int32, xc.shape, 1)
    first_half = (lane % HEAD_DIM) < HEAD_DIM // 2
    swapped = jnp.where(first_half,
                        pltpu.roll(xc, LANES - HEAD_DIM // 2, 1),
                        pltpu.roll(xc, HEAD_DIM // 2, 1))
    return xc * cos + swapped * sin_signed


def _qkv_kernel(x_ref, g_ref, w_ref, cos_ref, sin_ref, q_ref, k_ref, v_ref, h_ref, *, nq_tiles):
    n = pl.program_id(1)
    tn = w_ref.shape[1]
    ncol = tn // LANES

    @pl.when(n == 0)
    def _():
        h_ref[...] = _rms(x_ref[...], g_ref[...]).astype(BF16)

    y = jnp.dot(h_ref[...], w_ref[...], preferred_element_type=F32)
    cos = cos_ref[...]
    sin = sin_ref[...]

    @pl.when(n < nq_tiles)
    def _():
        for c in range(ncol):
            qc = _rope_col(y[:, c * LANES:(c + 1) * LANES], cos, sin) * (HEAD_DIM ** -0.5)
            q_ref[c] = qc.astype(BF16)

    @pl.when(n == nq_tiles)
    def _():
        nk = k_ref.shape[1] // LANES
        for c in range(nk):
            k_ref[:, c * LANES:(c + 1) * LANES] = _rope_col(y[:, c * LANES:(c + 1) * LANES], cos, sin)
        v_ref[...] = y[:, nk * LANES:]


def _qkv_project(x, norm_g, w_qkv, cos, sin, layer, j, *, tm, n_heads, n_kv):
    m, d = x.shape
    kv_w = n_kv * HEAD_DIM
    tn = w_qkv.shape[3]
    assert tn == 2 * kv_w
    nq_tiles = (n_heads * HEAD_DIM) // tn
    ncol = tn // LANES
    return pl.pallas_call(
        functools.partial(_qkv_kernel, nq_tiles=nq_tiles),
        grid=(m // tm, nq_tiles + 1),
        in_specs=[
            pl.BlockSpec((tm, d), lambda i, c: (i, 0)),
            pl.BlockSpec((None, None, 1, d), lambda i, c: (layer, 2, 0, 0)),
            pl.BlockSpec((None, None, d, tn), lambda i, c: (j, c, 0, 0)),
            pl.BlockSpec((tm, LANES), lambda i, c: (i, 0)),
            pl.BlockSpec((tm, LANES), lambda i, c: (i, 0)),
        ],
        out_specs=[
            pl.BlockSpec((ncol, tm, LANES), lambda i, c: (jnp.minimum(c, nq_tiles - 1), i, 0)),
            pl.BlockSpec((tm, kv_w), lambda i, c: (i, 0)),
            pl.BlockSpec((tm, kv_w), lambda i, c: (i, 0)),
        ],
        out_shape=[
            jax.ShapeDtypeStruct((n_heads // HEADS_PER_COL, m, LANES), BF16),
            jax.ShapeDtypeStruct((m, kv_w), F32),
            jax.ShapeDtypeStruct((m, kv_w), F32),
        ],
        scratch_shapes=[pltpu.VMEM((tm, d), BF16)],
        compiler_params=_params("parallel", "arbitrary"),
        name="qkv_rope",
    )(x, norm_g, w_qkv, cos, sin)


def _block_diag(col, rolled, first):
    lane = lax.broadcasted_iota(jnp.int32, col.shape, 1)
    lo = lane < HEAD_DIM
    zero = jnp.zeros_like(col)
    if first:
        return jnp.concatenate([jnp.where(lo, col, zero), jnp.where(lo, zero, rolled)], axis=0)
    return jnp.concatenate([jnp.where(lo, rolled, zero), jnp.where(lo, zero, col)], axis=0)


def _attn_kernel(sink_ref, q_ref, kp_ref, kc_ref, vp_ref, vc_ref, x_ref, g_ref, wo_ref, o_ref,
                 att_ref, *, sq, n_kv, streaming, q_pos0, k_pos0):
    i = pl.program_id(0)
    tq = x_ref.shape[0]
    nsub = tq // sq
    nk = WINDOW + sq
    n_cols = q_ref.shape[0]
    cols_per_kv = n_cols // n_kv

    if streaming:
        k_all = jnp.concatenate([kp_ref[...], kc_ref[...]], axis=0)
        v_all = jnp.concatenate([vp_ref[...], vc_ref[...]], axis=0)

    row = lax.broadcasted_iota(jnp.int32, (sq, nk), 0)
    colk = lax.broadcasted_iota(jnp.int32, (sq, nk), 1)

    for s in range(nsub):
        if streaming:
            q0 = i * tq + s * sq
            k0 = q0 - WINDOW
            ks = k_all[s * sq:s * sq + nk]
            vs = v_all[s * sq:s * sq + nk]
        else:
            q0 = q_pos0
            k0 = k_pos0
            ks = jnp.concatenate([kp_ref[s], kc_ref[s * sq:(s + 1) * sq]], axis=0)
            vs = jnp.concatenate([vp_ref[s], vc_ref[s * sq:(s + 1) * sq]], axis=0)
        kpos = k0 + colk
        q_chunk = lax.shift_right_arithmetic(q0 + row, 6)
        k_chunk = lax.shift_right_arithmetic(kpos, 6)
        valid = (k_chunk >= q_chunk - LOOKBACK_CHUNKS) & (k_chunk <= q_chunk)
        if streaming:
            valid = valid & (kpos >= 0)

        for kv in range(n_kv):
            c, first = divmod(kv, HEADS_PER_COL)
            kcol = ks[:, c * LANES:(c + 1) * LANES]
            vcol = vs[:, c * LANES:(c + 1) * LANES]
            bdk = _block_diag(kcol, pltpu.roll(kcol, HEAD_DIM, 1), first == 0).astype(BF16)
            bdv = _block_diag(vcol, pltpu.roll(vcol, HEAD_DIM, 1), first == 0).astype(BF16)
            qs = q_ref[kv * cols_per_kv:(kv + 1) * cols_per_kv, s * sq:(s + 1) * sq, :]
            qs = qs.reshape(cols_per_kv * sq, LANES)
            sc = lax.dot_general(qs, bdk, (((1,), (1,)), ((), ())), preferred_element_type=F32)
            p_rows = []
            for p in range(cols_per_kv):
                halves = []
                for e in range(HEADS_PER_COL):
                    sink = sink_ref[(kv * cols_per_kv + p) * HEADS_PER_COL + e]
                    blk = sc[p * sq:(p + 1) * sq, e * nk:(e + 1) * nk]
                    blk = jnp.where(valid, blk, -jnp.inf)
                    mx = jnp.maximum(jnp.max(blk, axis=-1, keepdims=True), sink)
                    ex = jnp.exp(blk - mx)
                    den = jnp.sum(ex, axis=-1, keepdims=True) + jnp.exp(sink - mx)
                    halves.append((ex / den).astype(BF16))
                p_rows.append(jnp.concatenate(halves, axis=1))
            pm = jnp.concatenate(p_rows, axis=0)
            ov = jnp.dot(pm, bdv, preferred_element_type=F32)
            for p in range(cols_per_kv):
                cc = kv * cols_per_kv + p
                att_ref[s * sq:(s + 1) * sq, cc * LANES:(cc + 1) * LANES] = (
                    ov[p * sq:(p + 1) * sq].astype(BF16))

    y = jnp.dot(att_ref[...], wo_ref[...], preferred_element_type=F32)
    o_ref[...] = x_ref[...] + _rms(y, g_ref[...])


def _attention(x, q, k_prev, k, v_prev, v, sinks, norm_g, w_o, layer, j, *, tq, sq, n_kv,
               streaming):
    m, d = x.shape
    n_cols = q.shape[0]
    kv_w = k.shape[1]
    nsub = tq // sq
    if streaming:
        per = tq // WINDOW
        prev_spec = pl.BlockSpec((WINDOW, kv_w), lambda i: (jnp.maximum(i * per - 1, 0), 0))
    else:
        prev_spec = pl.BlockSpec((nsub, WINDOW, kv_w), lambda i: (i, 0, 0))
    cur_spec = pl.BlockSpec((tq, kv_w), lambda i: (i, 0))
    kern = functools.partial(_attn_kernel, sq=sq, n_kv=n_kv, streaming=streaming,
                             q_pos0=PAST_LEN, k_pos0=PAST_LEN - WINDOW)
    return pl.pallas_call(
        kern,
        grid=(m // tq,),
        in_specs=[
            pl.BlockSpec(memory_space=pltpu.SMEM),
            pl.BlockSpec((n_cols, tq, LANES), lambda i: (0, i, 0)),
            prev_spec, cur_spec, prev_spec, cur_spec,
            pl.BlockSpec((tq, d), lambda i: (i, 0)),
            pl.BlockSpec((None, None, 1, d), lambda i: (layer, 3, 0, 0)),
            pl.BlockSpec((None, d, d), lambda i: (j, 0, 0)),
        ],
        out_specs=pl.BlockSpec((tq, d), lambda i: (i, 0)),
        out_shape=jax.ShapeDtypeStruct((m, d), F32),
        scratch_shapes=[pltpu.VMEM((tq, d), BF16)],
        compiler_params=_params("arbitrary"),
        name="swa_attention",
    )(sinks, q, k_prev, k, v_prev, v, x, norm_g, w_o)


def _rope_tables(pos):
    half = HEAD_DIM // 2
    inv_freq = ROPE_THETA ** (-jnp.arange(half, dtype=F32) / half)
    ang = pos.astype(F32)[:, None] * inv_freq[None, :]
    cos = jnp.cos(ang)
    sin = jnp.sin(ang)
    cos_t = jnp.tile(cos, (1, LANES // half))
    sin_t = jnp.tile(jnp.concatenate([-sin, sin], axis=1), (1, HEADS_PER_COL))
    return cos_t, sin_t


def kernel(x_prompt, x_sample, state_conv, cache_k, cache_v, norm_g, w_ffn_in, w_ffn_out,
           w_conv_in, w_conv, w_conv_out, w_qkv, w_attn_out, attn_sinks):
    batch, seq, d = x_prompt.shape
    dec_batch, dec_seq, _ = x_sample.shape
    depth = norm_g.shape[0]
    n_heads = attn_sinks.shape[1]
    n_kv = cache_k.shape[3]
    kv_w = n_kv * HEAD_DIM
    assert batch == 1 and dec_seq == CHUNK and cache_k.shape[2] == WINDOW

    g4 = norm_g.reshape(depth, norm_g.shape[1], 1, d)
    f = w_ffn_out.shape[2]
    w_ffn_in_b = jnp.transpose(
        w_ffn_in.astype(BF16).reshape(depth, 2, d, 2, f // FFN_TF, FFN_TF), (0, 1, 3, 4, 2, 5))
    w_ffn_out_b = w_ffn_out.astype(BF16)
    n_conv = w_conv_in.shape[0]
    w_conv_in_b = jnp.transpose(
        w_conv_in.astype(BF16).reshape(n_conv, d, 3, d // CONV_TN, CONV_TN), (0, 2, 3, 1, 4))
    w_conv_out_b = w_conv_out.astype(BF16)
    qkv_tn = 2 * kv_w
    w_qkv_b = jnp.transpose(
        w_qkv.astype(BF16).reshape(w_qkv.shape[0], d, w_qkv.shape[2] // qkv_tn, qkv_tn), (0, 2, 1, 3))
    w_attn_out_b = w_attn_out.astype(BF16)

    xp = x_prompt.reshape(seq, d)
    xs = x_sample.reshape(dec_batch * dec_seq, d)
    cos_p, sin_p = _rope_tables(jnp.arange(seq))
    cos_s, sin_s = _rope_tables(jnp.tile(PAST_LEN + jnp.arange(dec_seq), dec_batch))

    tm = ROW_TILE
    conv_p, conv_s, k_p, v_p, k_s, v_s = [], [], [], [], [], []
    for layer in range(depth):
        j = layer // 2
        ffn = functools.partial(_ffn, norm_g=g4, w_in=w_ffn_in_b, w_out=w_ffn_out_b, layer=layer,
                                tm=tm)
        xp = ffn(xp, slot=0, n_pre=0, n_post=1)
        xs = ffn(xs, slot=0, n_pre=0, n_post=1)
        if layer % 2 == 0:
            mix = functools.partial(_conv_mixer, norm_g=g4, w_in=w_conv_in_b, conv_w=w_conv,
                                    w_out=w_conv_out_b, layer=layer, j=j, tm=tm)
            xp, st = mix(xp, jnp.zeros((1, 2, d), F32), seg=tm, carried=True)
            conv_p.append(st)
            xs, st = mix(xs, state_conv[j], seg=dec_seq, carried=False)
            conv_s.append(st)
        else:
            proj = functools.partial(_qkv_project, norm_g=g4, w_qkv=w_qkv_b, layer=layer, j=j,
                                     tm=tm, n_heads=n_heads, n_kv=n_kv)
            att = functools.partial(_attention, sinks=attn_sinks[j], norm_g=g4, w_o=w_attn_out_b,
                                    layer=layer, j=j, n_kv=n_kv)
            q, k, v = proj(xp, cos=cos_p, sin=sin_p)
            xp = att(xp, q, k, k, v, v, tq=tm, sq=WINDOW, streaming=True)
            k_p.append(k[seq - WINDOW:].reshape(1, WINDOW, n_kv, HEAD_DIM))
            v_p.append(v[seq - WINDOW:].reshape(1, WINDOW, n_kv, HEAD_DIM))
            q, k, v = proj(xs, cos=cos_s, sin=sin_s)
            ck = cache_k[j].reshape(dec_batch, WINDOW, kv_w)
            cv = cache_v[j].reshape(dec_batch, WINDOW, kv_w)
            xs = att(xs, q, ck, k, cv, v, tq=tm, sq=dec_seq, streaming=False)
            k_new = jnp.concatenate([ck, k.reshape(dec_batch, dec_seq, kv_w)], axis=1)[:, -WINDOW:]
            v_new = jnp.concatenate([cv, v.reshape(dec_batch, dec_seq, kv_w)], axis=1)[:, -WINDOW:]
            k_s.append(k_new.reshape(dec_batch, WINDOW, n_kv, HEAD_DIM))
            v_s.append(v_new.reshape(dec_batch, WINDOW, n_kv, HEAD_DIM))
        xp = ffn(xp, slot=1, n_pre=4, n_post=5)
        xs = ffn(xs, slot=1, n_pre=4, n_post=5)

    return (xp.reshape(batch, seq, d), xs.reshape(dec_batch, dec_seq, d),
            jnp.stack(conv_p), jnp.stack(conv_s),
            jnp.stack(k_p), jnp.stack(v_p), jnp.stack(k_s), jnp.stack(v_s))
```

```python
import functools

import jax
import jax.numpy as jnp
import numpy as np
from jax import lax
from jax.experimental import pallas as pl
from jax.experimental.pallas import tpu as pltpu

F32 = jnp.float32
BF16 = jnp.bfloat16

CHUNK = 64
LOOKBACK_CHUNKS = 2
WINDOW = LOOKBACK_CHUNKS * CHUNK
HEAD_DIM = 64
PAST_LEN = 2048
ROPE_THETA = 10000.0
NORM_EPS = 1e-6

LANES = 128
HEADS_PER_COL = LANES // HEAD_DIM
VMEM_LIMIT = 60 * 1024 * 1024
ROW_TILE = 512
FFN_TM = 1024
FFN_TF = 256
CONV_TN = 512


def _rms(x, g):
    ms = jnp.mean(x * x, axis=-1, keepdims=True)
    return x * lax.rsqrt(ms + NORM_EPS) * g


def _params(*sem):
    return pltpu.CompilerParams(dimension_semantics=sem, vmem_limit_bytes=VMEM_LIMIT)


def _ffn_kernel(x_ref, gpre_ref, gpost_ref, wg_ref, wu_ref, wo_ref, o_ref, h_ref, acc_ref):
    j = pl.program_id(1)

    @pl.when(j == 0)
    def _():
        h_ref[...] = _rms(x_ref[...], gpre_ref[...]).astype(BF16)
        acc_ref[...] = jnp.zeros_like(acc_ref)

    h = h_ref[...]
    gate = jnp.dot(h, wg_ref[...], preferred_element_type=F32)
    up = jnp.dot(h, wu_ref[...], preferred_element_type=F32)
    act = (gate * jax.nn.sigmoid(gate) * up).astype(BF16)
    acc_ref[...] += jnp.dot(act, wo_ref[...], preferred_element_type=F32)

    @pl.when(j == pl.num_programs(1) - 1)
    def _():
        o_ref[...] = x_ref[...] + 0.5 * _rms(acc_ref[...], gpost_ref[...])


def _ffn(x, norm_g, w_in, w_out, layer, slot, n_pre, n_post, *, tm):
    m, d = x.shape
    nf, tf = w_in.shape[3], w_in.shape[5]
    g_spec = lambda n: pl.BlockSpec((None, None, 1, d), lambda i, j: (layer, n, 0, 0))
    w_spec = lambda sec: pl.BlockSpec((None, None, None, None, d, tf),
                                      lambda i, j: (layer, slot, sec, j, 0, 0))
    return pl.pallas_call(
        _ffn_kernel,
        grid=(m // tm, nf),
        in_specs=[
            pl.BlockSpec((tm, d), lambda i, j: (i, 0)),
            g_spec(n_pre),
            g_spec(n_post),
            w_spec(0),
            w_spec(1),
            pl.BlockSpec((None, None, tf, d), lambda i, j: (layer, slot, j, 0)),
        ],
        out_specs=pl.BlockSpec((tm, d), lambda i, j: (i, 0)),
        out_shape=jax.ShapeDtypeStruct((m, d), F32),
        scratch_shapes=[pltpu.VMEM((tm, d), BF16), pltpu.VMEM((tm, d), F32)],
        compiler_params=_params("parallel", "arbitrary"),
        name="ffn",
    )(x, norm_g, norm_g, w_in, w_in, w_out)


def _conv_kernel(x_ref, gpre_ref, gpost_ref, wb_ref, wc_ref, wh_ref, cw_ref, wo_ref, st_ref,
                 o_ref, ns_ref, h_ref, acc_ref, carry_ref, *, seg, carried):
    i = pl.program_id(0)
    n = pl.program_id(1)
    tm = x_ref.shape[0]
    tn = wb_ref.shape[1]
    nseg = tm // seg
    cols = pl.ds(pl.multiple_of(n * tn, tn), tn)

    @pl.when(n == 0)
    def _():
        h_ref[...] = _rms(x_ref[...], gpre_ref[...]).astype(BF16)
        acc_ref[...] = jnp.zeros_like(acc_ref)

    if carried:
        @pl.when(i == 0)
        def _():
            carry_ref[:, cols] = st_ref[0, :, cols]

    h = h_ref[...]
    b_gate = jnp.dot(h, wb_ref[...], preferred_element_type=F32)
    c_gate = jnp.dot(h, wc_ref[...], preferred_element_type=F32)
    hh = jnp.dot(h, wh_ref[...], preferred_element_type=F32)
    u = c_gate * hh
    cw = cw_ref[...]
    w0, w1, w2 = cw[0:1], cw[1:2], cw[2:3]
    row = lax.broadcasted_iota(jnp.int32, (seg, tn), 0)
    convs = []
    for s in range(nseg):
        us = u[s * seg:(s + 1) * seg]
        prev = carry_ref[:, cols] if carried else st_ref[s]
        p1 = jnp.where(row == 0, prev[1:2], pltpu.roll(us, 1, 0))
        p2 = jnp.where(row == 0, prev[0:1], jnp.where(row == 1, prev[1:2], pltpu.roll(us, 2, 0)))
        convs.append(p2 * w0 + p1 * w1 + us * w2)
        if carried:
            carry_ref[:, cols] = us[seg - 2:seg]
            ns_ref[0, :, cols] = us[seg - 2:seg]
        else:
            ns_ref[s] = us[seg - 2:seg]
    conv = convs[0] if nseg == 1 else jnp.concatenate(convs, axis=0)
    z = (b_gate * conv).astype(BF16)
    acc_ref[...] += jnp.dot(z, wo_ref[...], preferred_element_type=F32)

    @pl.when(n == pl.num_programs(1) - 1)
    def _():
        o_ref[...] = x_ref[...] + _rms(acc_ref[...], gpost_ref[...])


def _conv_mixer(x, state, norm_g, w_in, conv_w, w_out, layer, j, *, tm, seg, carried):
    m, d = x.shape
    nn, tn = w_in.shape[2], w_in.shape[4]
    nseg = tm // seg
    g_spec = lambda n: pl.BlockSpec((None, None, 1, d), lambda i, c: (layer, n, 0, 0))
    w_spec = lambda sec: pl.BlockSpec((None, None, None, d, tn), lambda i, c: (j, sec, c, 0, 0))
    if carried:
        st_spec = pl.BlockSpec((1, 2, d), lambda i, c: (0, 0, 0))
    else:
        st_spec = pl.BlockSpec((nseg, 2, tn), lambda i, c: (i, 0, c))
    return pl.pallas_call(
        functools.partial(_conv_kernel, seg=seg, carried=carried),
        grid=(m // tm, nn),
        in_specs=[
            pl.BlockSpec((tm, d), lambda i, c: (i, 0)),
            g_spec(2),
            g_spec(3),
            w_spec(0),
            w_spec(1),
            w_spec(2),
            pl.BlockSpec((None, 3, tn), lambda i, c: (j, 0, c)),
            pl.BlockSpec((None, tn, d), lambda i, c: (j, c, 0)),
            st_spec,
        ],
        out_specs=[pl.BlockSpec((tm, d), lambda i, c: (i, 0)), st_spec],
        out_shape=[
            jax.ShapeDtypeStruct((m, d), F32),
            jax.ShapeDtypeStruct(state.shape, F32),
        ],
        scratch_shapes=[
            pltpu.VMEM((tm, d), BF16),
            pltpu.VMEM((tm, d), F32),
            pltpu.VMEM((2, d), F32),
        ],
        compiler_params=_params("arbitrary", "arbitrary"),
        name="conv_mixer",
    )(x, norm_g, norm_g, w_in, w_in, w_in, conv_w, w_out, state)


def _rope_col(xc, cos, sin_signed):
    lane = lax.broadcasted_iota(jnp.int32, xc.shape, 1)
    first_half = (lane % HEAD_DIM) < HEAD_DIM // 2
    swapped = jnp.where(first_half,
                        pltpu.roll(xc, LANES - HEAD_DIM // 2, 1),
                        pltpu.roll(xc, HEAD_DIM // 2, 1))
    return xc * cos + swapped * sin_signed


def _qkv_kernel(x_ref, g_ref, w_ref, cos_ref, sin_ref, q_ref, k_ref, v_ref, h_ref, *, nq_tiles):
    n = pl.program_id(1)
    tn = w_ref.shape[1]
    ncol = tn // LANES

    @pl.when(n == 0)
    def _():
        h_ref[...] = _rms(x_ref[...], g_ref[...]).astype(BF16)

    y = jnp.dot(h_ref[...], w_ref[...], preferred_element_type=F32)
    cos = cos_ref[...]
    sin = sin_ref[...]

    @pl.when(n < nq_tiles)
    def _():
        for c in range(ncol):
            qc = _rope_col(y[:, c * LANES:(c + 1) * LANES], cos, sin) * (HEAD_DIM ** -0.5)
            q_ref[c] = qc.astype(BF16)

    @pl.when(n == nq_tiles)
    def _():
        nk = k_ref.shape[1] // LANES
        for c in range(nk):
            k_ref[:, c * LANES:(c + 1) * LANES] = _rope_col(y[:, c * LANES:(c + 1) * LANES], cos, sin)
        v_ref[...] = y[:, nk * LANES:]


def _qkv_project(x, norm_g, w_qkv, cos, sin, layer, j, *, tm, n_heads, n_kv):
    m, d = x.shape
    kv_w = n_kv * HEAD_DIM
    tn = w_qkv.shape[3]
    assert tn == 2 * kv_w
    nq_tiles = (n_heads * HEAD_DIM) // tn
    ncol = tn // LANES
    return pl.pallas_call(
        functools.partial(_qkv_kernel, nq_tiles=nq_tiles),
        grid=(m // tm, nq_tiles + 1),
        in_specs=[
            pl.BlockSpec((tm, d), lambda i, c: (i, 0)),
            pl.BlockSpec((None, None, 1, d), lambda i, c: (layer, 2, 0, 0)),
            pl.BlockSpec((None, None, d, tn), lambda i, c: (j, c, 0, 0)),
            pl.BlockSpec((tm, LANES), lambda i, c: (i, 0)),
            pl.BlockSpec((tm, LANES), lambda i, c: (i, 0)),
        ],
        out_specs=[
            pl.BlockSpec((ncol, tm, LANES), lambda i, c: (jnp.minimum(c, nq_tiles - 1), i, 0)),
            pl.BlockSpec((tm, kv_w), lambda i, c: (i, 0)),
            pl.BlockSpec((tm, kv_w), lambda i, c: (i, 0)),
        ],
        out_shape=[
            jax.ShapeDtypeStruct((n_heads // HEADS_PER_COL, m, LANES), BF16),
            jax.ShapeDtypeStruct((m, kv_w), F32),
            jax.ShapeDtypeStruct((m, kv_w), F32),
        ],
        scratch_shapes=[pltpu.VMEM((tm, d), BF16)],
        compiler_params=_params("parallel", "arbitrary"),
        name="qkv_rope",
    )(x, norm_g, w_qkv, cos, sin)


def _block_diag(col, rolled, first):
    lane = lax.broadcasted_iota(jnp.int32, col.shape, 1)
    lo = lane < HEAD_DIM
    zero = jnp.zeros_like(col)
    if first:
        return jnp.concatenate([jnp.where(lo, col, zero), jnp.where(lo, zero, rolled)], axis=0)
    return jnp.concatenate([jnp.where(lo, rolled, zero), jnp.where(lo, zero, col)], axis=0)


def _attn_kernel(sink_ref, q_ref, kp_ref, kc_ref, vp_ref, vc_ref, x_ref, g_ref, wo_ref, o_ref,
                 att_ref, *, sq, n_kv, streaming, q_pos0, k_pos0):
    i = pl.program_id(0)
    tq = x_ref.shape[0]
    nsub = tq // sq
    nk = WINDOW + sq
    n_cols = q_ref.shape[0]
    cols_per_kv = n_cols // n_kv

    if streaming:
        k_all = jnp.concatenate([kp_ref[...], kc_ref[...]], axis=0)
        v_all = jnp.concatenate([vp_ref[...], vc_ref[...]], axis=0)

    row = lax.broadcasted_iota(jnp.int32, (sq, nk), 0)
    colk = lax.broadcasted_iota(jnp.int32, (sq, nk), 1)

    for s in range(nsub):
        if streaming:
            q0 = i * tq + s * sq
            k0 = q0 - WINDOW
            ks = k_all[s * sq:s * sq + nk]
            vs = v_all[s * sq:s * sq + nk]
        else:
            q0 = q_pos0
            k0 = k_pos0
            ks = jnp.concatenate([kp_ref[s], kc_ref[s * sq:(s + 1) * sq]], axis=0)
            vs = jnp.concatenate([vp_ref[s], vc_ref[s * sq:(s + 1) * sq]], axis=0)
        kpos = k0 + colk
        q_chunk = lax.shift_right_arithmetic(q0 + row, 6)
        k_chunk = lax.shift_right_arithmetic(kpos, 6)
        valid = (k_chunk >= q_chunk - LOOKBACK_CHUNKS) & (k_chunk <= q_chunk)
        if streaming:
            valid = valid & (kpos >= 0)

        for kv in range(n_kv):
            c, first = divmod(kv, HEADS_PER_COL)
            kcol = ks[:, c * LANES:(c + 1) * LANES]
            vcol = vs[:, c * LANES:(c + 1) * LANES]
            bdk = _block_diag(kcol, pltpu.roll(kcol, HEAD_DIM, 1), first == 0).astype(BF16)
            bdv = _block_diag(vcol, pltpu.roll(vcol, HEAD_DIM, 1), first == 0).astype(BF16)
            qs = q_ref[kv * cols_per_kv:(kv + 1) * cols_per_kv, s * sq:(s + 1) * sq, :]
            qs = qs.reshape(cols_per_kv * sq, LANES)
            sc = lax.dot_general(qs, bdk, (((1,), (1,)), ((), ())), preferred_element_type=F32)
            p_rows = []
            for p in range(cols_per_kv):
                halves = []
                for e in range(HEADS_PER_COL):
                    sink = sink_ref[(kv * cols_per_kv + p) * HEADS_PER_COL + e]
                    blk = sc[p * sq:(p + 1) * sq, e * nk:(e + 1) * nk]
                    blk = jnp.where(valid, blk, -jnp.inf)
                    mx = jnp.maximum(jnp.max(blk, axis=-1, keepdims=True), sink)
                    ex = jnp.exp(blk - mx)
                    den = jnp.sum(ex, axis=-1, keepdims=True) + jnp.exp(sink - mx)
                    halves.append((ex / den).astype(BF16))
                p_rows.append(jnp.concatenate(halves, axis=1))
            pm = jnp.concatenate(p_rows, axis=0)
            ov = jnp.dot(pm, bdv, preferred_element_type=F32)
            for p in range(cols_per_kv):
                cc = kv * cols_per_kv + p
                att_ref[s * sq:(s + 1) * sq, cc * LANES:(cc + 1) * LANES] = (
                    ov[p * sq:(p + 1) * sq].astype(BF16))

    y = jnp.dot(att_ref[...], wo_ref[...], preferred_element_type=F32)
    o_ref[...] = x_ref[...] + _rms(y, g_ref[...])


def _attention(x, q, k_prev, k, v_prev, v, sinks, norm_g, w_o, layer, j, *, tq, sq, n_kv,
               streaming):
    m, d = x.shape
    n_cols = q.shape[0]
    kv_w = k.shape[1]
    nsub = tq // sq
    if streaming:
        per = tq // WINDOW
        prev_spec = pl.BlockSpec((WINDOW, kv_w), lambda i: (jnp.maximum(i * per - 1, 0), 0))
    else:
        prev_spec = pl.BlockSpec((nsub, WINDOW, kv_w), lambda i: (i, 0, 0))
    cur_spec = pl.BlockSpec((tq, kv_w), lambda i: (i, 0))
    kern = functools.partial(_attn_kernel, sq=sq, n_kv=n_kv, streaming=streaming,
                             q_pos0=PAST_LEN, k_pos0=PAST_LEN - WINDOW)
    return pl.pallas_call(
        kern,
        grid=(m // tq,),
        in_specs=[
            pl.BlockSpec(memory_space=pltpu.SMEM),
            pl.BlockSpec((n_cols, tq, LANES), lambda i: (0, i, 0)),
            prev_spec, cur_spec, prev_spec, cur_spec,
            pl.BlockSpec((tq, d), lambda i: (i, 0)),
            pl.BlockSpec((None, None, 1, d), lambda i: (layer, 3, 0, 0)),
            pl.BlockSpec((None, d, d), lambda i: (j, 0, 0)),
        ],
        out_specs=pl.BlockSpec((tq, d), lambda i: (i, 0)),
        out_shape=jax.ShapeDtypeStruct((m, d), F32),
        scratch_shapes=[pltpu.VMEM((tq, d), BF16)],
        compiler_params=_params("arbitrary"),
        name="swa_attention",
    )(sinks, q, k_prev, k, v_prev, v, x, norm_g, w_o)


def _rope_tables(pos):
    half = HEAD_DIM // 2
    inv_freq = ROPE_THETA ** (-jnp.arange(half, dtype=F32) / half)
    ang = pos.astype(F32)[:, None] * inv_freq[None, :]
    cos = jnp.cos(ang)
    sin = jnp.sin(ang)
    cos_t = jnp.tile(cos, (1, LANES // half))
    sin_t = jnp.tile(jnp.concatenate([-sin, sin], axis=1), (1, HEADS_PER_COL))
    return cos_t, sin_t


def kernel(x_prompt, x_sample, state_conv, cache_k, cache_v, norm_g, w_ffn_in, w_ffn_out,
           w_conv_in, w_conv, w_conv_out, w_qkv, w_attn_out, attn_sinks):
    batch, seq, d = x_prompt.shape
    dec_batch, dec_seq, _ = x_sample.shape
    depth = norm_g.shape[0]
    n_heads = attn_sinks.shape[1]
    n_kv = cache_k.shape[3]
    kv_w = n_kv * HEAD_DIM
    assert batch == 1 and dec_seq == CHUNK and cache_k.shape[2] == WINDOW

    g4 = norm_g.reshape(depth, norm_g.shape[1], 1, d)
    f = w_ffn_out.shape[2]
    w_ffn_in_b = jnp.transpose(
        w_ffn_in.astype(BF16).reshape(depth, 2, d, 2, f // FFN_TF, FFN_TF), (0, 1, 3, 4, 2, 5))
    w_ffn_out_b = w_ffn_out.astype(BF16)
    n_conv = w_conv_in.shape[0]
    w_conv_in_b = jnp.transpose(
        w_conv_in.astype(BF16).reshape(n_conv, d, 3, d // CONV_TN, CONV_TN), (0, 2, 3, 1, 4))
    w_conv_out_b = w_conv_out.astype(BF16)
    qkv_tn = 2 * kv_w
    w_qkv_b = jnp.transpose(
        w_qkv.astype(BF16).reshape(w_qkv.shape[0], d, w_qkv.shape[2] // qkv_tn, qkv_tn), (0, 2, 1, 3))
    w_attn_out_b = w_attn_out.astype(BF16)

    xp = x_prompt.reshape(seq, d)
    xs = x_sample.reshape(dec_batch * dec_seq, d)
    cos_p, sin_p = _rope_tables(jnp.arange(seq))
    cos_s, sin_s = _rope_tables(jnp.tile(PAST_LEN + jnp.arange(dec_seq), dec_batch))

    tm = ROW_TILE
    conv_p, conv_s, k_p, v_p, k_s, v_s = [], [], [], [], [], []
    for layer in range(depth):
        j = layer // 2
        ffn = functools.partial(_ffn, norm_g=g4, w_in=w_ffn_in_b, w_out=w_ffn_out_b, layer=layer,
                                tm=FFN_TM)
        xp = ffn(xp, slot=0, n_pre=0, n_post=1)
        xs = ffn(xs, slot=0, n_pre=0, n_post=1)
        if layer % 2 == 0:
            mix = functools.partial(_conv_mixer, norm_g=g4, w_in=w_conv_in_b, conv_w=w_conv,
                                    w_out=w_conv_out_b, layer=layer, j=j, tm=tm)
            xp, st = mix(xp, jnp.zeros((1, 2, d), F32), seg=tm, carried=True)
            conv_p.append(st)
            xs, st = mix(xs, state_conv[j], seg=dec_seq, carried=False)
            conv_s.append(st)
        else:
            proj = functools.partial(_qkv_project, norm_g=g4, w_qkv=w_qkv_b, layer=layer, j=j,
                                     tm=tm, n_heads=n_heads, n_kv=n_kv)
            att = functools.partial(_attention, sinks=attn_sinks[j], norm_g=g4, w_o=w_attn_out_b,
                                    layer=layer, j=j, n_kv=n_kv)
            q, k, v = proj(xp, cos=cos_p, sin=sin_p)
            xp = att(xp, q, k, k, v, v, tq=tm, sq=WINDOW, streaming=True)
            k_p.append(k[seq - WINDOW:].reshape(1, WINDOW, n_kv, HEAD_DIM))
            v_p.append(v[seq - WINDOW:].reshape(1, WINDOW, n_kv, HEAD_DIM))
            q, k, v = proj(xs, cos=cos_s, sin=sin_s)
            ck = cache_k[j].reshape(dec_batch, WINDOW, kv_w)
            cv = cache_v[j].reshape(dec_batch, WINDOW, kv_w)
            xs = att(xs, q, ck, k, cv, v, tq=tm, sq=dec_seq, streaming=False)
            k_new = jnp.concatenate([ck, k.reshape(dec_batch, dec_seq, kv_w)], axis=1)[:, -WINDOW:]
            v_new = jnp.concatenate([cv, v.reshape(dec_batch, dec_seq, kv_w)], axis=1)[:, -WINDOW:]
            k_s.append(k_new.reshape(dec_batch, WINDOW, n_kv, HEAD_DIM))
            v_s.append(v_new.reshape(dec_batch, WINDOW, n_kv, HEAD_DIM))
        xp = ffn(xp, slot=1, n_pre=4, n_post=5)
        xs = ffn(xs, slot=1, n_pre=4, n_post=5)

    return (xp.reshape(batch, seq, d), xs.reshape(dec_batch, dec_seq, d),
            jnp.stack(conv_p), jnp.stack(conv_s),
            jnp.stack(k_p), jnp.stack(v_p), jnp.stack(k_s), jnp.stack(v_s))
```

```python
import functools

import jax
import jax.numpy as jnp
import numpy as np
from jax import lax
from jax.experimental import pallas as pl
from jax.experimental.pallas import tpu as pltpu

F32 = jnp.float32
BF16 = jnp.bfloat16

CHUNK = 64
LOOKBACK_CHUNKS = 2
WINDOW = LOOKBACK_CHUNKS * CHUNK
HEAD_DIM = 64
PAST_LEN = 2048
ROPE_THETA = 10000.0
NORM_EPS = 1e-6

LANES = 128
HEADS_PER_COL = LANES // HEAD_DIM
VMEM_LIMIT = 60 * 1024 * 1024
ROW_TILE = 512
FFN_TM = 1024
FFN_TF = 256
CONV_TN = 512


def _rms(x, g):
    ms = jnp.mean(x * x, axis=-1, keepdims=True)
    return x * lax.rsqrt(ms + NORM_EPS) * g


def _params(*sem):
    return pltpu.CompilerParams(dimension_semantics=sem, vmem_limit_bytes=VMEM_LIMIT)


def _ffn_kernel(x_ref, gpre_ref, gpost_ref, wg_ref, wu_ref, wo_ref, o_ref, h_ref, acc_ref):
    j = pl.program_id(1)

    @pl.when(j == 0)
    def _():
        h_ref[...] = _rms(x_ref[...], gpre_ref[...]).astype(BF16)
        acc_ref[...] = jnp.zeros_like(acc_ref)

    h = h_ref[...]
    gate = jnp.dot(h, wg_ref[...], preferred_element_type=F32)
    up = jnp.dot(h, wu_ref[...], preferred_element_type=F32)
    act = (gate * jax.nn.sigmoid(gate) * up).astype(BF16)
    acc_ref[...] += jnp.dot(act, wo_ref[...], preferred_element_type=F32)

    @pl.when(j == pl.num_programs(1) - 1)
    def _():
        o_ref[...] = x_ref[...] + 0.5 * _rms(acc_ref[...], gpost_ref[...])


def _ffn(x, norm_g, w_in, w_out, layer, slot, n_pre, n_post, *, tm, tf):
    m, d = x.shape
    nf = w_out.shape[2] // tf
    g_spec = lambda n: pl.BlockSpec((None, None, 1, d), lambda i, j: (layer, n, 0, 0))
    w_spec = lambda sec: pl.BlockSpec((None, None, d, tf), lambda i, j: (layer, slot, 0, j + sec * nf))
    return pl.pallas_call(
        _ffn_kernel,
        grid=(m // tm, nf),
        in_specs=[
            pl.BlockSpec((tm, d), lambda i, j: (i, 0)),
            g_spec(n_pre),
            g_spec(n_post),
            w_spec(0),
            w_spec(1),
            pl.BlockSpec((None, None, tf, d), lambda i, j: (layer, slot, j, 0)),
        ],
        out_specs=pl.BlockSpec((tm, d), lambda i, j: (i, 0)),
        out_shape=jax.ShapeDtypeStruct((m, d), F32),
        scratch_shapes=[pltpu.VMEM((tm, d), BF16), pltpu.VMEM((tm, d), F32)],
        compiler_params=_params("parallel", "arbitrary"),
        name="ffn",
    )(x, norm_g, norm_g, w_in, w_in, w_out)


def _conv_kernel(x_ref, gpre_ref, gpost_ref, wb_ref, wc_ref, wh_ref, cw_ref, wo_ref, st_ref,
                 o_ref, ns_ref, h_ref, acc_ref, carry_ref, *, seg, carried):
    i = pl.program_id(0)
    n = pl.program_id(1)
    tm = x_ref.shape[0]
    tn = wb_ref.shape[1]
    nseg = tm // seg
    cols = pl.ds(pl.multiple_of(n * tn, tn), tn)

    @pl.when(n == 0)
    def _():
        h_ref[...] = _rms(x_ref[...], gpre_ref[...]).astype(BF16)
        acc_ref[...] = jnp.zeros_like(acc_ref)

    if carried:
        @pl.when(i == 0)
        def _():
            carry_ref[:, cols] = st_ref[0, :, cols]

    h = h_ref[...]
    b_gate = jnp.dot(h, wb_ref[...], preferred_element_type=F32)
    c_gate = jnp.dot(h, wc_ref[...], preferred_element_type=F32)
    hh = jnp.dot(h, wh_ref[...], preferred_element_type=F32)
    u = c_gate * hh
    cw = cw_ref[...]
    w0, w1, w2 = cw[0:1], cw[1:2], cw[2:3]
    row = lax.broadcasted_iota(jnp.int32, (seg, tn), 0)
    convs = []
    for s in range(nseg):
        us = u[s * seg:(s + 1) * seg]
        prev = carry_ref[:, cols] if carried else st_ref[s]
        p1 = jnp.where(row == 0, prev[1:2], pltpu.roll(us, 1, 0))
        p2 = jnp.where(row == 0, prev[0:1], jnp.where(row == 1, prev[1:2], pltpu.roll(us, 2, 0)))
        convs.append(p2 * w0 + p1 * w1 + us * w2)
        if carried:
            carry_ref[:, cols] = us[seg - 2:seg]
            ns_ref[0, :, cols] = us[seg - 2:seg]
        else:
            ns_ref[s] = us[seg - 2:seg]
    conv = convs[0] if nseg == 1 else jnp.concatenate(convs, axis=0)
    z = (b_gate * conv).astype(BF16)
    acc_ref[...] += jnp.dot(z, wo_ref[...], preferred_element_type=F32)

    @pl.when(n == pl.num_programs(1) - 1)
    def _():
        o_ref[...] = x_ref[...] + _rms(acc_ref[...], gpost_ref[...])


def _conv_mixer(x, state, norm_g, w_in, conv_w, w_out, layer, j, *, tm, tn, seg, carried):
    m, d = x.shape
    nn = d // tn
    nseg = tm // seg
    g_spec = lambda n: pl.BlockSpec((None, None, 1, d), lambda i, c: (layer, n, 0, 0))
    w_spec = lambda sec: pl.BlockSpec((None, d, tn), lambda i, c: (j, 0, c + sec * nn))
    if carried:
        st_spec = pl.BlockSpec((1, 2, d), lambda i, c: (0, 0, 0))
    else:
        st_spec = pl.BlockSpec((nseg, 2, tn), lambda i, c: (i, 0, c))
    return pl.pallas_call(
        functools.partial(_conv_kernel, seg=seg, carried=carried),
        grid=(m // tm, nn),
        in_specs=[
            pl.BlockSpec((tm, d), lambda i, c: (i, 0)),
            g_spec(2),
            g_spec(3),
            w_spec(0),
            w_spec(1),
            w_spec(2),
            pl.BlockSpec((None, 3, tn), lambda i, c: (j, 0, c)),
            pl.BlockSpec((None, tn, d), lambda i, c: (j, c, 0)),
            st_spec,
        ],
        out_specs=[pl.BlockSpec((tm, d), lambda i, c: (i, 0)), st_spec],
        out_shape=[
            jax.ShapeDtypeStruct((m, d), F32),
            jax.ShapeDtypeStruct(state.shape, F32),
        ],
        scratch_shapes=[
            pltpu.VMEM((tm, d), BF16),
            pltpu.VMEM((tm, d), F32),
            pltpu.VMEM((2, d), F32),
        ],
        compiler_params=_params("arbitrary", "arbitrary"),
        name="conv_mixer",
    )(x, norm_g, norm_g, w_in, w_in, w_in, conv_w, w_out, state)


def _rope_col(xc, cos, sin_signed):
    lane = lax.broadcasted_iota(jnp.int32, xc.shape, 1)
    first_half = (lane % HEAD_DIM) < HEAD_DIM // 2
    swapped = jnp.where(first_half,
                        pltpu.roll(xc, LANES - HEAD_DIM // 2, 1),
                        pltpu.roll(xc, HEAD_DIM // 2, 1))
    return xc * cos + swapped * sin_signed


def _qkv_kernel(x_ref, g_ref, w_ref, cos_ref, sin_ref, q_ref, k_ref, v_ref, *, chunk):
    n_q = q_ref.shape[0] * LANES
    kv_w = k_ref.shape[1]
    h = _rms(x_ref[...], g_ref[...]).astype(BF16)
    cos = cos_ref[...]
    sin = sin_ref[...]
    for c0 in range(0, n_q, chunk):
        y = jnp.dot(h, w_ref[:, c0:c0 + chunk], preferred_element_type=F32)
        for c in range(chunk // LANES):
            qc = _rope_col(y[:, c * LANES:(c + 1) * LANES], cos, sin) * (HEAD_DIM ** -0.5)
            q_ref[c0 // LANES + c] = qc.astype(BF16)
    y = jnp.dot(h, w_ref[:, n_q:], preferred_element_type=F32)
    for c in range(kv_w // LANES):
        k_ref[:, c * LANES:(c + 1) * LANES] = _rope_col(y[:, c * LANES:(c + 1) * LANES], cos, sin)
    v_ref[...] = y[:, kv_w:]


def _qkv_project(x, norm_g, w_qkv, cos, sin, layer, j, *, tm, n_heads, n_kv):
    m, d = x.shape
    kv_w = n_kv * HEAD_DIM
    n_cols = n_heads // HEADS_PER_COL
    return pl.pallas_call(
        functools.partial(_qkv_kernel, chunk=2 * kv_w),
        grid=(m // tm,),
        in_specs=[
            pl.BlockSpec((tm, d), lambda i: (i, 0)),
            pl.BlockSpec((None, None, 1, d), lambda i: (layer, 2, 0, 0)),
            pl.BlockSpec((None, d, w_qkv.shape[2]), lambda i: (j, 0, 0)),
            pl.BlockSpec((tm, LANES), lambda i: (i, 0)),
            pl.BlockSpec((tm, LANES), lambda i: (i, 0)),
        ],
        out_specs=[
            pl.BlockSpec((n_cols, tm, LANES), lambda i: (0, i, 0)),
            pl.BlockSpec((tm, kv_w), lambda i: (i, 0)),
            pl.BlockSpec((tm, kv_w), lambda i: (i, 0)),
        ],
        out_shape=[
            jax.ShapeDtypeStruct((n_cols, m, LANES), BF16),
            jax.ShapeDtypeStruct((m, kv_w), F32),
            jax.ShapeDtypeStruct((m, kv_w), F32),
        ],
        compiler_params=_params("parallel"),
        name="qkv_rope",
    )(x, norm_g, w_qkv, cos, sin)


def _block_diag(col, rolled, first):
    lane = lax.broadcasted_iota(jnp.int32, col.shape, 1)
    lo = lane < HEAD_DIM
    zero = jnp.zeros_like(col)
    if first:
        return jnp.concatenate([jnp.where(lo, col, zero), jnp.where(lo, zero, rolled)], axis=0)
    return jnp.concatenate([jnp.where(lo, rolled, zero), jnp.where(lo, zero, col)], axis=0)


def _attn_kernel(sink_ref, q_ref, kp_ref, kc_ref, vp_ref, vc_ref, x_ref, g_ref, wo_ref, o_ref,
                 att_ref, *, sq, n_kv, streaming, q_pos0, k_pos0):
    i = pl.program_id(0)
    tq = x_ref.shape[0]
    nsub = tq // sq
    nk = WINDOW + sq
    n_cols = q_ref.shape[0]
    cols_per_kv = n_cols // n_kv

    if streaming:
        k_all = jnp.concatenate([kp_ref[...], kc_ref[...]], axis=0)
        v_all = jnp.concatenate([vp_ref[...], vc_ref[...]], axis=0)

    row = lax.broadcasted_iota(jnp.int32, (sq, nk), 0)
    colk = lax.broadcasted_iota(jnp.int32, (sq, nk), 1)

    for s in range(nsub):
        if streaming:
            q0 = i * tq + s * sq
            k0 = q0 - WINDOW
            ks = k_all[s * sq:s * sq + nk]
            vs = v_all[s * sq:s * sq + nk]
        else:
            q0 = q_pos0
            k0 = k_pos0
            ks = jnp.concatenate([kp_ref[s], kc_ref[s * sq:(s + 1) * sq]], axis=0)
            vs = jnp.concatenate([vp_ref[s], vc_ref[s * sq:(s + 1) * sq]], axis=0)
        kpos = k0 + colk
        q_chunk = lax.shift_right_arithmetic(q0 + row, 6)
        k_chunk = lax.shift_right_arithmetic(kpos, 6)
        valid = (k_chunk >= q_chunk - LOOKBACK_CHUNKS) & (k_chunk <= q_chunk)
        if streaming:
            valid = valid & (kpos >= 0)

        for kv in range(n_kv):
            c, first = divmod(kv, HEADS_PER_COL)
            kcol = ks[:, c * LANES:(c + 1) * LANES]
            vcol = vs[:, c * LANES:(c + 1) * LANES]
            bdk = _block_diag(kcol, pltpu.roll(kcol, HEAD_DIM, 1), first == 0).astype(BF16)
            bdv = _block_diag(vcol, pltpu.roll(vcol, HEAD_DIM, 1), first == 0).astype(BF16)
            qs = q_ref[kv * cols_per_kv:(kv + 1) * cols_per_kv, s * sq:(s + 1) * sq, :]
            qs = qs.reshape(cols_per_kv * sq, LANES)
            sc = lax.dot_general(qs, bdk, (((1,), (1,)), ((), ())), preferred_element_type=F32)
            p_rows = []
            for p in range(cols_per_kv):
                halves = []
                for e in range(HEADS_PER_COL):
                    sink = sink_ref[(kv * cols_per_kv + p) * HEADS_PER_COL + e]
                    blk = sc[p * sq:(p + 1) * sq, e * nk:(e + 1) * nk]
                    blk = jnp.where(valid, blk, -jnp.inf)
                    mx = jnp.maximum(jnp.max(blk, axis=-1, keepdims=True), sink)
                    ex = jnp.exp(blk - mx)
                    den = jnp.sum(ex, axis=-1, keepdims=True) + jnp.exp(sink - mx)
                    halves.append((ex * (1.0 / den)).astype(BF16))
                p_rows.append(jnp.concatenate(halves, axis=1))
            pm = jnp.concatenate(p_rows, axis=0)
            ov = jnp.dot(pm, bdv, preferred_element_type=F32)
            for p in range(cols_per_kv):
                cc = kv * cols_per_kv + p
                att_ref[s * sq:(s + 1) * sq, cc * LANES:(cc + 1) * LANES] = (
                    ov[p * sq:(p + 1) * sq].astype(BF16))

    y = jnp.dot(att_ref[...], wo_ref[...], preferred_element_type=F32)
    o_ref[...] = x_ref[...] + _rms(y, g_ref[...])


def _attention(x, q, k_prev, k, v_prev, v, sinks, norm_g, w_o, layer, j, *, tq, sq, n_kv,
               streaming):
    m, d = x.shape
    n_cols = q.shape[0]
    kv_w = k.shape[1]
    nsub = tq // sq
    if streaming:
        per = tq // WINDOW
        prev_spec = pl.BlockSpec((WINDOW, kv_w), lambda i: (jnp.maximum(i * per - 1, 0), 0))
    else:
        prev_spec = pl.BlockSpec((nsub, WINDOW, kv_w), lambda i: (i, 0, 0))
    cur_spec = pl.BlockSpec((tq, kv_w), lambda i: (i, 0))
    kern = functools.partial(_attn_kernel, sq=sq, n_kv=n_kv, streaming=streaming,
                             q_pos0=PAST_LEN, k_pos0=PAST_LEN - WINDOW)
    return pl.pallas_call(
        kern,
        grid=(m // tq,),
        in_specs=[
            pl.BlockSpec(memory_space=pltpu.SMEM),
            pl.BlockSpec((n_cols, tq, LANES), lambda i: (0, i, 0)),
            prev_spec, cur_spec, prev_spec, cur_spec,
            pl.BlockSpec((tq, d), lambda i: (i, 0)),
            pl.BlockSpec((None, None, 1, d), lambda i: (layer, 3, 0, 0)),
            pl.BlockSpec((None, d, d), lambda i: (j, 0, 0)),
        ],
        out_specs=pl.BlockSpec((tq, d), lambda i: (i, 0)),
        out_shape=jax.ShapeDtypeStruct((m, d), F32),
        scratch_shapes=[pltpu.VMEM((tq, d), BF16)],
        compiler_params=_params("arbitrary"),
        name="swa_attention",
    )(sinks, q, k_prev, k, v_prev, v, x, norm_g, w_o)


def _rope_tables(pos):
    half = HEAD_DIM // 2
    inv_freq = ROPE_THETA ** (-jnp.arange(half, dtype=F32) / half)
    ang = pos.astype(F32)[:, None] * inv_freq[None, :]
    cos = jnp.cos(ang)
    sin = jnp.sin(ang)
    cos_t = jnp.tile(cos, (1, LANES // half))
    sin_t = jnp.tile(jnp.concatenate([-sin, sin], axis=1), (1, HEADS_PER_COL))
    return cos_t, sin_t


def kernel(x_prompt, x_sample, state_conv, cache_k, cache_v, norm_g, w_ffn_in, w_ffn_out,
           w_conv_in, w_conv, w_conv_out, w_qkv, w_attn_out, attn_sinks):
    batch, seq, d = x_prompt.shape
    dec_batch, dec_seq, _ = x_sample.shape
    depth = norm_g.shape[0]
    n_heads = attn_sinks.shape[1]
    n_kv = cache_k.shape[3]
    kv_w = n_kv * HEAD_DIM
    assert batch == 1 and dec_seq == CHUNK and cache_k.shape[2] == WINDOW

    g4 = norm_g.reshape(depth, norm_g.shape[1], 1, d)
    w_ffn_in_b = w_ffn_in.astype(BF16)
    w_ffn_out_b = w_ffn_out.astype(BF16)
    w_conv_in_b = w_conv_in.astype(BF16)
    w_conv_out_b = w_conv_out.astype(BF16)
    w_qkv_b = w_qkv.astype(BF16)
    w_attn_out_b = w_attn_out.astype(BF16)

    xp = x_prompt.reshape(seq, d)
    xs = x_sample.reshape(dec_batch * dec_seq, d)
    cos_p, sin_p = _rope_tables(jnp.arange(seq))
    cos_s, sin_s = _rope_tables(jnp.tile(PAST_LEN + jnp.arange(dec_seq), dec_batch))

    tm = ROW_TILE
    conv_p, conv_s, k_p, v_p, k_s, v_s = [], [], [], [], [], []
    for layer in range(depth):
        j = layer // 2
        ffn = functools.partial(_ffn, norm_g=g4, w_in=w_ffn_in_b, w_out=w_ffn_out_b, layer=layer,
                                tm=FFN_TM, tf=FFN_TF)
        xp = ffn(xp, slot=0, n_pre=0, n_post=1)
        xs = ffn(xs, slot=0, n_pre=0, n_post=1)
        if layer % 2 == 0:
            mix = functools.partial(_conv_mixer, norm_g=g4, w_in=w_conv_in_b, conv_w=w_conv,
                                    w_out=w_conv_out_b, layer=layer, j=j, tm=tm, tn=CONV_TN)
            xp, st = mix(xp, jnp.zeros((1, 2, d), F32), seg=tm, carried=True)
            conv_p.append(st)
            xs, st = mix(xs, state_conv[j], seg=dec_seq, carried=False)
            conv_s.append(st)
        else:
            proj = functools.partial(_qkv_project, norm_g=g4, w_qkv=w_qkv_b, layer=layer, j=j,
                                     tm=tm, n_heads=n_heads, n_kv=n_kv)
            att = functools.partial(_attention, sinks=attn_sinks[j], norm_g=g4, w_o=w_attn_out_b,
                                    layer=layer, j=j, n_kv=n_kv)
            q, k, v = proj(xp, cos=cos_p, sin=sin_p)
            xp = att(xp, q, k, k, v, v, tq=tm, sq=WINDOW, streaming=True)
            k_p.append(k[seq - WINDOW:].reshape(1, WINDOW, n_kv, HEAD_DIM))
            v_p.append(v[seq - WINDOW:].reshape(1, WINDOW, n_kv, HEAD_DIM))
            q, k, v = proj(xs, cos=cos_s, sin=sin_s)
            ck = cache_k[j].reshape(dec_batch, WINDOW, kv_w)
            cv = cache_v[j].reshape(dec_batch, WINDOW, kv_w)
            xs = att(xs, q, ck, k, cv, v, tq=tm, sq=dec_seq, streaming=False)
            k_new = jnp.concatenate([ck, k.reshape(dec_batch, dec_seq, kv_w)], axis=1)[:, -WINDOW:]
            v_new = jnp.concatenate([cv, v.reshape(dec_batch, dec_seq, kv_w)], axis=1)[:, -WINDOW:]
            k_s.append(k_new.reshape(dec_batch, WINDOW, n_kv, HEAD_DIM))
            v_s.append(v_new.reshape(dec_batch, WINDOW, n_kv, HEAD_DIM))
        xp = ffn(xp, slot=1, n_pre=4, n_post=5)
        xs = ffn(xs, slot=1, n_pre=4, n_post=5)

    return (xp.reshape(batch, seq, d), xs.reshape(dec_batch, dec_seq, d),
            jnp.stack(conv_p), jnp.stack(conv_s),
            jnp.stack(k_p), jnp.stack(v_p), jnp.stack(k_s), jnp.stack(v_s))
```

```python
import functools

import jax
import jax.numpy as jnp
import numpy as np
from jax import lax
from jax.experimental import pallas as pl
from jax.experimental.pallas import tpu as pltpu

F32 = jnp.float32
BF16 = jnp.bfloat16

CHUNK = 64
LOOKBACK_CHUNKS = 2
WINDOW = LOOKBACK_CHUNKS * CHUNK
HEAD_DIM = 64
PAST_LEN = 2048
ROPE_THETA = 10000.0
NORM_EPS = 1e-6

LANES = 128
HEADS_PER_COL = LANES // HEAD_DIM
VMEM_LIMIT = 62 * 1024 * 1024
ROW_TILE = 512
FFN_TM = 1024
FFN_ROWS = 512
FFN_TF = 512
CONV_TN = 512


def _rms(x, g):
    ms = jnp.mean(x * x, axis=-1, keepdims=True)
    return x * lax.rsqrt(ms + NORM_EPS) * g


def _params(*sem):
    return pltpu.CompilerParams(dimension_semantics=sem, vmem_limit_bytes=VMEM_LIMIT)


def _ffn_kernel(x_ref, gpre_ref, gpost_ref, wg_ref, wu_ref, wo_ref, o_ref, h_ref):
    j = pl.program_id(1)

    @pl.when(j == 0)
    def _():
        h_ref[...] = _rms(x_ref[...], gpre_ref[...]).astype(BF16)
        o_ref[...] = jnp.zeros_like(o_ref)

    for r in range(0, x_ref.shape[0], FFN_ROWS):
        rows = slice(r, r + FFN_ROWS)
        h = h_ref[rows]
        gate = jnp.dot(h, wg_ref[...], preferred_element_type=F32)
        up = jnp.dot(h, wu_ref[...], preferred_element_type=F32)
        act = (gate * jax.nn.sigmoid(gate) * up).astype(BF16)
        o_ref[rows] += jnp.dot(act, wo_ref[...], preferred_element_type=F32)

    @pl.when(j == pl.num_programs(1) - 1)
    def _():
        o_ref[...] = x_ref[...] + 0.5 * _rms(o_ref[...], gpost_ref[...])


def _ffn(x, norm_g, w_in, w_out, layer, slot, n_pre, n_post, *, tm, tf):
    m, d = x.shape
    nf = w_out.shape[2] // tf
    g_spec = lambda n: pl.BlockSpec((None, None, 1, d), lambda i, j: (layer, n, 0, 0))
    w_spec = lambda sec: pl.BlockSpec((None, None, d, tf), lambda i, j: (layer, slot, 0, j + sec * nf))
    return pl.pallas_call(
        _ffn_kernel,
        grid=(m // tm, nf),
        in_specs=[
            pl.BlockSpec((tm, d), lambda i, j: (i, 0)),
            g_spec(n_pre),
            g_spec(n_post),
            w_spec(0),
            w_spec(1),
            pl.BlockSpec((None, None, tf, d), lambda i, j: (layer, slot, j, 0)),
        ],
        out_specs=pl.BlockSpec((tm, d), lambda i, j: (i, 0)),
        out_shape=jax.ShapeDtypeStruct((m, d), F32),
        scratch_shapes=[pltpu.VMEM((tm, d), BF16)],
        compiler_params=_params("parallel", "arbitrary"),
        name="ffn",
    )(x, norm_g, norm_g, w_in, w_in, w_out)


def _conv_kernel(x_ref, gpre_ref, gpost_ref, wb_ref, wc_ref, wh_ref, cw_ref, wo_ref, st_ref,
                 o_ref, ns_ref, h_ref, acc_ref, carry_ref, *, seg, carried):
    i = pl.program_id(0)
    n = pl.program_id(1)
    tm = x_ref.shape[0]
    tn = wb_ref.shape[1]
    nseg = tm // seg
    cols = pl.ds(pl.multiple_of(n * tn, tn), tn)

    @pl.when(n == 0)
    def _():
        h_ref[...] = _rms(x_ref[...], gpre_ref[...]).astype(BF16)
        acc_ref[...] = jnp.zeros_like(acc_ref)

    if carried:
        @pl.when(i == 0)
        def _():
            carry_ref[:, cols] = st_ref[0, :, cols]

    h = h_ref[...]
    b_gate = jnp.dot(h, wb_ref[...], preferred_element_type=F32)
    c_gate = jnp.dot(h, wc_ref[...], preferred_element_type=F32)
    hh = jnp.dot(h, wh_ref[...], preferred_element_type=F32)
    u = c_gate * hh
    cw = cw_ref[...]
    w0, w1, w2 = cw[0:1], cw[1:2], cw[2:3]
    row = lax.broadcasted_iota(jnp.int32, (seg, tn), 0)
    convs = []
    for s in range(nseg):
        us = u[s * seg:(s + 1) * seg]
        prev = carry_ref[:, cols] if carried else st_ref[s]
        p1 = jnp.where(row == 0, prev[1:2], pltpu.roll(us, 1, 0))
        p2 = jnp.where(row == 0, prev[0:1], jnp.where(row == 1, prev[1:2], pltpu.roll(us, 2, 0)))
        convs.append(p2 * w0 + p1 * w1 + us * w2)
        if carried:
            carry_ref[:, cols] = us[seg - 2:seg]
            ns_ref[0, :, cols] = us[seg - 2:seg]
        else:
            ns_ref[s] = us[seg - 2:seg]
    conv = convs[0] if nseg == 1 else jnp.concatenate(convs, axis=0)
    z = (b_gate * conv).astype(BF16)
    acc_ref[...] += jnp.dot(z, wo_ref[...], preferred_element_type=F32)

    @pl.when(n == pl.num_programs(1) - 1)
    def _():
        o_ref[...] = x_ref[...] + _rms(acc_ref[...], gpost_ref[...])


def _conv_mixer(x, state, norm_g, w_in, conv_w, w_out, layer, j, *, tm, tn, seg, carried):
    m, d = x.shape
    nn = d // tn
    nseg = tm // seg
    g_spec = lambda n: pl.BlockSpec((None, None, 1, d), lambda i, c: (layer, n, 0, 0))
    w_spec = lambda sec: pl.BlockSpec((None, d, tn), lambda i, c: (j, 0, c + sec * nn))
    if carried:
        st_spec = pl.BlockSpec((1, 2, d), lambda i, c: (0, 0, 0))
    else:
        st_spec = pl.BlockSpec((nseg, 2, tn), lambda i, c: (i, 0, c))
    return pl.pallas_call(
        functools.partial(_conv_kernel, seg=seg, carried=carried),
        grid=(m // tm, nn),
        in_specs=[
            pl.BlockSpec((tm, d), lambda i, c: (i, 0)),
            g_spec(2),
            g_spec(3),
            w_spec(0),
            w_spec(1),
            w_spec(2),
            pl.BlockSpec((None, 3, tn), lambda i, c: (j, 0, c)),
            pl.BlockSpec((None, tn, d), lambda i, c: (j, c, 0)),
            st_spec,
        ],
        out_specs=[pl.BlockSpec((tm, d), lambda i, c: (i, 0)), st_spec],
        out_shape=[
            jax.ShapeDtypeStruct((m, d), F32),
            jax.ShapeDtypeStruct(state.shape, F32),
        ],
        scratch_shapes=[
            pltpu.VMEM((tm, d), BF16),
            pltpu.VMEM((tm, d), F32),
            pltpu.VMEM((2, d), F32),
        ],
        compiler_params=_params("arbitrary", "arbitrary"),
        name="conv_mixer",
    )(x, norm_g, norm_g, w_in, w_in, w_in, conv_w, w_out, state)


def _rope_col(xc, cos, sin_signed):
    lane = lax.broadcasted_iota(jnp.int32, xc.shape, 1)
    first_half = (lane % HEAD_DIM) < HEAD_DIM // 2
    swapped = jnp.where(first_half,
                        pltpu.roll(xc, LANES - HEAD_DIM // 2, 1),
                        pltpu.roll(xc, HEAD_DIM // 2, 1))
    return xc * cos + swapped * sin_signed


def _qkv_kernel(x_ref, g_ref, w_ref, cos_ref, sin_ref, q_ref, k_ref, v_ref, *, chunk):
    n_q = q_ref.shape[0] * LANES
    kv_w = k_ref.shape[1]
    h = _rms(x_ref[...], g_ref[...]).astype(BF16)
    cos = cos_ref[...]
    sin = sin_ref[...]
    for c0 in range(0, n_q, chunk):
        y = jnp.dot(h, w_ref[:, c0:c0 + chunk], preferred_element_type=F32)
        for c in range(chunk // LANES):
            qc = _rope_col(y[:, c * LANES:(c + 1) * LANES], cos, sin) * (HEAD_DIM ** -0.5)
            q_ref[c0 // LANES + c] = qc.astype(BF16)
    y = jnp.dot(h, w_ref[:, n_q:], preferred_element_type=F32)
    for c in range(kv_w // LANES):
        k_ref[:, c * LANES:(c + 1) * LANES] = _rope_col(y[:, c * LANES:(c + 1) * LANES], cos, sin)
    v_ref[...] = y[:, kv_w:]


def _qkv_project(x, norm_g, w_qkv, cos, sin, layer, j, *, tm, n_heads, n_kv):
    m, d = x.shape
    kv_w = n_kv * HEAD_DIM
    n_cols = n_heads // HEADS_PER_COL
    return pl.pallas_call(
        functools.partial(_qkv_kernel, chunk=2 * kv_w),
        grid=(m // tm,),
        in_specs=[
            pl.BlockSpec((tm, d), lambda i: (i, 0)),
            pl.BlockSpec((None, None, 1, d), lambda i: (layer, 2, 0, 0)),
            pl.BlockSpec((None, d, w_qkv.shape[2]), lambda i: (j, 0, 0)),
            pl.BlockSpec((tm, LANES), lambda i: (i, 0)),
            pl.BlockSpec((tm, LANES), lambda i: (i, 0)),
        ],
        out_specs=[
            pl.BlockSpec((n_cols, tm, LANES), lambda i: (0, i, 0)),
            pl.BlockSpec((tm, kv_w), lambda i: (i, 0)),
            pl.BlockSpec((tm, kv_w), lambda i: (i, 0)),
        ],
        out_shape=[
            jax.ShapeDtypeStruct((n_cols, m, LANES), BF16),
            jax.ShapeDtypeStruct((m, kv_w), F32),
            jax.ShapeDtypeStruct((m, kv_w), F32),
        ],
        compiler_params=_params("parallel"),
        name="qkv_rope",
    )(x, norm_g, w_qkv, cos, sin)


def _block_diag(col, rolled, first):
    lane = lax.broadcasted_iota(jnp.int32, col.shape, 1)
    lo = lane < HEAD_DIM
    zero = jnp.zeros_like(col)
    if first:
        return jnp.concatenate([jnp.where(lo, col, zero), jnp.where(lo, zero, rolled)], axis=0)
    return jnp.concatenate([jnp.where(lo, rolled, zero), jnp.where(lo, zero, col)], axis=0)


def _attn_kernel(sink_ref, q_ref, kp_ref, kc_ref, vp_ref, vc_ref, x_ref, g_ref, wo_ref, o_ref,
                 att_ref, *, sq, n_kv, streaming, q_pos0, k_pos0):
    i = pl.program_id(0)
    tq = x_ref.shape[0]
    nsub = tq // sq
    nk = WINDOW + sq
    n_cols = q_ref.shape[0]
    cols_per_kv = n_cols // n_kv

    if streaming:
        k_all = jnp.concatenate([kp_ref[...], kc_ref[...]], axis=0)
        v_all = jnp.concatenate([vp_ref[...], vc_ref[...]], axis=0)

    row = lax.broadcasted_iota(jnp.int32, (sq, nk), 0)
    colk = lax.broadcasted_iota(jnp.int32, (sq, nk), 1)

    for s in range(nsub):
        if streaming:
            q0 = i * tq + s * sq
            k0 = q0 - WINDOW
            ks = k_all[s * sq:s * sq + nk]
            vs = v_all[s * sq:s * sq + nk]
        else:
            q0 = q_pos0
            k0 = k_pos0
            ks = jnp.concatenate([kp_ref[s], kc_ref[s * sq:(s + 1) * sq]], axis=0)
            vs = jnp.concatenate([vp_ref[s], vc_ref[s * sq:(s + 1) * sq]], axis=0)
        kpos = k0 + colk
        q_chunk = lax.shift_right_arithmetic(q0 + row, 6)
        k_chunk = lax.shift_right_arithmetic(kpos, 6)
        valid = (k_chunk >= q_chunk - LOOKBACK_CHUNKS) & (k_chunk <= q_chunk)
        if streaming:
            valid = valid & (kpos >= 0)

        for kv in range(n_kv):
            c, first = divmod(kv, HEADS_PER_COL)
            kcol = ks[:, c * LANES:(c + 1) * LANES]
            vcol = vs[:, c * LANES:(c + 1) * LANES]
            bdk = _block_diag(kcol, pltpu.roll(kcol, HEAD_DIM, 1), first == 0).astype(BF16)
            bdv = _block_diag(vcol, pltpu.roll(vcol, HEAD_DIM, 1), first == 0).astype(BF16)
            qs = q_ref[kv * cols_per_kv:(kv + 1) * cols_per_kv, s * sq:(s + 1) * sq, :]
            qs = qs.reshape(cols_per_kv * sq, LANES)
            sc = lax.dot_general(qs, bdk, (((1,), (1,)), ((), ())), preferred_element_type=F32)
            p_rows = []
            for p in range(cols_per_kv):
                halves = []
                for e in range(HEADS_PER_COL):
                    sink = sink_ref[(kv * cols_per_kv + p) * HEADS_PER_COL + e]
                    blk = sc[p * sq:(p + 1) * sq, e * nk:(e + 1) * nk]
                    blk = jnp.where(valid, blk, -jnp.inf)
                    mx = jnp.maximum(jnp.max(blk, axis=-1, keepdims=True), sink)
                    ex = jnp.exp(blk - mx)
                    den = jnp.sum(ex, axis=-1, keepdims=True) + jnp.exp(sink - mx)
                    halves.append((ex * (1.0 / den)).astype(BF16))
                p_rows.append(jnp.concatenate(halves, axis=1))
            pm = jnp.concatenate(p_rows, axis=0)
            ov = jnp.dot(pm, bdv, preferred_element_type=F32)
            for p in range(cols_per_kv):
                cc = kv * cols_per_kv + p
                att_ref[s * sq:(s + 1) * sq, cc * LANES:(cc + 1) * LANES] = (
                    ov[p * sq:(p + 1) * sq].astype(BF16))

    y = jnp.dot(att_ref[...], wo_ref[...], preferred_element_type=F32)
    o_ref[...] = x_ref[...] + _rms(y, g_ref[...])


def _attention(x, q, k_prev, k, v_prev, v, sinks, norm_g, w_o, layer, j, *, tq, sq, n_kv,
               streaming):
    m, d = x.shape
    n_cols = q.shape[0]
    kv_w = k.shape[1]
    nsub = tq // sq
    if streaming:
        per = tq // WINDOW
        prev_spec = pl.BlockSpec((WINDOW, kv_w), lambda i: (jnp.maximum(i * per - 1, 0), 0))
    else:
        prev_spec = pl.BlockSpec((nsub, WINDOW, kv_w), lambda i: (i, 0, 0))
    cur_spec = pl.BlockSpec((tq, kv_w), lambda i: (i, 0))
    kern = functools.partial(_attn_kernel, sq=sq, n_kv=n_kv, streaming=streaming,
                             q_pos0=PAST_LEN, k_pos0=PAST_LEN - WINDOW)
    return pl.pallas_call(
        kern,
        grid=(m // tq,),
        in_specs=[
            pl.BlockSpec(memory_space=pltpu.SMEM),
            pl.BlockSpec((n_cols, tq, LANES), lambda i: (0, i, 0)),
            prev_spec, cur_spec, prev_spec, cur_spec,
            pl.BlockSpec((tq, d), lambda i: (i, 0)),
            pl.BlockSpec((None, None, 1, d), lambda i: (layer, 3, 0, 0)),
            pl.BlockSpec((None, d, d), lambda i: (j, 0, 0)),
        ],
        out_specs=pl.BlockSpec((tq, d), lambda i: (i, 0)),
        out_shape=jax.ShapeDtypeStruct((m, d), F32),
        scratch_shapes=[pltpu.VMEM((tq, d), BF16)],
        compiler_params=_params("arbitrary"),
        name="swa_attention",
    )(sinks, q, k_prev, k, v_prev, v, x, norm_g, w_o)


def _rope_tables(pos):
    half = HEAD_DIM // 2
    inv_freq = ROPE_THETA ** (-jnp.arange(half, dtype=F32) / half)
    ang = pos.astype(F32)[:, None] * inv_freq[None, :]
    cos = jnp.cos(ang)
    sin = jnp.sin(ang)
    cos_t = jnp.tile(cos, (1, LANES // half))
    sin_t = jnp.tile(jnp.concatenate([-sin, sin], axis=1), (1, HEADS_PER_COL))
    return cos_t, sin_t


def kernel(x_prompt, x_sample, state_conv, cache_k, cache_v, norm_g, w_ffn_in, w_ffn_out,
           w_conv_in, w_conv, w_conv_out, w_qkv, w_attn_out, attn_sinks):
    batch, seq, d = x_prompt.shape
    dec_batch, dec_seq, _ = x_sample.shape
    depth = norm_g.shape[0]
    n_heads = attn_sinks.shape[1]
    n_kv = cache_k.shape[3]
    kv_w = n_kv * HEAD_DIM
    assert batch == 1 and dec_seq == CHUNK and cache_k.shape[2] == WINDOW

    g4 = norm_g.reshape(depth, norm_g.shape[1], 1, d)
    w_ffn_in_b = w_ffn_in.astype(BF16)
    w_ffn_out_b = w_ffn_out.astype(BF16)
    w_conv_in_b = w_conv_in.astype(BF16)
    w_conv_out_b = w_conv_out.astype(BF16)
    w_qkv_b = w_qkv.astype(BF16)
    w_attn_out_b = w_attn_out.astype(BF16)

    xp = x_prompt.reshape(seq, d)
    xs = x_sample.reshape(dec_batch * dec_seq, d)
    cos_p, sin_p = _rope_tables(jnp.arange(seq))
    cos_s, sin_s = _rope_tables(jnp.tile(PAST_LEN + jnp.arange(dec_seq), dec_batch))

    tm = ROW_TILE
    conv_p, conv_s, k_p, v_p, k_s, v_s = [], [], [], [], [], []
    for layer in range(depth):
        j = layer // 2
        ffn = functools.partial(_ffn, norm_g=g4, w_in=w_ffn_in_b, w_out=w_ffn_out_b, layer=layer,
                                tm=FFN_TM, tf=FFN_TF)
        xp = ffn(xp, slot=0, n_pre=0, n_post=1)
        xs = ffn(xs, slot=0, n_pre=0, n_post=1)
        if layer % 2 == 0:
            mix = functools.partial(_conv_mixer, norm_g=g4, w_in=w_conv_in_b, conv_w=w_conv,
                                    w_out=w_conv_out_b, layer=layer, j=j, tm=tm, tn=CONV_TN)
            xp, st = mix(xp, jnp.zeros((1, 2, d), F32), seg=tm, carried=True)
            conv_p.append(st)
            xs, st = mix(xs, state_conv[j], seg=dec_seq, carried=False)
            conv_s.append(st)
        else:
            proj = functools.partial(_qkv_project, norm_g=g4, w_qkv=w_qkv_b, layer=layer, j=j,
                                     tm=tm, n_heads=n_heads, n_kv=n_kv)
            att = functools.partial(_attention, sinks=attn_sinks[j], norm_g=g4, w_o=w_attn_out_b,
                                    layer=layer, j=j, n_kv=n_kv)
            q, k, v = proj(xp, cos=cos_p, sin=sin_p)
            xp = att(xp, q, k, k, v, v, tq=tm, sq=WINDOW, streaming=True)
            k_p.append(k[seq - WINDOW:].reshape(1, WINDOW, n_kv, HEAD_DIM))
            v_p.append(v[seq - WINDOW:].reshape(1, WINDOW, n_kv, HEAD_DIM))
            q, k, v = proj(xs, cos=cos_s, sin=sin_s)
            ck = cache_k[j].reshape(dec_batch, WINDOW, kv_w)
            cv = cache_v[j].reshape(dec_batch, WINDOW, kv_w)
            xs = att(xs, q, ck, k, cv, v, tq=tm, sq=dec_seq, streaming=False)
            k_new = jnp.concatenate([ck, k.reshape(dec_batch, dec_seq, kv_w)], axis=1)[:, -WINDOW:]
            v_new = jnp.concatenate([cv, v.reshape(dec_batch, dec_seq, kv_w)], axis=1)[:, -WINDOW:]
            k_s.append(k_new.reshape(dec_batch, WINDOW, n_kv, HEAD_DIM))
            v_s.append(v_new.reshape(dec_batch, WINDOW, n_kv, HEAD_DIM))
        xp = ffn(xp, slot=1, n_pre=4, n_post=5)
        xs = ffn(xs, slot=1, n_pre=4, n_post=5)

    return (xp.reshape(batch, seq, d), xs.reshape(dec_batch, dec_seq, d),
            jnp.stack(conv_p), jnp.stack(conv_s),
            jnp.stack(k_p), jnp.stack(v_p), jnp.stack(k_s), jnp.stack(v_s))
```

```python
import functools

import jax
import jax.numpy as jnp
import numpy as np
from jax import lax
from jax.experimental import pallas as pl
from jax.experimental.pallas import tpu as pltpu

F32 = jnp.float32
BF16 = jnp.bfloat16

CHUNK = 64
LOOKBACK_CHUNKS = 2
WINDOW = LOOKBACK_CHUNKS * CHUNK
HEAD_DIM = 64
PAST_LEN = 2048
ROPE_THETA = 10000.0
NORM_EPS = 1e-6

LANES = 128
HEADS_PER_COL = LANES // HEAD_DIM
VMEM_LIMIT = 62 * 1024 * 1024
ROW_TILE = 512
FFN_TM = 1024
FFN_ROWS = 512
FFN_TF = 512
CONV_TN = 512
CONV_ROWS = 256


def _rms(x, g):
    ms = jnp.mean(x * x, axis=-1, keepdims=True)
    return x * lax.rsqrt(ms + NORM_EPS) * g


def _params(*sem):
    return pltpu.CompilerParams(dimension_semantics=sem, vmem_limit_bytes=VMEM_LIMIT)


def _ffn_kernel(x_ref, gpre_ref, gpost_ref, wg_ref, wu_ref, wo_ref, o_ref, h_ref):
    j = pl.program_id(1)
    last = pl.num_programs(1) - 1

    def step(first, final):
        for r in range(0, x_ref.shape[0], FFN_ROWS):
            rows = slice(r, r + FFN_ROWS)
            if first:
                h = _rms(x_ref[rows], gpre_ref[...]).astype(BF16)
                h_ref[rows] = h
            else:
                h = h_ref[rows]
            gate = jnp.dot(h, wg_ref[...], preferred_element_type=F32)
            up = jnp.dot(h, wu_ref[...], preferred_element_type=F32)
            act = (gate * jax.nn.sigmoid(gate) * up).astype(BF16)
            part = jnp.dot(act, wo_ref[...], preferred_element_type=F32)
            if first:
                o_ref[rows] = part
            elif final:
                o_ref[rows] = x_ref[rows] + 0.5 * _rms(o_ref[rows] + part, gpost_ref[...])
            else:
                o_ref[rows] += part

    pl.when(j == 0)(functools.partial(step, True, False))
    pl.when((j > 0) & (j < last))(functools.partial(step, False, False))
    pl.when(j == last)(functools.partial(step, False, True))


def _ffn(x, norm_g, w_in, w_out, layer, slot, n_pre, n_post, *, tm, tf):
    m, d = x.shape
    nf = w_out.shape[2] // tf
    g_spec = lambda n: pl.BlockSpec((None, None, 1, d), lambda i, j: (layer, n, 0, 0))
    w_spec = lambda sec: pl.BlockSpec((None, None, d, tf), lambda i, j: (layer, slot, 0, j + sec * nf))
    return pl.pallas_call(
        _ffn_kernel,
        grid=(m // tm, nf),
        in_specs=[
            pl.BlockSpec((tm, d), lambda i, j: (i, 0)),
            g_spec(n_pre),
            g_spec(n_post),
            w_spec(0),
            w_spec(1),
            pl.BlockSpec((None, None, tf, d), lambda i, j: (layer, slot, j, 0)),
        ],
        out_specs=pl.BlockSpec((tm, d), lambda i, j: (i, 0)),
        out_shape=jax.ShapeDtypeStruct((m, d), F32),
        scratch_shapes=[pltpu.VMEM((tm, d), BF16)],
        compiler_params=_params("parallel", "arbitrary"),
        name="ffn",
    )(x, norm_g, norm_g, w_in, w_in, w_out)


def _conv_kernel(x_ref, gpre_ref, gpost_ref, wb_ref, wc_ref, wh_ref, cw_ref, wo_ref, st_ref,
                 o_ref, ns_ref, h_ref, carry_ref, *, seg, carried):
    i = pl.program_id(0)
    n = pl.program_id(1)
    last = pl.num_programs(1) - 1
    tm = x_ref.shape[0]
    tn = wb_ref.shape[1]
    group = max(seg, CONV_ROWS)
    cols = pl.ds(pl.multiple_of(n * tn, tn), tn)

    if carried:
        @pl.when(i == 0)
        def _():
            carry_ref[:, cols] = st_ref[0, :, cols]

    def step(first, final):
        cw = cw_ref[...]
        w0, w1, w2 = cw[0:1], cw[1:2], cw[2:3]
        row = lax.broadcasted_iota(jnp.int32, (seg, tn), 0)
        for r in range(0, tm, group):
            rows = slice(r, r + group)
            if first:
                h = _rms(x_ref[rows], gpre_ref[...]).astype(BF16)
                h_ref[rows] = h
            else:
                h = h_ref[rows]
            b_gate = jnp.dot(h, wb_ref[...], preferred_element_type=F32)
            c_gate = jnp.dot(h, wc_ref[...], preferred_element_type=F32)
            hh = jnp.dot(h, wh_ref[...], preferred_element_type=F32)
            u = c_gate * hh
            convs = []
            for s in range(group // seg):
                us = u[s * seg:(s + 1) * seg]
                sidx = r // seg + s
                prev = carry_ref[:, cols] if carried else st_ref[sidx]
                p1 = jnp.where(row == 0, prev[1:2], pltpu.roll(us, 1, 0))
                p2 = jnp.where(row == 0, prev[0:1],
                               jnp.where(row == 1, prev[1:2], pltpu.roll(us, 2, 0)))
                convs.append(p2 * w0 + p1 * w1 + us * w2)
                if carried:
                    carry_ref[:, cols] = us[seg - 2:seg]
                    ns_ref[0, :, cols] = us[seg - 2:seg]
                else:
                    ns_ref[sidx] = us[seg - 2:seg]
            conv = convs[0] if len(convs) == 1 else jnp.concatenate(convs, axis=0)
            z = (b_gate * conv).astype(BF16)
            part = jnp.dot(z, wo_ref[...], preferred_element_type=F32)
            if first:
                o_ref[rows] = part
            elif final:
                o_ref[rows] = x_ref[rows] + _rms(o_ref[rows] + part, gpost_ref[...])
            else:
                o_ref[rows] += part

    pl.when(n == 0)(functools.partial(step, True, False))
    pl.when((n > 0) & (n < last))(functools.partial(step, False, False))
    pl.when(n == last)(functools.partial(step, False, True))


def _conv_mixer(x, state, norm_g, w_in, conv_w, w_out, layer, j, *, tm, tn, seg, carried):
    m, d = x.shape
    nn = d // tn
    nseg = tm // seg
    g_spec = lambda n: pl.BlockSpec((None, None, 1, d), lambda i, c: (layer, n, 0, 0))
    w_spec = lambda sec: pl.BlockSpec((None, d, tn), lambda i, c: (j, 0, c + sec * nn))
    if carried:
        st_spec = pl.BlockSpec((1, 2, d), lambda i, c: (0, 0, 0))
    else:
        st_spec = pl.BlockSpec((nseg, 2, tn), lambda i, c: (i, 0, c))
    return pl.pallas_call(
        functools.partial(_conv_kernel, seg=seg, carried=carried),
        grid=(m // tm, nn),
        in_specs=[
            pl.BlockSpec((tm, d), lambda i, c: (i, 0)),
            g_spec(2),
            g_spec(3),
            w_spec(0),
            w_spec(1),
            w_spec(2),
            pl.BlockSpec((None, 3, tn), lambda i, c: (j, 0, c)),
            pl.BlockSpec((None, tn, d), lambda i, c: (j, c, 0)),
            st_spec,
        ],
        out_specs=[pl.BlockSpec((tm, d), lambda i, c: (i, 0)), st_spec],
        out_shape=[
            jax.ShapeDtypeStruct((m, d), F32),
            jax.ShapeDtypeStruct(state.shape, F32),
        ],
        scratch_shapes=[pltpu.VMEM((tm, d), BF16), pltpu.VMEM((2, d), F32)],
        compiler_params=_params("arbitrary", "arbitrary"),
        name="conv_mixer",
    )(x, norm_g, norm_g, w_in, w_in, w_in, conv_w, w_out, state)


def _rope_col(xc, cos, sin_signed):
    lane = lax.broadcasted_iota(jnp.int32, xc.shape, 1)
    first_half = (lane % HEAD_DIM) < HEAD_DIM // 2
    swapped = jnp.where(first_half,
                        pltpu.roll(xc, LANES - HEAD_DIM // 2, 1),
                        pltpu.roll(xc, HEAD_DIM // 2, 1))
    return xc * cos + swapped * sin_signed


def _qkv_kernel(x_ref, g_ref, w_ref, cos_ref, sin_ref, q_ref, k_ref, v_ref, *, chunk):
    n_q = q_ref.shape[0] * LANES
    kv_w = k_ref.shape[1]
    h = _rms(x_ref[...], g_ref[...]).astype(BF16)
    cos = cos_ref[...]
    sin = sin_ref[...]
    for c0 in range(0, n_q, chunk):
        y = jnp.dot(h, w_ref[:, c0:c0 + chunk], preferred_element_type=F32)
        for c in range(chunk // LANES):
            qc = _rope_col(y[:, c * LANES:(c + 1) * LANES], cos, sin) * (HEAD_DIM ** -0.5)
            q_ref[c0 // LANES + c] = qc.astype(BF16)
    y = jnp.dot(h, w_ref[:, n_q:], preferred_element_type=F32)
    for c in range(kv_w // LANES):
        k_ref[:, c * LANES:(c + 1) * LANES] = _rope_col(y[:, c * LANES:(c + 1) * LANES], cos, sin)
    v_ref[...] = y[:, kv_w:]


def _qkv_project(x, norm_g, w_qkv, cos, sin, layer, j, *, tm, n_heads, n_kv):
    m, d = x.shape
    kv_w = n_kv * HEAD_DIM
    n_cols = n_heads // HEADS_PER_COL
    return pl.pallas_call(
        functools.partial(_qkv_kernel, chunk=2 * kv_w),
        grid=(m // tm,),
        in_specs=[
            pl.BlockSpec((tm, d), lambda i: (i, 0)),
            pl.BlockSpec((None, None, 1, d), lambda i: (layer, 2, 0, 0)),
            pl.BlockSpec((None, d, w_qkv.shape[2]), lambda i: (j, 0, 0)),
            pl.BlockSpec((tm, LANES), lambda i: (i, 0)),
            pl.BlockSpec((tm, LANES), lambda i: (i, 0)),
        ],
        out_specs=[
            pl.BlockSpec((n_cols, tm, LANES), lambda i: (0, i, 0)),
            pl.BlockSpec((tm, kv_w), lambda i: (i, 0)),
            pl.BlockSpec((tm, kv_w), lambda i: (i, 0)),
        ],
        out_shape=[
            jax.ShapeDtypeStruct((n_cols, m, LANES), BF16),
            jax.ShapeDtypeStruct((m, kv_w), F32),
            jax.ShapeDtypeStruct((m, kv_w), F32),
        ],
        compiler_params=_params("parallel"),
        name="qkv_rope",
    )(x, norm_g, w_qkv, cos, sin)


def _block_diag(col, rolled, first):
    lane = lax.broadcasted_iota(jnp.int32, col.shape, 1)
    lo = lane < HEAD_DIM
    zero = jnp.zeros_like(col)
    if first:
        return jnp.concatenate([jnp.where(lo, col, zero), jnp.where(lo, zero, rolled)], axis=0)
    return jnp.concatenate([jnp.where(lo, rolled, zero), jnp.where(lo, zero, col)], axis=0)


def _attn_kernel(sink_ref, q_ref, kp_ref, kc_ref, vp_ref, vc_ref, x_ref, g_ref, wo_ref, o_ref,
                 att_ref, *, sq, n_kv, streaming, q_pos0, k_pos0):
    i = pl.program_id(0)
    tq = x_ref.shape[0]
    nsub = tq // sq
    nk = WINDOW + sq
    n_cols = q_ref.shape[0]
    cols_per_kv = n_cols // n_kv

    if streaming:
        k_all = jnp.concatenate([kp_ref[...], kc_ref[...]], axis=0)
        v_all = jnp.concatenate([vp_ref[...], vc_ref[...]], axis=0)

    row = lax.broadcasted_iota(jnp.int32, (sq, nk), 0)
    colk = lax.broadcasted_iota(jnp.int32, (sq, nk), 1)

    for s in range(nsub):
        if streaming:
            q0 = i * tq + s * sq
            k0 = q0 - WINDOW
            ks = k_all[s * sq:s * sq + nk]
            vs = v_all[s * sq:s * sq + nk]
        else:
            q0 = q_pos0
            k0 = k_pos0
            ks = jnp.concatenate([kp_ref[s], kc_ref[s * sq:(s + 1) * sq]], axis=0)
            vs = jnp.concatenate([vp_ref[s], vc_ref[s * sq:(s + 1) * sq]], axis=0)
        kpos = k0 + colk
        q_chunk = lax.shift_right_arithmetic(q0 + row, 6)
        k_chunk = lax.shift_right_arithmetic(kpos, 6)
        valid = (k_chunk >= q_chunk - LOOKBACK_CHUNKS) & (k_chunk <= q_chunk)
        if streaming:
            valid = valid & (kpos >= 0)

        for kv in range(n_kv):
            c, first = divmod(kv, HEADS_PER_COL)
            kcol = ks[:, c * LANES:(c + 1) * LANES]
            vcol = vs[:, c * LANES:(c + 1) * LANES]
            bdk = _block_diag(kcol, pltpu.roll(kcol, HEAD_DIM, 1), first == 0).astype(BF16)
            bdv = _block_diag(vcol, pltpu.roll(vcol, HEAD_DIM, 1), first == 0).astype(BF16)
            qs = q_ref[kv * cols_per_kv:(kv + 1) * cols_per_kv, s * sq:(s + 1) * sq, :]
            qs = qs.reshape(cols_per_kv * sq, LANES)
            sc = lax.dot_general(qs, bdk, (((1,), (1,)), ((), ())), preferred_element_type=F32)
            p_rows = []
            for p in range(cols_per_kv):
                halves = []
                for e in range(HEADS_PER_COL):
                    sink = sink_ref[(kv * cols_per_kv + p) * HEADS_PER_COL + e]
                    blk = sc[p * sq:(p + 1) * sq, e * nk:(e + 1) * nk]
                    blk = jnp.where(valid, blk, -jnp.inf)
                    mx = jnp.maximum(jnp.max(blk, axis=-1, keepdims=True), sink)
                    ex = jnp.exp(blk - mx)
                    den = jnp.sum(ex, axis=-1, keepdims=True) + jnp.exp(sink - mx)
                    halves.append((ex * (1.0 / den)).astype(BF16))
                p_rows.append(jnp.concatenate(halves, axis=1))
            pm = jnp.concatenate(p_rows, axis=0)
            ov = jnp.dot(pm, bdv, preferred_element_type=F32)
            for p in range(cols_per_kv):
                cc = kv * cols_per_kv + p
                att_ref[s * sq:(s + 1) * sq, cc * LANES:(cc + 1) * LANES] = (
                    ov[p * sq:(p + 1) * sq].astype(BF16))

    y = jnp.dot(att_ref[...], wo_ref[...], preferred_element_type=F32)
    o_ref[...] = x_ref[...] + _rms(y, g_ref[...])


def _attention(x, q, k_prev, k, v_prev, v, sinks, norm_g, w_o, layer, j, *, tq, sq, n_kv,
               streaming):
    m, d = x.shape
    n_cols = q.shape[0]
    kv_w = k.shape[1]
    nsub = tq // sq
    if streaming:
        per = tq // WINDOW
        prev_spec = pl.BlockSpec((WINDOW, kv_w), lambda i: (jnp.maximum(i * per - 1, 0), 0))
    else:
        prev_spec = pl.BlockSpec((nsub, WINDOW, kv_w), lambda i: (i, 0, 0))
    cur_spec = pl.BlockSpec((tq, kv_w), lambda i: (i, 0))
    kern = functools.partial(_attn_kernel, sq=sq, n_kv=n_kv, streaming=streaming,
                             q_pos0=PAST_LEN, k_pos0=PAST_LEN - WINDOW)
    return pl.pallas_call(
        kern,
        grid=(m // tq,),
        in_specs=[
            pl.BlockSpec(memory_space=pltpu.SMEM),
            pl.BlockSpec((n_cols, tq, LANES), lambda i: (0, i, 0)),
            prev_spec, cur_spec, prev_spec, cur_spec,
            pl.BlockSpec((tq, d), lambda i: (i, 0)),
            pl.BlockSpec((None, None, 1, d), lambda i: (layer, 3, 0, 0)),
            pl.BlockSpec((None, d, d), lambda i: (j, 0, 0)),
        ],
        out_specs=pl.BlockSpec((tq, d), lambda i: (i, 0)),
        out_shape=jax.ShapeDtypeStruct((m, d), F32),
        scratch_shapes=[pltpu.VMEM((tq, d), BF16)],
        compiler_params=_params("arbitrary"),
        name="swa_attention",
    )(sinks, q, k_prev, k, v_prev, v, x, norm_g, w_o)


def _rope_tables(pos):
    half = HEAD_DIM // 2
    inv_freq = ROPE_THETA ** (-jnp.arange(half, dtype=F32) / half)
    ang = pos.astype(F32)[:, None] * inv_freq[None, :]
    cos = jnp.cos(ang)
    sin = jnp.sin(ang)
    cos_t = jnp.tile(cos, (1, LANES // half))
    sin_t = jnp.tile(jnp.concatenate([-sin, sin], axis=1), (1, HEADS_PER_COL))
    return cos_t, sin_t


def kernel(x_prompt, x_sample, state_conv, cache_k, cache_v, norm_g, w_ffn_in, w_ffn_out,
           w_conv_in, w_conv, w_conv_out, w_qkv, w_attn_out, attn_sinks):
    batch, seq, d = x_prompt.shape
    dec_batch, dec_seq, _ = x_sample.shape
    depth = norm_g.shape[0]
    n_heads = attn_sinks.shape[1]
    n_kv = cache_k.shape[3]
    kv_w = n_kv * HEAD_DIM
    assert batch == 1 and dec_seq == CHUNK and cache_k.shape[2] == WINDOW

    g4 = norm_g.reshape(depth, norm_g.shape[1], 1, d)
    w_ffn_in_b = w_ffn_in.astype(BF16)
    w_ffn_out_b = w_ffn_out.astype(BF16)
    w_conv_in_b = w_conv_in.astype(BF16)
    w_conv_out_b = w_conv_out.astype(BF16)
    w_qkv_b = w_qkv.astype(BF16)
    w_attn_out_b = w_attn_out.astype(BF16)

    xp = x_prompt.reshape(seq, d)
    xs = x_sample.reshape(dec_batch * dec_seq, d)
    cos_p, sin_p = _rope_tables(jnp.arange(seq))
    cos_s, sin_s = _rope_tables(jnp.tile(PAST_LEN + jnp.arange(dec_seq), dec_batch))

    tm = ROW_TILE
    conv_p, conv_s, k_p, v_p, k_s, v_s = [], [], [], [], [], []
    for layer in range(depth):
        j = layer // 2
        ffn = functools.partial(_ffn, norm_g=g4, w_in=w_ffn_in_b, w_out=w_ffn_out_b, layer=layer,
                                tm=FFN_TM, tf=FFN_TF)
        xp = ffn(xp, slot=0, n_pre=0, n_post=1)
        xs = ffn(xs, slot=0, n_pre=0, n_post=1)
        if layer % 2 == 0:
            mix = functools.partial(_conv_mixer, norm_g=g4, w_in=w_conv_in_b, conv_w=w_conv,
                                    w_out=w_conv_out_b, layer=layer, j=j, tm=tm, tn=CONV_TN)
            xp, st = mix(xp, jnp.zeros((1, 2, d), F32), seg=CONV_ROWS, carried=True)
            conv_p.append(st)
            xs, st = mix(xs, state_conv[j], seg=dec_seq, carried=False)
            conv_s.append(st)
        else:
            proj = functools.partial(_qkv_project, norm_g=g4, w_qkv=w_qkv_b, layer=layer, j=j,
                                     tm=tm, n_heads=n_heads, n_kv=n_kv)
            att = functools.partial(_attention, sinks=attn_sinks[j], norm_g=g4, w_o=w_attn_out_b,
                                    layer=layer, j=j, n_kv=n_kv)
            q, k, v = proj(xp, cos=cos_p, sin=sin_p)
            xp = att(xp, q, k, k, v, v, tq=tm, sq=WINDOW, streaming=True)
            k_p.append(k[seq - WINDOW:].reshape(1, WINDOW, n_kv, HEAD_DIM))
            v_p.append(v[seq - WINDOW:].reshape(1, WINDOW, n_kv, HEAD_DIM))
            q, k, v = proj(xs, cos=cos_s, sin=sin_s)
            ck = cache_k[j].reshape(dec_batch, WINDOW, kv_w)
            cv = cache_v[j].reshape(dec_batch, WINDOW, kv_w)
            xs = att(xs, q, ck, k, cv, v, tq=tm, sq=dec_seq, streaming=False)
            k_new = jnp.concatenate([ck, k.reshape(dec_batch, dec_seq, kv_w)], axis=1)[:, -WINDOW:]
            v_new = jnp.concatenate([cv, v.reshape(dec_batch, dec_seq, kv_w)], axis=1)[:, -WINDOW:]
            k_s.append(k_new.reshape(dec_batch, WINDOW, n_kv, HEAD_DIM))
            v_s.append(v_new.reshape(dec_batch, WINDOW, n_kv, HEAD_DIM))
        xp = ffn(xp, slot=1, n_pre=4, n_post=5)
        xs = ffn(xs, slot=1, n_pre=4, n_post=5)

    return (xp.reshape(batch, seq, d), xs.reshape(dec_batch, dec_seq, d),
            jnp.stack(conv_p), jnp.stack(conv_s),
            jnp.stack(k_p), jnp.stack(v_p), jnp.stack(k_s), jnp.stack(v_s))
```

```python
import functools

import jax
import jax.numpy as jnp
import numpy as np
from jax import lax
from jax.experimental import pallas as pl
from jax.experimental.pallas import tpu as pltpu

F32 = jnp.float32
BF16 = jnp.bfloat16

CHUNK = 64
LOOKBACK_CHUNKS = 2
WINDOW = LOOKBACK_CHUNKS * CHUNK
HEAD_DIM = 64
PAST_LEN = 2048
ROPE_THETA = 10000.0
NORM_EPS = 1e-6

LANES = 128
HEADS_PER_COL = LANES // HEAD_DIM
VMEM_LIMIT = 62 * 1024 * 1024
ROW_TILE = 512
FFN_TM = 1024
FFN_ROWS = 512
FFN_TF = 512
CONV_TN = 512
CONV_COLS = 256
ATTN_OUT_COLS = 256


def _rms(x, g):
    ms = jnp.mean(x * x, axis=-1, keepdims=True)
    return x * lax.rsqrt(ms + NORM_EPS) * g


def _params(*sem):
    return pltpu.CompilerParams(dimension_semantics=sem, vmem_limit_bytes=VMEM_LIMIT)


def _ffn_kernel(x_ref, gpre_ref, gpost_ref, wg_ref, wu_ref, wo_ref, o_ref, h_ref):
    j = pl.program_id(1)
    last = pl.num_programs(1) - 1

    def step(first, final):
        for r in range(0, x_ref.shape[0], FFN_ROWS):
            rows = slice(r, r + FFN_ROWS)
            if first:
                h = _rms(x_ref[rows], gpre_ref[...]).astype(BF16)
                h_ref[rows] = h
            else:
                h = h_ref[rows]
            gate = jnp.dot(h, wg_ref[...], preferred_element_type=F32)
            up = jnp.dot(h, wu_ref[...], preferred_element_type=F32)
            act = (gate * jax.nn.sigmoid(gate) * up).astype(BF16)
            part = jnp.dot(act, wo_ref[...], preferred_element_type=F32)
            if first:
                o_ref[rows] = part
            elif final:
                o_ref[rows] = x_ref[rows] + 0.5 * _rms(o_ref[rows] + part, gpost_ref[...])
            else:
                o_ref[rows] += part

    pl.when(j == 0)(functools.partial(step, True, False))
    pl.when((j > 0) & (j < last))(functools.partial(step, False, False))
    pl.when(j == last)(functools.partial(step, False, True))


def _ffn(x, norm_g, w_in, w_out, layer, slot, n_pre, n_post, *, tm, tf):
    m, d = x.shape
    nf = w_out.shape[2] // tf
    g_spec = lambda n: pl.BlockSpec((None, None, 1, d), lambda i, j: (layer, n, 0, 0))
    w_spec = lambda sec: pl.BlockSpec((None, None, d, tf), lambda i, j: (layer, slot, 0, j + sec * nf))
    return pl.pallas_call(
        _ffn_kernel,
        grid=(m // tm, nf),
        in_specs=[
            pl.BlockSpec((tm, d), lambda i, j: (i, 0)),
            g_spec(n_pre),
            g_spec(n_post),
            w_spec(0),
            w_spec(1),
            pl.BlockSpec((None, None, tf, d), lambda i, j: (layer, slot, j, 0)),
        ],
        out_specs=pl.BlockSpec((tm, d), lambda i, j: (i, 0)),
        out_shape=jax.ShapeDtypeStruct((m, d), F32),
        scratch_shapes=[pltpu.VMEM((tm, d), BF16)],
        compiler_params=_params("parallel", "arbitrary"),
        name="ffn",
    )(x, norm_g, norm_g, w_in, w_in, w_out)


def _conv_kernel(x_ref, gpre_ref, gpost_ref, wb_ref, wc_ref, wh_ref, cw_ref, wo_ref, st_ref,
                 o_ref, ns_ref, h_ref, carry_ref, *, seg, carried):
    i = pl.program_id(0)
    n = pl.program_id(1)
    last = pl.num_programs(1) - 1
    tm = x_ref.shape[0]
    tn = wb_ref.shape[1]
    tc = CONV_COLS
    cols = pl.ds(pl.multiple_of(n * tn, tn), tn)

    if carried:
        @pl.when(i == 0)
        def _():
            carry_ref[:, cols] = st_ref[0, :, cols]

    def gated_conv(c0, b_gate, c_gate, hh):
        gcols = pl.ds(pl.multiple_of(n * tn + c0, tc), tc)
        cw = cw_ref[:, c0:c0 + tc]
        w0, w1, w2 = cw[0:1], cw[1:2], cw[2:3]
        row = lax.broadcasted_iota(jnp.int32, (seg, tc), 0)
        u = c_gate * hh
        convs = []
        for s in range(tm // seg):
            us = u[s * seg:(s + 1) * seg]
            prev = carry_ref[:, gcols] if carried else st_ref[s, :, c0:c0 + tc]
            p1 = jnp.where(row == 0, prev[1:2], pltpu.roll(us, 1, 0))
            p2 = jnp.where(row == 0, prev[0:1], jnp.where(row == 1, prev[1:2], pltpu.roll(us, 2, 0)))
            convs.append(p2 * w0 + p1 * w1 + us * w2)
            if carried:
                carry_ref[:, gcols] = us[seg - 2:seg]
                ns_ref[0, :, gcols] = us[seg - 2:seg]
            else:
                ns_ref[s, :, c0:c0 + tc] = us[seg - 2:seg]
        conv = convs[0] if len(convs) == 1 else jnp.concatenate(convs, axis=0)
        return (b_gate * conv).astype(BF16)

    def out_proj(assign, final, c0, z):
        part = jnp.dot(z, wo_ref[c0:c0 + tc, :], preferred_element_type=F32)
        if assign:
            o_ref[...] = part
        elif final:
            o_ref[...] = x_ref[...] + _rms(o_ref[...] + part, gpost_ref[...])
        else:
            o_ref[...] += part

    def step(first, final):
        if first:
            h = _rms(x_ref[...], gpre_ref[...]).astype(BF16)
            h_ref[...] = h
        else:
            h = h_ref[...]
        pending = None
        for c0 in range(0, tn, tc):
            b_gate = jnp.dot(h, wb_ref[:, c0:c0 + tc], preferred_element_type=F32)
            c_gate = jnp.dot(h, wc_ref[:, c0:c0 + tc], preferred_element_type=F32)
            hh = jnp.dot(h, wh_ref[:, c0:c0 + tc], preferred_element_type=F32)
            if pending is not None:
                out_proj(first and pending[0] == 0, False, *pending)
            pending = (c0, gated_conv(c0, b_gate, c_gate, hh))
        out_proj(first and pending[0] == 0, final, *pending)

    pl.when(n == 0)(functools.partial(step, True, False))
    pl.when((n > 0) & (n < last))(functools.partial(step, False, False))
    pl.when(n == last)(functools.partial(step, False, True))


def _conv_mixer(x, state, norm_g, w_in, conv_w, w_out, layer, j, *, tm, tn, seg, carried):
    m, d = x.shape
    nn = d // tn
    nseg = tm // seg
    g_spec = lambda n: pl.BlockSpec((None, None, 1, d), lambda i, c: (layer, n, 0, 0))
    w_spec = lambda sec: pl.BlockSpec((None, d, tn), lambda i, c: (j, 0, c + sec * nn))
    if carried:
        st_spec = pl.BlockSpec((1, 2, d), lambda i, c: (0, 0, 0))
    else:
        st_spec = pl.BlockSpec((nseg, 2, tn), lambda i, c: (i, 0, c))
    return pl.pallas_call(
        functools.partial(_conv_kernel, seg=seg, carried=carried),
        grid=(m // tm, nn),
        in_specs=[
            pl.BlockSpec((tm, d), lambda i, c: (i, 0)),
            g_spec(2),
            g_spec(3),
            w_spec(0),
            w_spec(1),
            w_spec(2),
            pl.BlockSpec((None, 3, tn), lambda i, c: (j, 0, c)),
            pl.BlockSpec((None, tn, d), lambda i, c: (j, c, 0)),
            st_spec,
        ],
        out_specs=[pl.BlockSpec((tm, d), lambda i, c: (i, 0)), st_spec],
        out_shape=[
            jax.ShapeDtypeStruct((m, d), F32),
            jax.ShapeDtypeStruct(state.shape, F32),
        ],
        scratch_shapes=[pltpu.VMEM((tm, d), BF16), pltpu.VMEM((2, d), F32)],
        compiler_params=_params("arbitrary", "arbitrary"),
        name="conv_mixer",
    )(x, norm_g, norm_g, w_in, w_in, w_in, conv_w, w_out, state)


def _rope_col(xc, cos, sin_signed):
    lane = lax.broadcasted_iota(jnp.int32, xc.shape, 1)
    first_half = (lane % HEAD_DIM) < HEAD_DIM // 2
    swapped = jnp.where(first_half,
                        pltpu.roll(xc, LANES - HEAD_DIM // 2, 1),
                        pltpu.roll(xc, HEAD_DIM // 2, 1))
    return xc * cos + swapped * sin_signed


def _qkv_kernel(x_ref, g_ref, w_ref, cos_ref, sin_ref, q_ref, k_ref, v_ref, *, chunk):
    n_q = q_ref.shape[0] * LANES
    kv_w = k_ref.shape[1]
    h = _rms(x_ref[...], g_ref[...]).astype(BF16)
    cos = cos_ref[...]
    sin = sin_ref[...]
    for c0 in range(0, n_q, chunk):
        y = jnp.dot(h, w_ref[:, c0:c0 + chunk], preferred_element_type=F32)
        for c in range(chunk // LANES):
            qc = _rope_col(y[:, c * LANES:(c + 1) * LANES], cos, sin) * (HEAD_DIM ** -0.5)
            q_ref[c0 // LANES + c] = qc.astype(BF16)
    y = jnp.dot(h, w_ref[:, n_q:], preferred_element_type=F32)
    for c in range(kv_w // LANES):
        k_ref[:, c * LANES:(c + 1) * LANES] = _rope_col(y[:, c * LANES:(c + 1) * LANES], cos, sin)
    v_ref[...] = y[:, kv_w:]


def _qkv_project(x, norm_g, w_qkv, cos, sin, layer, j, *, tm, n_heads, n_kv):
    m, d = x.shape
    kv_w = n_kv * HEAD_DIM
    n_cols = n_heads // HEADS_PER_COL
    return pl.pallas_call(
        functools.partial(_qkv_kernel, chunk=2 * kv_w),
        grid=(m // tm,),
        in_specs=[
            pl.BlockSpec((tm, d), lambda i: (i, 0)),
            pl.BlockSpec((None, None, 1, d), lambda i: (layer, 2, 0, 0)),
            pl.BlockSpec((None, d, w_qkv.shape[2]), lambda i: (j, 0, 0)),
            pl.BlockSpec((tm, LANES), lambda i: (i, 0)),
            pl.BlockSpec((tm, LANES), lambda i: (i, 0)),
        ],
        out_specs=[
            pl.BlockSpec((n_cols, tm, LANES), lambda i: (0, i, 0)),
            pl.BlockSpec((tm, kv_w), lambda i: (i, 0)),
            pl.BlockSpec((tm, kv_w), lambda i: (i, 0)),
        ],
        out_shape=[
            jax.ShapeDtypeStruct((n_cols, m, LANES), BF16),
            jax.ShapeDtypeStruct((m, kv_w), F32),
            jax.ShapeDtypeStruct((m, kv_w), F32),
        ],
        compiler_params=_params("parallel"),
        name="qkv_rope",
    )(x, norm_g, w_qkv, cos, sin)


def _block_diag(col, rolled, first):
    lane = lax.broadcasted_iota(jnp.int32, col.shape, 1)
    lo = lane < HEAD_DIM
    zero = jnp.zeros_like(col)
    if first:
        return jnp.concatenate([jnp.where(lo, col, zero), jnp.where(lo, zero, rolled)], axis=0)
    return jnp.concatenate([jnp.where(lo, rolled, zero), jnp.where(lo, zero, col)], axis=0)


def _attn_kernel(sink_ref, q_ref, kp_ref, kc_ref, vp_ref, vc_ref, x_ref, g_ref, wo_ref, o_ref,
                 att_ref, *, sq, n_kv, streaming, q_pos0, k_pos0):
    i = pl.program_id(0)
    t = jnp.minimum(i, pl.num_programs(0) - 2)
    cur = lax.rem(i, 2)
    tq, d = x_ref.shape
    nsub = tq // sq
    nk = WINDOW + sq
    n_cols = q_ref.shape[0]
    cols_per_kv = n_cols // n_kv
    n_blocks = nsub * n_kv
    blocks_per_chunk = n_blocks // (d // ATTN_OUT_COLS)

    @pl.when(i == 0)
    def _():
        att_ref[1] = jnp.zeros((tq, d), BF16)

    if streaming:
        k_all = jnp.concatenate([kp_ref[...], kc_ref[...]], axis=0)
        v_all = jnp.concatenate([vp_ref[...], vc_ref[...]], axis=0)

    row = lax.broadcasted_iota(jnp.int32, (sq, nk), 0)
    colk = lax.broadcasted_iota(jnp.int32, (sq, nk), 1)

    @functools.cache
    def sub_tile(s):
        if streaming:
            q0 = t * tq + s * sq
            k0 = q0 - WINDOW
            ks = k_all[s * sq:s * sq + nk]
            vs = v_all[s * sq:s * sq + nk]
        else:
            q0 = q_pos0
            k0 = k_pos0
            ks = jnp.concatenate([kp_ref[s], kc_ref[s * sq:(s + 1) * sq]], axis=0)
            vs = jnp.concatenate([vp_ref[s], vc_ref[s * sq:(s + 1) * sq]], axis=0)
        kpos = k0 + colk
        q_chunk = lax.shift_right_arithmetic(q0 + row, 6)
        k_chunk = lax.shift_right_arithmetic(kpos, 6)
        valid = (k_chunk >= q_chunk - LOOKBACK_CHUNKS) & (k_chunk <= q_chunk)
        if streaming:
            valid = valid & (kpos >= 0)
        return ks, vs, valid

    def scores(s, kv):
        ks, _, _ = sub_tile(s)
        c, first = divmod(kv, HEADS_PER_COL)
        kcol = ks[:, c * LANES:(c + 1) * LANES]
        bdk = _block_diag(kcol, pltpu.roll(kcol, HEAD_DIM, 1), first == 0).astype(BF16)
        qs = q_ref[kv * cols_per_kv:(kv + 1) * cols_per_kv, s * sq:(s + 1) * sq, :]
        qs = qs.reshape(cols_per_kv * sq, LANES)
        return lax.dot_general(qs, bdk, (((1,), (1,)), ((), ())), preferred_element_type=F32)

    def attend(s, kv, sc):
        _, vs, valid = sub_tile(s)
        c, first = divmod(kv, HEADS_PER_COL)
        vcol = vs[:, c * LANES:(c + 1) * LANES]
        bdv = _block_diag(vcol, pltpu.roll(vcol, HEAD_DIM, 1), first == 0).astype(BF16)
        p_rows = []
        for p in range(cols_per_kv):
            halves = []
            for e in range(HEADS_PER_COL):
                sink = sink_ref[(kv * cols_per_kv + p) * HEADS_PER_COL + e]
                blk = sc[p * sq:(p + 1) * sq, e * nk:(e + 1) * nk]
                blk = jnp.where(valid, blk, -jnp.inf)
                mx = jnp.maximum(jnp.max(blk, axis=-1, keepdims=True), sink)
                ex = jnp.exp(blk - mx)
                den = jnp.sum(ex, axis=-1, keepdims=True) + jnp.exp(sink - mx)
                halves.append((ex * (1.0 / den)).astype(BF16))
            p_rows.append(jnp.concatenate(halves, axis=1))
        pm = jnp.concatenate(p_rows, axis=0)
        ov = jnp.dot(pm, bdv, preferred_element_type=F32)
        for p in range(cols_per_kv):
            cc = kv * cols_per_kv + p
            att_ref[cur, s * sq:(s + 1) * sq, cc * LANES:(cc + 1) * LANES] = (
                ov[p * sq:(p + 1) * sq].astype(BF16))

    blocks = [(s, kv) for s in range(nsub) for kv in range(n_kv)]
    sc_next = scores(*blocks[0])
    for b, (s, kv) in enumerate(blocks):
        sc = sc_next
        if b + 1 < n_blocks:
            sc_next = scores(*blocks[b + 1])
        if (b + 1) % blocks_per_chunk == 0:
            oc = ((b + 1) // blocks_per_chunk - 1) * ATTN_OUT_COLS
            o_ref[:, oc:oc + ATTN_OUT_COLS] = jnp.dot(
                att_ref[1 - cur], wo_ref[:, oc:oc + ATTN_OUT_COLS], preferred_element_type=F32)
        attend(s, kv, sc)

    o_ref[...] = x_ref[...] + _rms(o_ref[...], g_ref[...])


def _attention(x, q, k_prev, k, v_prev, v, sinks, norm_g, w_o, layer, j, *, tq, sq, n_kv,
               streaming):
    m, d = x.shape
    n_cols = q.shape[0]
    kv_w = k.shape[1]
    nsub = tq // sq
    n_tiles = m // tq
    att_tile = lambda i: jnp.minimum(i, n_tiles - 1)
    out_tile = lambda i: jnp.maximum(i - 1, 0)
    if streaming:
        per = tq // WINDOW
        prev_spec = pl.BlockSpec((WINDOW, kv_w), lambda i: (jnp.maximum(att_tile(i) * per - 1, 0), 0))
    else:
        prev_spec = pl.BlockSpec((nsub, WINDOW, kv_w), lambda i: (att_tile(i), 0, 0))
    cur_spec = pl.BlockSpec((tq, kv_w), lambda i: (att_tile(i), 0))
    kern = functools.partial(_attn_kernel, sq=sq, n_kv=n_kv, streaming=streaming,
                             q_pos0=PAST_LEN, k_pos0=PAST_LEN - WINDOW)
    return pl.pallas_call(
        kern,
        grid=(n_tiles + 1,),
        in_specs=[
            pl.BlockSpec(memory_space=pltpu.SMEM),
            pl.BlockSpec((n_cols, tq, LANES), lambda i: (0, att_tile(i), 0)),
            prev_spec, cur_spec, prev_spec, cur_spec,
            pl.BlockSpec((tq, d), lambda i: (out_tile(i), 0)),
            pl.BlockSpec((None, None, 1, d), lambda i: (layer, 3, 0, 0)),
            pl.BlockSpec((None, d, d), lambda i: (j, 0, 0)),
        ],
        out_specs=pl.BlockSpec((tq, d), lambda i: (out_tile(i), 0)),
        out_shape=jax.ShapeDtypeStruct((m, d), F32),
        scratch_shapes=[pltpu.VMEM((2, tq, d), BF16)],
        compiler_params=_params("arbitrary"),
        name="swa_attention",
    )(sinks, q, k_prev, k, v_prev, v, x, norm_g, w_o)


def _rope_tables(pos):
    half = HEAD_DIM // 2
    inv_freq = ROPE_THETA ** (-jnp.arange(half, dtype=F32) / half)
    ang = pos.astype(F32)[:, None] * inv_freq[None, :]
    cos = jnp.cos(ang)
    sin = jnp.sin(ang)
    cos_t = jnp.tile(cos, (1, LANES // half))
    sin_t = jnp.tile(jnp.concatenate([-sin, sin], axis=1), (1, HEADS_PER_COL))
    return cos_t, sin_t


def kernel(x_prompt, x_sample, state_conv, cache_k, cache_v, norm_g, w_ffn_in, w_ffn_out,
           w_conv_in, w_conv, w_conv_out, w_qkv, w_attn_out, attn_sinks):
    batch, seq, d = x_prompt.shape
    dec_batch, dec_seq, _ = x_sample.shape
    depth = norm_g.shape[0]
    n_heads = attn_sinks.shape[1]
    n_kv = cache_k.shape[3]
    kv_w = n_kv * HEAD_DIM
    assert batch == 1 and dec_seq == CHUNK and cache_k.shape[2] == WINDOW

    g4 = norm_g.reshape(depth, norm_g.shape[1], 1, d)
    w_ffn_in_b = w_ffn_in.astype(BF16)
    w_ffn_out_b = w_ffn_out.astype(BF16)
    w_conv_in_b = w_conv_in.astype(BF16)
    w_conv_out_b = w_conv_out.astype(BF16)
    w_qkv_b = w_qkv.astype(BF16)
    w_attn_out_b = w_attn_out.astype(BF16)

    xp = x_prompt.reshape(seq, d)
    xs = x_sample.reshape(dec_batch * dec_seq, d)
    cos_p, sin_p = _rope_tables(jnp.arange(seq))
    cos_s, sin_s = _rope_tables(jnp.tile(PAST_LEN + jnp.arange(dec_seq), dec_batch))

    tm = ROW_TILE
    conv_p, conv_s, k_p, v_p, k_s, v_s = [], [], [], [], [], []
    for layer in range(depth):
        j = layer // 2
        ffn = functools.partial(_ffn, norm_g=g4, w_in=w_ffn_in_b, w_out=w_ffn_out_b, layer=layer,
                                tm=FFN_TM, tf=FFN_TF)
        xp = ffn(xp, slot=0, n_pre=0, n_post=1)
        xs = ffn(xs, slot=0, n_pre=0, n_post=1)
        if layer % 2 == 0:
            mix = functools.partial(_conv_mixer, norm_g=g4, w_in=w_conv_in_b, conv_w=w_conv,
                                    w_out=w_conv_out_b, layer=layer, j=j, tm=tm, tn=CONV_TN)
            xp, st = mix(xp, jnp.zeros((1, 2, d), F32), seg=tm, carried=True)
            conv_p.append(st)
            xs, st = mix(xs, state_conv[j], seg=dec_seq, carried=False)
            conv_s.append(st)
        else:
            proj = functools.partial(_qkv_project, norm_g=g4, w_qkv=w_qkv_b, layer=layer, j=j,
                                     tm=tm, n_heads=n_heads, n_kv=n_kv)
            att = functools.partial(_attention, sinks=attn_sinks[j], norm_g=g4, w_o=w_attn_out_b,
                                    layer=layer, j=j, n_kv=n_kv)
            q, k, v = proj(xp, cos=cos_p, sin=sin_p)
            xp = att(xp, q, k, k, v, v, tq=tm, sq=WINDOW, streaming=True)
            k_p.append(k[seq - WINDOW:].reshape(1, WINDOW, n_kv, HEAD_DIM))
            v_p.append(v[seq - WINDOW:].reshape(1, WINDOW, n_kv, HEAD_DIM))
            q, k, v = proj(xs, cos=cos_s, sin=sin_s)
            ck = cache_k[j].reshape(dec_batch, WINDOW, kv_w)
            cv = cache_v[j].reshape(dec_batch, WINDOW, kv_w)
            xs = att(xs, q, ck, k, cv, v, tq=tm, sq=dec_seq, streaming=False)
            k_new = jnp.concatenate([ck, k.reshape(dec_batch, dec_seq, kv_w)], axis=1)[:, -WINDOW:]
            v_new = jnp.concatenate([cv, v.reshape(dec_batch, dec_seq, kv_w)], axis=1)[:, -WINDOW:]
            k_s.append(k_new.reshape(dec_batch, WINDOW, n_kv, HEAD_DIM))
            v_s.append(v_new.reshape(dec_batch, WINDOW, n_kv, HEAD_DIM))
        xp = ffn(xp, slot=1, n_pre=4, n_post=5)
        xs = ffn(xs, slot=1, n_pre=4, n_post=5)

    return (xp.reshape(batch, seq, d), xs.reshape(dec_batch, dec_seq, d),
            jnp.stack(conv_p), jnp.stack(conv_s),
            jnp.stack(k_p), jnp.stack(v_p), jnp.stack(k_s), jnp.stack(v_s))
```

```python
import functools

import jax
import jax.numpy as jnp
import numpy as np
from jax import lax
from jax.experimental import pallas as pl
from jax.experimental.pallas import tpu as pltpu

F32 = jnp.float32
BF16 = jnp.bfloat16

CHUNK = 64
LOOKBACK_CHUNKS = 2
WINDOW = LOOKBACK_CHUNKS * CHUNK
HEAD_DIM = 64
PAST_LEN = 2048
ROPE_THETA = 10000.0
NORM_EPS = 1e-6

LANES = 128
HEADS_PER_COL = LANES // HEAD_DIM
VMEM_LIMIT = 62 * 1024 * 1024
ROW_TILE = 512
FFN_TM = 1024
FFN_ROWS = 512
FFN_TF = 512
CONV_TN = 512
CONV_COLS = 256
ATTN_OUT_COLS = 256
ATTN_MIN_LAGGED_TILES = 4


def _rms(x, g):
    ms = jnp.mean(x * x, axis=-1, keepdims=True)
    return x * lax.rsqrt(ms + NORM_EPS) * g


def _params(*sem):
    return pltpu.CompilerParams(dimension_semantics=sem, vmem_limit_bytes=VMEM_LIMIT)


def _ffn_kernel(*refs, cast_next):
    x_ref, gpre_ref, gpost_ref, wg_ref, wu_ref, wo_ref = refs[:6]
    if cast_next:
        nwi_ref, nwo_ref, o_ref, nwi_out, nwo_out, h_ref = refs[6:]
    else:
        o_ref, h_ref = refs[6:]
    j = pl.program_id(1)
    last = pl.num_programs(1) - 1

    def step(first, final):
        if cast_next:
            nwi_out[...] = nwi_ref[...].astype(BF16)
            nwo_out[...] = nwo_ref[...].astype(BF16)
        for r in range(0, x_ref.shape[0], FFN_ROWS):
            rows = slice(r, r + FFN_ROWS)
            if first:
                h = _rms(x_ref[rows], gpre_ref[...]).astype(BF16)
                h_ref[rows] = h
            else:
                h = h_ref[rows]
            gate = jnp.dot(h, wg_ref[...], preferred_element_type=F32)
            up = jnp.dot(h, wu_ref[...], preferred_element_type=F32)
            act = (gate * jax.nn.sigmoid(gate) * up).astype(BF16)
            part = jnp.dot(act, wo_ref[...], preferred_element_type=F32)
            if first:
                o_ref[rows] = part
            elif final:
                o_ref[rows] = x_ref[rows] + 0.5 * _rms(o_ref[rows] + part, gpost_ref[...])
            else:
                o_ref[rows] += part

    pl.when(j == 0)(functools.partial(step, True, False))
    pl.when((j > 0) & (j < last))(functools.partial(step, False, False))
    pl.when(j == last)(functools.partial(step, False, True))


def _ffn(x, norm_g, w_in, w_out, layer, n_pre, n_post, *, tm, tf, next_w=None):
    m, d = x.shape
    f = w_out.shape[0]
    nf = f // tf
    n_steps = (m // tm) * nf
    g_spec = lambda n: pl.BlockSpec((None, None, 1, d), lambda i, j: (layer, n, 0, 0))
    w_spec = lambda sec: pl.BlockSpec((d, tf), lambda i, j: (0, j + sec * nf))
    in_specs = [
        pl.BlockSpec((tm, d), lambda i, j: (i, 0)),
        g_spec(n_pre),
        g_spec(n_post),
        w_spec(0),
        w_spec(1),
        pl.BlockSpec((tf, d), lambda i, j: (j, 0)),
    ]
    operands = [x, norm_g, norm_g, w_in, w_in, w_out]
    out_specs = [pl.BlockSpec((tm, d), lambda i, j: (i, 0))]
    out_shape = [jax.ShapeDtypeStruct((m, d), F32)]
    if next_w is not None:
        nw_in, nw_out, nl, ns = next_w
        bi = (d // (m // tm), 2 * f // nf)
        wo_cols = d // 2
        bo = (f * d // (n_steps * wo_cols), wo_cols)
        assert bi[0] * (m // tm) == d and bi[1] * nf == 2 * f and bo[0] * n_steps * wo_cols == f * d
        step_of = lambda i, j: i * nf + j
        in_specs += [
            pl.BlockSpec((None, None) + bi, lambda i, j: (nl, ns, i, j)),
            pl.BlockSpec((None, None) + bo, lambda i, j: (nl, ns, step_of(i, j) // 2, step_of(i, j) % 2)),
        ]
        operands += [nw_in, nw_out]
        out_specs += [
            pl.BlockSpec(bi, lambda i, j: (i, j)),
            pl.BlockSpec(bo, lambda i, j: (step_of(i, j) // 2, step_of(i, j) % 2)),
        ]
        out_shape += [jax.ShapeDtypeStruct((d, 2 * f), BF16), jax.ShapeDtypeStruct((f, d), BF16)]
    outs = pl.pallas_call(
        functools.partial(_ffn_kernel, cast_next=next_w is not None),
        grid=(m // tm, nf),
        in_specs=in_specs,
        out_specs=out_specs,
        out_shape=out_shape,
        scratch_shapes=[pltpu.VMEM((tm, d), BF16)],
        compiler_params=_params("parallel", "arbitrary"),
        name="ffn",
    )(*operands)
    return outs if next_w is not None else outs[0]


def _conv_kernel(x_ref, gpre_ref, gpost_ref, wb_ref, wc_ref, wh_ref, cw_ref, wo_ref, st_ref,
                 o_ref, ns_ref, h_ref, carry_ref, *, seg, carried):
    i = pl.program_id(0)
    n = pl.program_id(1)
    last = pl.num_programs(1) - 1
    tm = x_ref.shape[0]
    tn = wb_ref.shape[1]
    tc = CONV_COLS
    cols = pl.ds(pl.multiple_of(n * tn, tn), tn)

    if carried:
        @pl.when(i == 0)
        def _():
            carry_ref[:, cols] = st_ref[0, :, cols]

    def gated_conv(c0, b_gate, c_gate, hh):
        gcols = pl.ds(pl.multiple_of(n * tn + c0, tc), tc)
        cw = cw_ref[:, c0:c0 + tc]
        w0, w1, w2 = cw[0:1], cw[1:2], cw[2:3]
        row = lax.broadcasted_iota(jnp.int32, (seg, tc), 0)
        u = c_gate * hh
        convs = []
        for s in range(tm // seg):
            us = u[s * seg:(s + 1) * seg]
            prev = carry_ref[:, gcols] if carried else st_ref[s, :, c0:c0 + tc]
            p1 = jnp.where(row == 0, prev[1:2], pltpu.roll(us, 1, 0))
            p2 = jnp.where(row == 0, prev[0:1], jnp.where(row == 1, prev[1:2], pltpu.roll(us, 2, 0)))
            convs.append(p2 * w0 + p1 * w1 + us * w2)
            if carried:
                carry_ref[:, gcols] = us[seg - 2:seg]
                ns_ref[0, :, gcols] = us[seg - 2:seg]
            else:
                ns_ref[s, :, c0:c0 + tc] = us[seg - 2:seg]
        conv = convs[0] if len(convs) == 1 else jnp.concatenate(convs, axis=0)
        return (b_gate * conv).astype(BF16)

    def out_proj(assign, final, c0, z):
        part = jnp.dot(z, wo_ref[c0:c0 + tc, :], preferred_element_type=F32)
        if assign:
            o_ref[...] = part
        elif final:
            o_ref[...] = x_ref[...] + _rms(o_ref[...] + part, gpost_ref[...])
        else:
            o_ref[...] += part

    def step(first, final):
        if first:
            h = _rms(x_ref[...], gpre_ref[...]).astype(BF16)
            h_ref[...] = h
        else:
            h = h_ref[...]
        pending = None
        for c0 in range(0, tn, tc):
            b_gate = jnp.dot(h, wb_ref[:, c0:c0 + tc], preferred_element_type=F32)
            c_gate = jnp.dot(h, wc_ref[:, c0:c0 + tc], preferred_element_type=F32)
            hh = jnp.dot(h, wh_ref[:, c0:c0 + tc], preferred_element_type=F32)
            if pending is not None:
                out_proj(first and pending[0] == 0, False, *pending)
            pending = (c0, gated_conv(c0, b_gate, c_gate, hh))
        out_proj(first and pending[0] == 0, final, *pending)

    if tn == o_ref.shape[1]:
        step(True, True)
    else:
        pl.when(n == 0)(functools.partial(step, True, False))
        pl.when((n > 0) & (n < last))(functools.partial(step, False, False))
        pl.when(n == last)(functools.partial(step, False, True))


def _conv_mixer(x, state, norm_g, w_in, conv_w, w_out, layer, j, *, tm, tn, seg, carried):
    m, d = x.shape
    nn = d // tn
    nseg = tm // seg
    g_spec = lambda n: pl.BlockSpec((None, None, 1, d), lambda i, c: (layer, n, 0, 0))
    once = dict(pipeline_mode=pl.Buffered(1)) if nn == 1 else {}
    w_spec = lambda sec: pl.BlockSpec((None, d, tn), lambda i, c: (j, 0, c + sec * nn), **once)
    if carried:
        st_spec = pl.BlockSpec((1, 2, d), lambda i, c: (0, 0, 0))
    else:
        st_spec = pl.BlockSpec((nseg, 2, tn), lambda i, c: (i, 0, c))
    return pl.pallas_call(
        functools.partial(_conv_kernel, seg=seg, carried=carried),
        grid=(m // tm, nn),
        in_specs=[
            pl.BlockSpec((tm, d), lambda i, c: (i, 0)),
            g_spec(2),
            g_spec(3),
            w_spec(0),
            w_spec(1),
            w_spec(2),
            pl.BlockSpec((None, 3, tn), lambda i, c: (j, 0, c)),
            pl.BlockSpec((None, tn, d), lambda i, c: (j, c, 0), **once),
            st_spec,
        ],
        out_specs=[pl.BlockSpec((tm, d), lambda i, c: (i, 0)), st_spec],
        out_shape=[
            jax.ShapeDtypeStruct((m, d), F32),
            jax.ShapeDtypeStruct(state.shape, F32),
        ],
        scratch_shapes=[pltpu.VMEM((tm, d), BF16), pltpu.VMEM((2, d), F32)],
        compiler_params=_params("arbitrary", "arbitrary"),
        name="conv_mixer",
    )(x, norm_g, norm_g, w_in, w_in, w_in, conv_w, w_out, state)


def _rope_col(xc, cos, sin_signed):
    lane = lax.broadcasted_iota(jnp.int32, xc.shape, 1)
    first_half = (lane % HEAD_DIM) < HEAD_DIM // 2
    swapped = jnp.where(first_half,
                        pltpu.roll(xc, LANES - HEAD_DIM // 2, 1),
                        pltpu.roll(xc, HEAD_DIM // 2, 1))
    return xc * cos + swapped * sin_signed


def _qkv_kernel(x_ref, g_ref, w_ref, cos_ref, sin_ref, q_ref, k_ref, v_ref, *, chunk):
    n_q = q_ref.shape[0] * LANES
    kv_w = k_ref.shape[1]
    h = _rms(x_ref[...], g_ref[...]).astype(BF16)
    cos = cos_ref[...]
    sin = sin_ref[...]
    for c0 in range(0, n_q, chunk):
        y = jnp.dot(h, w_ref[:, c0:c0 + chunk], preferred_element_type=F32)
        for c in range(chunk // LANES):
            qc = _rope_col(y[:, c * LANES:(c + 1) * LANES], cos, sin) * (HEAD_DIM ** -0.5)
            q_ref[c0 // LANES + c] = qc.astype(BF16)
    y = jnp.dot(h, w_ref[:, n_q:], preferred_element_type=F32)
    for c in range(kv_w // LANES):
        k_ref[:, c * LANES:(c + 1) * LANES] = _rope_col(y[:, c * LANES:(c + 1) * LANES], cos, sin)
    v_ref[...] = y[:, kv_w:]


def _qkv_project(x, norm_g, w_qkv, cos, sin, layer, j, *, tm, n_heads, n_kv):
    m, d = x.shape
    kv_w = n_kv * HEAD_DIM
    n_cols = n_heads // HEADS_PER_COL
    return pl.pallas_call(
        functools.partial(_qkv_kernel, chunk=2 * kv_w),
        grid=(m // tm,),
        in_specs=[
            pl.BlockSpec((tm, d), lambda i: (i, 0)),
            pl.BlockSpec((None, None, 1, d), lambda i: (layer, 2, 0, 0)),
            pl.BlockSpec((None, d, w_qkv.shape[2]), lambda i: (j, 0, 0)),
            pl.BlockSpec((tm, LANES), lambda i: (i, 0)),
            pl.BlockSpec((tm, LANES), lambda i: (i, 0)),
        ],
        out_specs=[
            pl.BlockSpec((n_cols, tm, LANES), lambda i: (0, i, 0)),
            pl.BlockSpec((tm, kv_w), lambda i: (i, 0)),
            pl.BlockSpec((tm, kv_w), lambda i: (i, 0)),
        ],
        out_shape=[
            jax.ShapeDtypeStruct((n_cols, m, LANES), BF16),
            jax.ShapeDtypeStruct((m, kv_w), F32),
            jax.ShapeDtypeStruct((m, kv_w), F32),
        ],
        compiler_params=_params("parallel"),
        name="qkv_rope",
    )(x, norm_g, w_qkv, cos, sin)


def _block_diag(col, rolled, first):
    lane = lax.broadcasted_iota(jnp.int32, col.shape, 1)
    lo = lane < HEAD_DIM
    zero = jnp.zeros_like(col)
    if first:
        return jnp.concatenate([jnp.where(lo, col, zero), jnp.where(lo, zero, rolled)], axis=0)
    return jnp.concatenate([jnp.where(lo, rolled, zero), jnp.where(lo, zero, col)], axis=0)


def _attn_kernel(sink_ref, q_ref, kp_ref, kc_ref, vp_ref, vc_ref, x_ref, g_ref, wo_ref, o_ref,
                 att_ref, *, sq, n_kv, streaming, lag, q_pos0, k_pos0):
    i = pl.program_id(0)
    t = jnp.minimum(i, pl.num_programs(0) - 1 - lag)
    cur = lax.rem(i, 2) if lag else 0
    tq, d = x_ref.shape
    nsub = tq // sq
    nk_real = WINDOW + sq
    nk = -(-nk_real // LANES) * LANES
    assert nk == nk_real or not streaming
    n_cols = q_ref.shape[0]
    cols_per_kv = n_cols // n_kv
    n_blocks = nsub * n_kv
    blocks_per_chunk = n_blocks // (d // ATTN_OUT_COLS)

    if lag:
        @pl.when(i == 0)
        def _():
            att_ref[1] = jnp.zeros((tq, d), BF16)

    if streaming:
        k_all = jnp.concatenate([kp_ref[...], kc_ref[...]], axis=0)
        v_all = jnp.concatenate([vp_ref[...], vc_ref[...]], axis=0)

    row = lax.broadcasted_iota(jnp.int32, (sq, nk), 0)
    colk = lax.broadcasted_iota(jnp.int32, (sq, nk), 1)

    @functools.cache
    def sub_tile(s):
        if streaming:
            q0 = t * tq + s * sq
            k0 = q0 - WINDOW
            ks = k_all[s * sq:s * sq + nk]
            vs = v_all[s * sq:s * sq + nk]
        else:
            q0 = q_pos0
            k0 = k_pos0
            pad = [jnp.zeros((nk - nk_real, kp_ref.shape[2]), F32)] if nk > nk_real else []
            ks = jnp.concatenate([kp_ref[s], kc_ref[s * sq:(s + 1) * sq]] + pad, axis=0)
            vs = jnp.concatenate([vp_ref[s], vc_ref[s * sq:(s + 1) * sq]] + pad, axis=0)
        kpos = k0 + colk
        q_chunk = lax.shift_right_arithmetic(q0 + row, 6)
        k_chunk = lax.shift_right_arithmetic(kpos, 6)
        valid = (k_chunk >= q_chunk - LOOKBACK_CHUNKS) & (k_chunk <= q_chunk)
        if streaming:
            valid = valid & (kpos >= 0)
        if nk > nk_real:
            valid = valid & (colk < nk_real)
        return ks, vs, valid

    def scores(s, kv):
        ks, _, _ = sub_tile(s)
        c, first = divmod(kv, HEADS_PER_COL)
        kcol = ks[:, c * LANES:(c + 1) * LANES]
        bdk = _block_diag(kcol, pltpu.roll(kcol, HEAD_DIM, 1), first == 0).astype(BF16)
        qs = q_ref[kv * cols_per_kv:(kv + 1) * cols_per_kv, s * sq:(s + 1) * sq, :]
        qs = qs.reshape(cols_per_kv * sq, LANES)
        return lax.dot_general(qs, bdk, (((1,), (1,)), ((), ())), preferred_element_type=F32)

    def attend(s, kv, sc):
        _, vs, valid = sub_tile(s)
        c, first = divmod(kv, HEADS_PER_COL)
        vcol = vs[:, c * LANES:(c + 1) * LANES]
        bdv = _block_diag(vcol, pltpu.roll(vcol, HEAD_DIM, 1), first == 0).astype(BF16)
        p_rows = []
        for p in range(cols_per_kv):
            halves = []
            for e in range(HEADS_PER_COL):
                sink = sink_ref[(kv * cols_per_kv + p) * HEADS_PER_COL + e]
                blk = sc[p * sq:(p + 1) * sq, e * nk:(e + 1) * nk]
                blk = jnp.where(valid, blk, -jnp.inf)
                mx = jnp.maximum(jnp.max(blk, axis=-1, keepdims=True), sink)
                ex = jnp.exp(blk - mx)
                den = jnp.sum(ex, axis=-1, keepdims=True) + jnp.exp(sink - mx)
                halves.append((ex * (1.0 / den)).astype(BF16))
            p_rows.append(jnp.concatenate(halves, axis=1))
        pm = jnp.concatenate(p_rows, axis=0)
        ov = jnp.dot(pm, bdv, preferred_element_type=F32)
        for p in range(cols_per_kv):
            cc = kv * cols_per_kv + p
            att_ref[cur, s * sq:(s + 1) * sq, cc * LANES:(cc + 1) * LANES] = (
                ov[p * sq:(p + 1) * sq].astype(BF16))

    blocks = [(s, kv) for s in range(nsub) for kv in range(n_kv)]
    sc_next = scores(*blocks[0])
    for b, (s, kv) in enumerate(blocks):
        sc = sc_next
        if b + 1 < n_blocks:
            sc_next = scores(*blocks[b + 1])
        if lag and (b + 1) % blocks_per_chunk == 0:
            oc = ((b + 1) // blocks_per_chunk - 1) * ATTN_OUT_COLS
            o_ref[:, oc:oc + ATTN_OUT_COLS] = jnp.dot(
                att_ref[1 - cur], wo_ref[:, oc:oc + ATTN_OUT_COLS], preferred_element_type=F32)
        attend(s, kv, sc)

    if not lag:
        o_ref[...] = jnp.dot(att_ref[0], wo_ref[...], preferred_element_type=F32)
    o_ref[...] = x_ref[...] + _rms(o_ref[...], g_ref[...])


def _attention(x, q, k_prev, k, v_prev, v, sinks, norm_g, w_o, layer, j, *, tq, sq, n_kv,
               streaming):
    m, d = x.shape
    n_cols = q.shape[0]
    kv_w = k.shape[1]
    nsub = tq // sq
    n_tiles = m // tq
    lag = int(n_tiles >= ATTN_MIN_LAGGED_TILES)
    att_tile = lambda i: jnp.minimum(i, n_tiles - 1)
    out_tile = lambda i: jnp.maximum(i - lag, 0)
    if streaming:
        per = tq // WINDOW
        prev_spec = pl.BlockSpec((WINDOW, kv_w), lambda i: (jnp.maximum(att_tile(i) * per - 1, 0), 0))
    else:
        prev_spec = pl.BlockSpec((nsub, WINDOW, kv_w), lambda i: (att_tile(i), 0, 0))
    cur_spec = pl.BlockSpec((tq, kv_w), lambda i: (att_tile(i), 0))
    kern = functools.partial(_attn_kernel, sq=sq, n_kv=n_kv, streaming=streaming, lag=lag,
                             q_pos0=PAST_LEN, k_pos0=PAST_LEN - WINDOW)
    return pl.pallas_call(
        kern,
        grid=(n_tiles + lag,),
        in_specs=[
            pl.BlockSpec(memory_space=pltpu.SMEM),
            pl.BlockSpec((n_cols, tq, LANES), lambda i: (0, att_tile(i), 0)),
            prev_spec, cur_spec, prev_spec, cur_spec,
            pl.BlockSpec((tq, d), lambda i: (out_tile(i), 0)),
            pl.BlockSpec((None, None, 1, d), lambda i: (layer, 3, 0, 0)),
            pl.BlockSpec((None, d, d), lambda i: (j, 0, 0)),
        ],
        out_specs=pl.BlockSpec((tq, d), lambda i: (out_tile(i), 0)),
        out_shape=jax.ShapeDtypeStruct((m, d), F32),
        scratch_shapes=[pltpu.VMEM((1 + lag, tq, d), BF16)],
        compiler_params=_params("arbitrary"),
        name="swa_attention",
    )(sinks, q, k_prev, k, v_prev, v, x, norm_g, w_o)


def _rope_tables(pos):
    half = HEAD_DIM // 2
    inv_freq = ROPE_THETA ** (-jnp.arange(half, dtype=F32) / half)
    ang = pos.astype(F32)[:, None] * inv_freq[None, :]
    cos = jnp.cos(ang)
    sin = jnp.sin(ang)
    cos_t = jnp.tile(cos, (1, LANES // half))
    sin_t = jnp.tile(jnp.concatenate([-sin, sin], axis=1), (1, HEADS_PER_COL))
    return cos_t, sin_t


def kernel(x_prompt, x_sample, state_conv, cache_k, cache_v, norm_g, w_ffn_in, w_ffn_out,
           w_conv_in, w_conv, w_conv_out, w_qkv, w_attn_out, attn_sinks):
    batch, seq, d = x_prompt.shape
    dec_batch, dec_seq, _ = x_sample.shape
    depth = norm_g.shape[0]
    n_heads = attn_sinks.shape[1]
    n_kv = cache_k.shape[3]
    kv_w = n_kv * HEAD_DIM
    assert batch == 1 and dec_seq == CHUNK and cache_k.shape[2] == WINDOW

    g4 = norm_g.reshape(depth, norm_g.shape[1], 1, d)
    ffn_order = [(layer, slot) for layer in range(depth) for slot in range(2)]
    ffn_w = {ffn_order[0]: (w_ffn_in[0, 0].astype(BF16), w_ffn_out[0, 0].astype(BF16))}
    w_conv_in_b = w_conv_in.astype(BF16)
    w_conv_out_b = w_conv_out.astype(BF16)
    w_qkv_b = w_qkv.astype(BF16)
    w_attn_out_b = w_attn_out.astype(BF16)

    xp = x_prompt.reshape(seq, d)
    xs = x_sample.reshape(dec_batch * dec_seq, d)
    cos_p, sin_p = _rope_tables(jnp.arange(seq))
    cos_s, sin_s = _rope_tables(jnp.tile(PAST_LEN + jnp.arange(dec_seq), dec_batch))

    tm = ROW_TILE
    conv_p, conv_s, k_p, v_p, k_s, v_s = [], [], [], [], [], []
    for layer in range(depth):
        j = layer // 2
        def ffn_pair(xp, xs, slot, n_pre, n_post):
            w_in_b, w_out_b = ffn_w[(layer, slot)]
            ffn = functools.partial(_ffn, norm_g=g4, w_in=w_in_b, w_out=w_out_b, layer=layer,
                                    n_pre=n_pre, n_post=n_post, tm=FFN_TM, tf=FFN_TF)
            nxt = ffn_order.index((layer, slot)) + 1
            if nxt < len(ffn_order):
                xp, nw_in_b, nw_out_b = ffn(xp, next_w=(w_ffn_in, w_ffn_out) + ffn_order[nxt])
                ffn_w[ffn_order[nxt]] = (nw_in_b, nw_out_b)
            else:
                xp = ffn(xp)
            return xp, ffn(xs)

        xp, xs = ffn_pair(xp, xs, 0, 0, 1)
        if layer % 2 == 0:
            mix = functools.partial(_conv_mixer, norm_g=g4, w_in=w_conv_in_b, conv_w=w_conv,
                                    w_out=w_conv_out_b, layer=layer, j=j, tm=tm)
            xp, st = mix(xp, jnp.zeros((1, 2, d), F32), seg=tm, carried=True, tn=d)
            conv_p.append(st)
            xs, st = mix(xs, state_conv[j], seg=dec_seq, carried=False, tn=CONV_TN)
            conv_s.append(st)
        else:
            proj = functools.partial(_qkv_project, norm_g=g4, w_qkv=w_qkv_b, layer=layer, j=j,
                                     tm=tm, n_heads=n_heads, n_kv=n_kv)
            att = functools.partial(_attention, sinks=attn_sinks[j], norm_g=g4, w_o=w_attn_out_b,
                                    layer=layer, j=j, n_kv=n_kv)
            q, k, v = proj(xp, cos=cos_p, sin=sin_p)
            xp = att(xp, q, k, k, v, v, tq=tm, sq=WINDOW, streaming=True)
            k_p.append(k[seq - WINDOW:].reshape(1, WINDOW, n_kv, HEAD_DIM))
            v_p.append(v[seq - WINDOW:].reshape(1, WINDOW, n_kv, HEAD_DIM))
            q, k, v = proj(xs, cos=cos_s, sin=sin_s)
            ck = cache_k[j].reshape(dec_batch, WINDOW, kv_w)
            cv = cache_v[j].reshape(dec_batch, WINDOW, kv_w)
            xs = att(xs, q, ck, k, cv, v, tq=tm, sq=dec_seq, streaming=False)
            k_new = jnp.concatenate([ck, k.reshape(dec_batch, dec_seq, kv_w)], axis=1)[:, -WINDOW:]
            v_new = jnp.concatenate([cv, v.reshape(dec_batch, dec_seq, kv_w)], axis=1)[:, -WINDOW:]
            k_s.append(k_new.reshape(dec_batch, WINDOW, n_kv, HEAD_DIM))
            v_s.append(v_new.reshape(dec_batch, WINDOW, n_kv, HEAD_DIM))
        xp, xs = ffn_pair(xp, xs, 1, 4, 5)

    return (xp.reshape(batch, seq, d), xs.reshape(dec_batch, dec_seq, d),
            jnp.stack(conv_p), jnp.stack(conv_s),
            jnp.stack(k_p), jnp.stack(v_p), jnp.stack(k_s), jnp.stack(v_s))
```

```python
import functools

import jax
import jax.numpy as jnp
import numpy as np
from jax import lax
from jax.experimental import pallas as pl
from jax.experimental.pallas import tpu as pltpu

F32 = jnp.float32
BF16 = jnp.bfloat16

CHUNK = 64
LOOKBACK_CHUNKS = 2
WINDOW = LOOKBACK_CHUNKS * CHUNK
HEAD_DIM = 64
PAST_LEN = 2048
ROPE_THETA = 10000.0
NORM_EPS = 1e-6

LANES = 128
HEADS_PER_COL = LANES // HEAD_DIM
VMEM_LIMIT = 62 * 1024 * 1024
ROW_TILE = 512
FFN_TM = 1024
FFN_ROWS = 512
FFN_TF = 512
FFN_TF_F32 = 256
CONV_TN = 512
CONV_COLS = 256
ATTN_OUT_COLS = 256
ATTN_MIN_LAGGED_TILES = 4


def _rms(x, g):
    ms = jnp.mean(x * x, axis=-1, keepdims=True)
    return x * lax.rsqrt(ms + NORM_EPS) * g


def _params(*sem):
    return pltpu.CompilerParams(dimension_semantics=sem, vmem_limit_bytes=VMEM_LIMIT)


def _ffn_kernel(*refs, cast_next, cast_own):
    x_ref, gpre_ref, gpost_ref, wg_ref, wu_ref, wo_ref = refs[:6]
    n_side = 2 if cast_next else 0
    side_in = refs[6:6 + n_side]
    o_ref = refs[6 + n_side]
    side_out = refs[7 + n_side:7 + 2 * n_side]
    own_out = refs[7 + 2 * n_side:-1]
    h_ref = refs[-1]
    j = pl.program_id(1)
    last = pl.num_programs(1) - 1

    def step(first, final):
        for src, dst in zip(side_in, side_out):
            dst[...] = src[...].astype(BF16)
        wg, wu, wo = wg_ref[...], wu_ref[...], wo_ref[...]
        if cast_own:
            wg, wu, wo = wg.astype(BF16), wu.astype(BF16), wo.astype(BF16)
            for w, dst in zip((wg, wu, wo), own_out):
                dst[...] = w
        for r in range(0, x_ref.shape[0], FFN_ROWS):
            rows = slice(r, r + FFN_ROWS)
            if first:
                h = _rms(x_ref[rows], gpre_ref[...]).astype(BF16)
                h_ref[rows] = h
            else:
                h = h_ref[rows]
            gate = jnp.dot(h, wg, preferred_element_type=F32)
            up = jnp.dot(h, wu, preferred_element_type=F32)
            act = (gate * jax.nn.sigmoid(gate) * up).astype(BF16)
            part = jnp.dot(act, wo, preferred_element_type=F32)
            if first:
                o_ref[rows] = part
            elif final:
                o_ref[rows] = x_ref[rows] + 0.5 * _rms(o_ref[rows] + part, gpost_ref[...])
            else:
                o_ref[rows] += part

    pl.when(j == 0)(functools.partial(step, True, False))
    pl.when((j > 0) & (j < last))(functools.partial(step, False, False))
    pl.when(j == last)(functools.partial(step, False, True))


def _ffn(x, norm_g, weights, layer, n_pre, n_post, *, tm, tf, next_w=None):
    m, d = x.shape
    (wg, wg_lead, wg_off), (wu, wu_lead, wu_off), (wo, wo_lead, _) = weights
    f = wo.shape[-2]
    nf = f // tf
    n_tiles = m // tm
    cast_own = wg.dtype == F32
    assert not cast_own or (n_tiles == 1 and next_w is None)
    g_spec = lambda n: pl.BlockSpec((None, None, 1, d), lambda i, j: (layer, n, 0, 0))
    col_spec = lambda lead, off: pl.BlockSpec((None,) * len(lead) + (d, tf),
                                              lambda i, j: lead + (0, j + off // tf))
    row_spec = lambda lead: pl.BlockSpec((None,) * len(lead) + (tf, d), lambda i, j: lead + (j, 0))
    in_specs = [
        pl.BlockSpec((tm, d), lambda i, j: (i, 0)),
        g_spec(n_pre),
        g_spec(n_post),
        col_spec(wg_lead, wg_off),
        col_spec(wu_lead, wu_off),
        row_spec(wo_lead),
    ]
    operands = [x, norm_g, norm_g, wg, wu, wo]
    out_specs = [pl.BlockSpec((tm, d), lambda i, j: (i, 0))]
    out_shape = [jax.ShapeDtypeStruct((m, d), F32)]
    if next_w is not None:
        nw_in, nw_out, nl, ns = next_w
        bi = (d // n_tiles, 2 * f // nf)
        wo_cols = d // 2
        bo = (f * d // (n_tiles * nf * wo_cols), wo_cols)
        assert bi[0] * n_tiles == d and bi[1] * nf == 2 * f and bo[0] * n_tiles * nf * wo_cols == f * d
        step_of = lambda i, j: i * nf + j
        in_specs += [
            pl.BlockSpec((None, None) + bi, lambda i, j: (nl, ns, i, j)),
            pl.BlockSpec((None, None) + bo, lambda i, j: (nl, ns, step_of(i, j) // 2, step_of(i, j) % 2)),
        ]
        operands += [nw_in, nw_out]
        out_specs += [
            pl.BlockSpec(bi, lambda i, j: (i, j)),
            pl.BlockSpec(bo, lambda i, j: (step_of(i, j) // 2, step_of(i, j) % 2)),
        ]
        out_shape += [jax.ShapeDtypeStruct((d, 2 * f), BF16), jax.ShapeDtypeStruct((f, d), BF16)]
    if cast_own:
        out_specs += [col_spec((), 0), col_spec((), 0), row_spec(())]
        out_shape += [jax.ShapeDtypeStruct((d, f), BF16), jax.ShapeDtypeStruct((d, f), BF16),
                      jax.ShapeDtypeStruct((f, d), BF16)]
    outs = pl.pallas_call(
        functools.partial(_ffn_kernel, cast_next=next_w is not None, cast_own=cast_own),
        grid=(n_tiles, nf),
        in_specs=in_specs,
        out_specs=out_specs,
        out_shape=out_shape,
        scratch_shapes=[pltpu.VMEM((tm, d), BF16)],
        compiler_params=_params("parallel", "arbitrary"),
        name="ffn",
    )(*operands)
    if next_w is not None:
        return outs[0], ((outs[1], (), 0), (outs[1], (), f), (outs[2], (), 0))
    if cast_own:
        return outs[0], ((outs[1], (), 0), (outs[2], (), 0), (outs[3], (), 0))
    return outs[0]


def _conv_kernel(x_ref, gpre_ref, gpost_ref, wb_ref, wc_ref, wh_ref, cw_ref, wo_ref, st_ref,
                 o_ref, ns_ref, h_ref, carry_ref, *, seg, carried):
    i = pl.program_id(0)
    n = pl.program_id(1)
    last = pl.num_programs(1) - 1
    tm = x_ref.shape[0]
    tn = wb_ref.shape[1]
    tc = CONV_COLS
    cols = pl.ds(pl.multiple_of(n * tn, tn), tn)

    if carried:
        @pl.when(i == 0)
        def _():
            carry_ref[:, cols] = st_ref[0, :, cols]

    def gated_conv(c0, b_gate, c_gate, hh):
        gcols = pl.ds(pl.multiple_of(n * tn + c0, tc), tc)
        cw = cw_ref[:, c0:c0 + tc]
        w0, w1, w2 = cw[0:1], cw[1:2], cw[2:3]
        row = lax.broadcasted_iota(jnp.int32, (seg, tc), 0)
        u = c_gate * hh
        convs = []
        for s in range(tm // seg):
            us = u[s * seg:(s + 1) * seg]
            prev = carry_ref[:, gcols] if carried else st_ref[s, :, c0:c0 + tc]
            p1 = jnp.where(row == 0, prev[1:2], pltpu.roll(us, 1, 0))
            p2 = jnp.where(row == 0, prev[0:1], jnp.where(row == 1, prev[1:2], pltpu.roll(us, 2, 0)))
            convs.append(p2 * w0 + p1 * w1 + us * w2)
            if carried:
                carry_ref[:, gcols] = us[seg - 2:seg]
                ns_ref[0, :, gcols] = us[seg - 2:seg]
            else:
                ns_ref[s, :, c0:c0 + tc] = us[seg - 2:seg]
        conv = convs[0] if len(convs) == 1 else jnp.concatenate(convs, axis=0)
        return (b_gate * conv).astype(BF16)

    def out_proj(assign, final, c0, z):
        part = jnp.dot(z, wo_ref[c0:c0 + tc, :], preferred_element_type=F32)
        if assign:
            o_ref[...] = part
        elif final:
            o_ref[...] = x_ref[...] + _rms(o_ref[...] + part, gpost_ref[...])
        else:
            o_ref[...] += part

    def step(first, final):
        if first:
            h = _rms(x_ref[...], gpre_ref[...]).astype(BF16)
            h_ref[...] = h
        else:
            h = h_ref[...]
        pending = None
        for c0 in range(0, tn, tc):
            b_gate = jnp.dot(h, wb_ref[:, c0:c0 + tc], preferred_element_type=F32)
            c_gate = jnp.dot(h, wc_ref[:, c0:c0 + tc], preferred_element_type=F32)
            hh = jnp.dot(h, wh_ref[:, c0:c0 + tc], preferred_element_type=F32)
            if pending is not None:
                out_proj(first and pending[0] == 0, False, *pending)
            pending = (c0, gated_conv(c0, b_gate, c_gate, hh))
        out_proj(first and pending[0] == 0, final, *pending)

    if tn == o_ref.shape[1]:
        step(True, True)
    else:
        pl.when(n == 0)(functools.partial(step, True, False))
        pl.when((n > 0) & (n < last))(functools.partial(step, False, False))
        pl.when(n == last)(functools.partial(step, False, True))


def _conv_mixer(x, state, norm_g, w_in, conv_w, w_out, layer, j, *, tm, tn, seg, carried):
    m, d = x.shape
    nn = d // tn
    nseg = tm // seg
    g_spec = lambda n: pl.BlockSpec((None, None, 1, d), lambda i, c: (layer, n, 0, 0))
    once = dict(pipeline_mode=pl.Buffered(1)) if nn == 1 else {}
    w_spec = lambda sec: pl.BlockSpec((None, d, tn), lambda i, c: (j, 0, c + sec * nn), **once)
    if carried:
        st_spec = pl.BlockSpec((1, 2, d), lambda i, c: (0, 0, 0))
    else:
        st_spec = pl.BlockSpec((nseg, 2, tn), lambda i, c: (i, 0, c))
    return pl.pallas_call(
        functools.partial(_conv_kernel, seg=seg, carried=carried),
        grid=(m // tm, nn),
        in_specs=[
            pl.BlockSpec((tm, d), lambda i, c: (i, 0)),
            g_spec(2),
            g_spec(3),
            w_spec(0),
            w_spec(1),
            w_spec(2),
            pl.BlockSpec((None, 3, tn), lambda i, c: (j, 0, c)),
            pl.BlockSpec((None, tn, d), lambda i, c: (j, c, 0), **once),
            st_spec,
        ],
        out_specs=[pl.BlockSpec((tm, d), lambda i, c: (i, 0)), st_spec],
        out_shape=[
            jax.ShapeDtypeStruct((m, d), F32),
            jax.ShapeDtypeStruct(state.shape, F32),
        ],
        scratch_shapes=[pltpu.VMEM((tm, d), BF16), pltpu.VMEM((2, d), F32)],
        compiler_params=_params("arbitrary", "arbitrary"),
        name="conv_mixer",
    )(x, norm_g, norm_g, w_in, w_in, w_in, conv_w, w_out, state)


def _rope_col(xc, cos, sin_signed):
    lane = lax.broadcasted_iota(jnp.int32, xc.shape, 1)
    first_half = (lane % HEAD_DIM) < HEAD_DIM // 2
    swapped = jnp.where(first_half,
                        pltpu.roll(xc, LANES - HEAD_DIM // 2, 1),
                        pltpu.roll(xc, HEAD_DIM // 2, 1))
    return xc * cos + swapped * sin_signed


def _qkv_kernel(x_ref, g_ref, w_ref, cos_ref, sin_ref, q_ref, k_ref, v_ref, *, chunk):
    n_q = q_ref.shape[0] * LANES
    kv_w = k_ref.shape[1]
    h = _rms(x_ref[...], g_ref[...]).astype(BF16)
    cos = cos_ref[...]
    sin = sin_ref[...]
    for c0 in range(0, n_q, chunk):
        y = jnp.dot(h, w_ref[:, c0:c0 + chunk], preferred_element_type=F32)
        for c in range(chunk // LANES):
            qc = _rope_col(y[:, c * LANES:(c + 1) * LANES], cos, sin) * (HEAD_DIM ** -0.5)
            q_ref[c0 // LANES + c] = qc.astype(BF16)
    y = jnp.dot(h, w_ref[:, n_q:], preferred_element_type=F32)
    for c in range(kv_w // LANES):
        k_ref[:, c * LANES:(c + 1) * LANES] = _rope_col(y[:, c * LANES:(c + 1) * LANES], cos, sin)
    v_ref[...] = y[:, kv_w:]


def _qkv_project(x, norm_g, w_qkv, cos, sin, layer, j, *, tm, n_heads, n_kv):
    m, d = x.shape
    kv_w = n_kv * HEAD_DIM
    n_cols = n_heads // HEADS_PER_COL
    return pl.pallas_call(
        functools.partial(_qkv_kernel, chunk=2 * kv_w),
        grid=(m // tm,),
        in_specs=[
            pl.BlockSpec((tm, d), lambda i: (i, 0)),
            pl.BlockSpec((None, None, 1, d), lambda i: (layer, 2, 0, 0)),
            pl.BlockSpec((None, d, w_qkv.shape[2]), lambda i: (j, 0, 0)),
            pl.BlockSpec((tm, LANES), lambda i: (i, 0)),
            pl.BlockSpec((tm, LANES), lambda i: (i, 0)),
        ],
        out_specs=[
            pl.BlockSpec((n_cols, tm, LANES), lambda i: (0, i, 0)),
            pl.BlockSpec((tm, kv_w), lambda i: (i, 0)),
            pl.BlockSpec((tm, kv_w), lambda i: (i, 0)),
        ],
        out_shape=[
            jax.ShapeDtypeStruct((n_cols, m, LANES), BF16),
            jax.ShapeDtypeStruct((m, kv_w), F32),
            jax.ShapeDtypeStruct((m, kv_w), F32),
        ],
        compiler_params=_params("parallel"),
        name="qkv_rope",
    )(x, norm_g, w_qkv, cos, sin)


def _block_diag(col, rolled, first):
    lane = lax.broadcasted_iota(jnp.int32, col.shape, 1)
    lo = lane < HEAD_DIM
    zero = jnp.zeros_like(col)
    if first:
        return jnp.concatenate([jnp.where(lo, col, zero), jnp.where(lo, zero, rolled)], axis=0)
    return jnp.concatenate([jnp.where(lo, rolled, zero), jnp.where(lo, zero, col)], axis=0)


def _attn_kernel(sink_ref, q_ref, kp_ref, kc_ref, vp_ref, vc_ref, x_ref, g_ref, wo_ref, o_ref,
                 att_ref, *, sq, n_kv, streaming, lag, q_pos0, k_pos0):
    i = pl.program_id(0)
    t = jnp.minimum(i, pl.num_programs(0) - 1 - lag)
    cur = lax.rem(i, 2) if lag else 0
    tq, d = x_ref.shape
    nsub = tq // sq
    nk_real = WINDOW + sq
    nk = -(-nk_real // LANES) * LANES
    assert nk == nk_real or not streaming
    n_cols = q_ref.shape[0]
    cols_per_kv = n_cols // n_kv
    n_blocks = nsub * n_kv
    blocks_per_chunk = n_blocks // (d // ATTN_OUT_COLS)

    if lag:
        @pl.when(i == 0)
        def _():
            att_ref[1] = jnp.zeros((tq, d), BF16)

    if streaming:
        k_all = jnp.concatenate([kp_ref[...], kc_ref[...]], axis=0)
        v_all = jnp.concatenate([vp_ref[...], vc_ref[...]], axis=0)

    row = lax.broadcasted_iota(jnp.int32, (sq, nk), 0)
    colk = lax.broadcasted_iota(jnp.int32, (sq, nk), 1)

    @functools.cache
    def sub_tile(s):
        if streaming:
            q0 = t * tq + s * sq
            k0 = q0 - WINDOW
            ks = k_all[s * sq:s * sq + nk]
            vs = v_all[s * sq:s * sq + nk]
        else:
            q0 = q_pos0
            k0 = k_pos0
            pad = [jnp.zeros((nk - nk_real, kp_ref.shape[2]), F32)] if nk > nk_real else []
            ks = jnp.concatenate([kp_ref[s], kc_ref[s * sq:(s + 1) * sq]] + pad, axis=0)
            vs = jnp.concatenate([vp_ref[s], vc_ref[s * sq:(s + 1) * sq]] + pad, axis=0)
        kpos = k0 + colk
        q_chunk = lax.shift_right_arithmetic(q0 + row, 6)
        k_chunk = lax.shift_right_arithmetic(kpos, 6)
        valid = (k_chunk >= q_chunk - LOOKBACK_CHUNKS) & (k_chunk <= q_chunk)
        if streaming:
            valid = valid & (kpos >= 0)
        if nk > nk_real:
            valid = valid & (colk < nk_real)
        return ks, vs, valid

    def scores(s, kv):
        ks, _, _ = sub_tile(s)
        c, first = divmod(kv, HEADS_PER_COL)
        kcol = ks[:, c * LANES:(c + 1) * LANES]
        bdk = _block_diag(kcol, pltpu.roll(kcol, HEAD_DIM, 1), first == 0).astype(BF16)
        qs = q_ref[kv * cols_per_kv:(kv + 1) * cols_per_kv, s * sq:(s + 1) * sq, :]
        qs = qs.reshape(cols_per_kv * sq, LANES)
        return lax.dot_general(qs, bdk, (((1,), (1,)), ((), ())), preferred_element_type=F32)

    def attend(s, kv, sc):
        _, vs, valid = sub_tile(s)
        c, first = divmod(kv, HEADS_PER_COL)
        vcol = vs[:, c * LANES:(c + 1) * LANES]
        bdv = _block_diag(vcol, pltpu.roll(vcol, HEAD_DIM, 1), first == 0).astype(BF16)
        p_rows = []
        for p in range(cols_per_kv):
            halves = []
            for e in range(HEADS_PER_COL):
                sink = sink_ref[(kv * cols_per_kv + p) * HEADS_PER_COL + e]
                blk = sc[p * sq:(p + 1) * sq, e * nk:(e + 1) * nk]
                blk = jnp.where(valid, blk, -jnp.inf)
                mx = jnp.maximum(jnp.max(blk, axis=-1, keepdims=True), sink)
                ex = jnp.exp(blk - mx)
                den = jnp.sum(ex, axis=-1, keepdims=True) + jnp.exp(sink - mx)
                halves.append((ex * (1.0 / den)).astype(BF16))
            p_rows.append(jnp.concatenate(halves, axis=1))
        pm = jnp.concatenate(p_rows, axis=0)
        ov = jnp.dot(pm, bdv, preferred_element_type=F32)
        for p in range(cols_per_kv):
            cc = kv * cols_per_kv + p
            att_ref[cur, s * sq:(s + 1) * sq, cc * LANES:(cc + 1) * LANES] = (
                ov[p * sq:(p + 1) * sq].astype(BF16))

    blocks = [(s, kv) for s in range(nsub) for kv in range(n_kv)]
    sc_next = scores(*blocks[0])
    for b, (s, kv) in enumerate(blocks):
        sc = sc_next
        if b + 1 < n_blocks:
            sc_next = scores(*blocks[b + 1])
        if lag and (b + 1) % blocks_per_chunk == 0:
            oc = ((b + 1) // blocks_per_chunk - 1) * ATTN_OUT_COLS
            o_ref[:, oc:oc + ATTN_OUT_COLS] = jnp.dot(
                att_ref[1 - cur], wo_ref[:, oc:oc + ATTN_OUT_COLS], preferred_element_type=F32)
        attend(s, kv, sc)

    if not lag:
        o_ref[...] = jnp.dot(att_ref[0], wo_ref[...], preferred_element_type=F32)
    o_ref[...] = x_ref[...] + _rms(o_ref[...], g_ref[...])


def _attention(x, q, k_prev, k, v_prev, v, sinks, norm_g, w_o, layer, j, *, tq, sq, n_kv,
               streaming):
    m, d = x.shape
    n_cols = q.shape[0]
    kv_w = k.shape[1]
    nsub = tq // sq
    n_tiles = m // tq
    lag = int(n_tiles >= ATTN_MIN_LAGGED_TILES)
    att_tile = lambda i: jnp.minimum(i, n_tiles - 1)
    out_tile = lambda i: jnp.maximum(i - lag, 0)
    if streaming:
        per = tq // WINDOW
        prev_spec = pl.BlockSpec((WINDOW, kv_w), lambda i: (jnp.maximum(att_tile(i) * per - 1, 0), 0))
    else:
        prev_spec = pl.BlockSpec((nsub, WINDOW, kv_w), lambda i: (att_tile(i), 0, 0))
    cur_spec = pl.BlockSpec((tq, kv_w), lambda i: (att_tile(i), 0))
    kern = functools.partial(_attn_kernel, sq=sq, n_kv=n_kv, streaming=streaming, lag=lag,
                             q_pos0=PAST_LEN, k_pos0=PAST_LEN - WINDOW)
    return pl.pallas_call(
        kern,
        grid=(n_tiles + lag,),
        in_specs=[
            pl.BlockSpec(memory_space=pltpu.SMEM),
            pl.BlockSpec((n_cols, tq, LANES), lambda i: (0, att_tile(i), 0)),
            prev_spec, cur_spec, prev_spec, cur_spec,
            pl.BlockSpec((tq, d), lambda i: (out_tile(i), 0)),
            pl.BlockSpec((None, None, 1, d), lambda i: (layer, 3, 0, 0)),
            pl.BlockSpec((None, d, d), lambda i: (j, 0, 0)),
        ],
        out_specs=pl.BlockSpec((tq, d), lambda i: (out_tile(i), 0)),
        out_shape=jax.ShapeDtypeStruct((m, d), F32),
        scratch_shapes=[pltpu.VMEM((1 + lag, tq, d), BF16)],
        compiler_params=_params("arbitrary"),
        name="swa_attention",
    )(sinks, q, k_prev, k, v_prev, v, x, norm_g, w_o)


def _rope_tables(pos):
    half = HEAD_DIM // 2
    inv_freq = ROPE_THETA ** (-jnp.arange(half, dtype=F32) / half)
    ang = pos.astype(F32)[:, None] * inv_freq[None, :]
    cos = jnp.cos(ang)
    sin = jnp.sin(ang)
    cos_t = jnp.tile(cos, (1, LANES // half))
    sin_t = jnp.tile(jnp.concatenate([-sin, sin], axis=1), (1, HEADS_PER_COL))
    return cos_t, sin_t


def kernel(x_prompt, x_sample, state_conv, cache_k, cache_v, norm_g, w_ffn_in, w_ffn_out,
           w_conv_in, w_conv, w_conv_out, w_qkv, w_attn_out, attn_sinks):
    batch, seq, d = x_prompt.shape
    dec_batch, dec_seq, _ = x_sample.shape
    depth = norm_g.shape[0]
    n_heads = attn_sinks.shape[1]
    n_kv = cache_k.shape[3]
    kv_w = n_kv * HEAD_DIM
    assert batch == 1 and dec_seq == CHUNK and cache_k.shape[2] == WINDOW

    g4 = norm_g.reshape(depth, norm_g.shape[1], 1, d)
    ffn_order = [(layer, slot) for layer in range(depth) for slot in range(2)]
    ffn_w = {}
    w_conv_in_b = w_conv_in.astype(BF16)
    w_conv_out_b = w_conv_out.astype(BF16)
    w_qkv_b = w_qkv.astype(BF16)
    w_attn_out_b = w_attn_out.astype(BF16)

    xp = x_prompt.reshape(seq, d)
    xs = x_sample.reshape(dec_batch * dec_seq, d)
    cos_p, sin_p = _rope_tables(jnp.arange(seq))
    cos_s, sin_s = _rope_tables(jnp.tile(PAST_LEN + jnp.arange(dec_seq), dec_batch))

    tm = ROW_TILE
    conv_p, conv_s, k_p, v_p, k_s, v_s = [], [], [], [], [], []
    for layer in range(depth):
        j = layer // 2
        def ffn_pair(xp, xs, slot, n_pre, n_post):
            ffn = functools.partial(_ffn, norm_g=g4, layer=layer, n_pre=n_pre, n_post=n_post, tm=FFN_TM)
            here = ffn_order.index((layer, slot))
            if here == 0:
                f32_w = ((w_ffn_in, (0, 0), 0), (w_ffn_in, (0, 0), w_ffn_out.shape[2]),
                         (w_ffn_out, (0, 0), 0))
                xs, ffn_w[(layer, slot)] = ffn(xs, weights=f32_w, tf=FFN_TF_F32)
            else:
                xs = ffn(xs, weights=ffn_w[(layer, slot)], tf=FFN_TF)
            if here + 1 < len(ffn_order):
                xp, ffn_w[ffn_order[here + 1]] = ffn(
                    xp, weights=ffn_w[(layer, slot)], tf=FFN_TF,
                    next_w=(w_ffn_in, w_ffn_out) + ffn_order[here + 1])
            else:
                xp = ffn(xp, weights=ffn_w[(layer, slot)], tf=FFN_TF)
            return xp, xs

        xp, xs = ffn_pair(xp, xs, 0, 0, 1)
        if layer % 2 == 0:
            mix = functools.partial(_conv_mixer, norm_g=g4, w_in=w_conv_in_b, conv_w=w_conv,
                                    w_out=w_conv_out_b, layer=layer, j=j, tm=tm)
            xp, st = mix(xp, jnp.zeros((1, 2, d), F32), seg=tm, carried=True, tn=d)
            conv_p.append(st)
            xs, st = mix(xs, state_conv[j], seg=dec_seq, carried=False, tn=CONV_TN)
            conv_s.append(st)
        else:
            proj = functools.partial(_qkv_project, norm_g=g4, w_qkv=w_qkv_b, layer=layer, j=j,
                                     tm=tm, n_heads=n_heads, n_kv=n_kv)
            att = functools.partial(_attention, sinks=attn_sinks[j], norm_g=g4, w_o=w_attn_out_b,
                                    layer=layer, j=j, n_kv=n_kv)
            q, k, v = proj(xp, cos=cos_p, sin=sin_p)
            xp = att(xp, q, k, k, v, v, tq=tm, sq=WINDOW, streaming=True)
            k_p.append(k[seq - WINDOW:].reshape(1, WINDOW, n_kv, HEAD_DIM))
            v_p.append(v[seq - WINDOW:].reshape(1, WINDOW, n_kv, HEAD_DIM))
            q, k, v = proj(xs, cos=cos_s, sin=sin_s)
            ck = cache_k[j].reshape(dec_batch, WINDOW, kv_w)
            cv = cache_v[j].reshape(dec_batch, WINDOW, kv_w)
            xs = att(xs, q, ck, k, cv, v, tq=tm, sq=dec_seq, streaming=False)
            k_new = jnp.concatenate([ck, k.reshape(dec_batch, dec_seq, kv_w)], axis=1)[:, -WINDOW:]
            v_new = jnp.concatenate([cv, v.reshape(dec_batch, dec_seq, kv_w)], axis=1)[:, -WINDOW:]
            k_s.append(k_new.reshape(dec_batch, WINDOW, n_kv, HEAD_DIM))
            v_s.append(v_new.reshape(dec_batch, WINDOW, n_kv, HEAD_DIM))
        xp, xs = ffn_pair(xp, xs, 1, 4, 5)

    return (xp.reshape(batch, seq, d), xs.reshape(dec_batch, dec_seq, d),
            jnp.stack(conv_p), jnp.stack(conv_s),
            jnp.stack(k_p), jnp.stack(v_p), jnp.stack(k_s), jnp.stack(v_s))
```

```python
import functools

import jax
import jax.numpy as jnp
from jax import lax
from jax.experimental import pallas as pl
from jax.experimental.pallas import tpu as pltpu

F32 = jnp.float32
BF16 = jnp.bfloat16

CHUNK = 64
CHUNK_SHIFT = CHUNK.bit_length() - 1
assert 1 << CHUNK_SHIFT == CHUNK
LOOKBACK_CHUNKS = 2
WINDOW = LOOKBACK_CHUNKS * CHUNK
HEAD_DIM = 64
PAST_LEN = 2048
ROPE_THETA = 10000.0
NORM_EPS = 1e-6

LANES = 128
HEADS_PER_COL = LANES // HEAD_DIM
VMEM_LIMIT = 62 * 1024 * 1024
ROW_TILE = 512
FFN_TM = 1024
FFN_ROWS = 512
FFN_TF = 512
FFN_TF_F32 = 256
CONV_TN = 512
CONV_COLS = 256
ATTN_OUT_COLS = 256
ATTN_MIN_LAGGED_TILES = 4


def _rms(x, g):
    ms = jnp.mean(x * x, axis=-1, keepdims=True)
    return x * lax.rsqrt(ms + NORM_EPS) * g


def _params(*sem):
    return pltpu.CompilerParams(dimension_semantics=sem, vmem_limit_bytes=VMEM_LIMIT)


def _ffn_kernel(*refs, cast_next, cast_own):
    x_ref, gpre_ref, gpost_ref, wg_ref, wu_ref, wo_ref = refs[:6]
    n_side = 2 if cast_next else 0
    side_in = refs[6:6 + n_side]
    o_ref = refs[6 + n_side]
    side_out = refs[7 + n_side:7 + 2 * n_side]
    own_out = refs[7 + 2 * n_side:-1]
    h_ref = refs[-1]
    j = pl.program_id(1)
    last = pl.num_programs(1) - 1

    def step(first, final):
        for src, dst in zip(side_in, side_out):
            dst[...] = src[...].astype(BF16)
        wg, wu, wo = wg_ref[...], wu_ref[...], wo_ref[...]
        if cast_own:
            wg, wu, wo = wg.astype(BF16), wu.astype(BF16), wo.astype(BF16)
            for w, dst in zip((wg, wu, wo), own_out):
                dst[...] = w
        for r in range(0, x_ref.shape[0], FFN_ROWS):
            rows = slice(r, r + FFN_ROWS)
            if first:
                h = _rms(x_ref[rows], gpre_ref[...]).astype(BF16)
                h_ref[rows] = h
            else:
                h = h_ref[rows]
            gate = jnp.dot(h, wg, preferred_element_type=F32)
            up = jnp.dot(h, wu, preferred_element_type=F32)
            act = (gate * jax.nn.sigmoid(gate) * up).astype(BF16)
            part = jnp.dot(act, wo, preferred_element_type=F32)
            if first:
                o_ref[rows] = part
            elif final:
                o_ref[rows] = x_ref[rows] + 0.5 * _rms(o_ref[rows] + part, gpost_ref[...])
            else:
                o_ref[rows] += part

    pl.when(j == 0)(functools.partial(step, True, False))
    pl.when((j > 0) & (j < last))(functools.partial(step, False, False))
    pl.when(j == last)(functools.partial(step, False, True))


def _ffn(x, norm_g, weights, layer, n_pre, n_post, *, tm, tf, next_w=None):
    m, d = x.shape
    (wg, wg_lead, wg_off), (wu, wu_lead, wu_off), (wo, wo_lead, _) = weights
    f = wo.shape[-2]
    nf = f // tf
    n_tiles = m // tm
    cast_own = wg.dtype == F32
    assert not cast_own or (n_tiles == 1 and next_w is None)
    g_spec = lambda n: pl.BlockSpec((None, None, 1, d), lambda i, j: (layer, n, 0, 0))
    col_spec = lambda lead, off: pl.BlockSpec((None,) * len(lead) + (d, tf),
                                              lambda i, j: lead + (0, j + off // tf))
    row_spec = lambda lead: pl.BlockSpec((None,) * len(lead) + (tf, d), lambda i, j: lead + (j, 0))
    in_specs = [
        pl.BlockSpec((tm, d), lambda i, j: (i, 0)),
        g_spec(n_pre),
        g_spec(n_post),
        col_spec(wg_lead, wg_off),
        col_spec(wu_lead, wu_off),
        row_spec(wo_lead),
    ]
    operands = [x, norm_g, norm_g, wg, wu, wo]
    out_specs = [pl.BlockSpec((tm, d), lambda i, j: (i, 0))]
    out_shape = [jax.ShapeDtypeStruct((m, d), F32)]
    if next_w is not None:
        nw_in, nw_out, nl, ns = next_w
        bi = (d // n_tiles, 2 * f // nf)
        wo_cols = d // 2
        bo = (f * d // (n_tiles * nf * wo_cols), wo_cols)
        assert bi[0] * n_tiles == d and bi[1] * nf == 2 * f and bo[0] * n_tiles * nf * wo_cols == f * d
        step_of = lambda i, j: i * nf + j
        in_specs += [
            pl.BlockSpec((None, None) + bi, lambda i, j: (nl, ns, i, j)),
            pl.BlockSpec((None, None) + bo, lambda i, j: (nl, ns, step_of(i, j) // 2, step_of(i, j) % 2)),
        ]
        operands += [nw_in, nw_out]
        out_specs += [
            pl.BlockSpec(bi, lambda i, j: (i, j)),
            pl.BlockSpec(bo, lambda i, j: (step_of(i, j) // 2, step_of(i, j) % 2)),
        ]
        out_shape += [jax.ShapeDtypeStruct((d, 2 * f), BF16), jax.ShapeDtypeStruct((f, d), BF16)]
    if cast_own:
        out_specs += [col_spec((), 0), col_spec((), 0), row_spec(())]
        out_shape += [jax.ShapeDtypeStruct((d, f), BF16), jax.ShapeDtypeStruct((d, f), BF16),
                      jax.ShapeDtypeStruct((f, d), BF16)]
    outs = pl.pallas_call(
        functools.partial(_ffn_kernel, cast_next=next_w is not None, cast_own=cast_own),
        grid=(n_tiles, nf),
        in_specs=in_specs,
        out_specs=out_specs,
        out_shape=out_shape,
        scratch_shapes=[pltpu.VMEM((tm, d), BF16)],
        compiler_params=_params("parallel", "arbitrary"),
        name="ffn",
    )(*operands)
    if next_w is not None:
        return outs[0], ((outs[1], (), 0), (outs[1], (), f), (outs[2], (), 0))
    if cast_own:
        return outs[0], ((outs[1], (), 0), (outs[2], (), 0), (outs[3], (), 0))
    return outs[0]


def _conv_kernel(x_ref, gpre_ref, gpost_ref, wb_ref, wc_ref, wh_ref, cw_ref, wo_ref, st_ref,
                 o_ref, ns_ref, h_ref, carry_ref, *, seg, carried):
    i = pl.program_id(0)
    n = pl.program_id(1)
    last = pl.num_programs(1) - 1
    tm = x_ref.shape[0]
    tn = wb_ref.shape[1]
    tc = CONV_COLS
    cols = pl.ds(pl.multiple_of(n * tn, tn), tn)

    if carried:
        @pl.when(i == 0)
        def _():
            carry_ref[:, cols] = st_ref[0, :, cols]

    def gated_conv(c0, b_gate, c_gate, hh):
        gcols = pl.ds(pl.multiple_of(n * tn + c0, tc), tc)
        cw = cw_ref[:, c0:c0 + tc]
        w0, w1, w2 = cw[0:1], cw[1:2], cw[2:3]
        row = lax.broadcasted_iota(jnp.int32, (seg, tc), 0)
        u = c_gate * hh
        convs = []
        for s in range(tm // seg):
            us = u[s * seg:(s + 1) * seg]
            prev = carry_ref[:, gcols] if carried else st_ref[s, :, c0:c0 + tc]
            p1 = jnp.where(row == 0, prev[1:2], pltpu.roll(us, 1, 0))
            p2 = jnp.where(row == 0, prev[0:1], jnp.where(row == 1, prev[1:2], pltpu.roll(us, 2, 0)))
            convs.append(p2 * w0 + p1 * w1 + us * w2)
            if carried:
                carry_ref[:, gcols] = us[seg - 2:seg]
                ns_ref[0, :, gcols] = us[seg - 2:seg]
            else:
                ns_ref[s, :, c0:c0 + tc] = us[seg - 2:seg]
        conv = convs[0] if len(convs) == 1 else jnp.concatenate(convs, axis=0)
        return (b_gate * conv).astype(BF16)

    def out_proj(assign, final, c0, z):
        part = jnp.dot(z, wo_ref[c0:c0 + tc, :], preferred_element_type=F32)
        if assign:
            o_ref[...] = part
        elif final:
            o_ref[...] = x_ref[...] + _rms(o_ref[...] + part, gpost_ref[...])
        else:
            o_ref[...] += part

    def step(first, final):
        if first:
            h = _rms(x_ref[...], gpre_ref[...]).astype(BF16)
            h_ref[...] = h
        else:
            h = h_ref[...]
        pending = None
        for c0 in range(0, tn, tc):
            b_gate = jnp.dot(h, wb_ref[:, c0:c0 + tc], preferred_element_type=F32)
            c_gate = jnp.dot(h, wc_ref[:, c0:c0 + tc], preferred_element_type=F32)
            hh = jnp.dot(h, wh_ref[:, c0:c0 + tc], preferred_element_type=F32)
            if pending is not None:
                out_proj(first and pending[0] == 0, False, *pending)
            pending = (c0, gated_conv(c0, b_gate, c_gate, hh))
        out_proj(first and pending[0] == 0, final, *pending)

    if tn == o_ref.shape[1]:
        step(True, True)
    else:
        pl.when(n == 0)(functools.partial(step, True, False))
        pl.when((n > 0) & (n < last))(functools.partial(step, False, False))
        pl.when(n == last)(functools.partial(step, False, True))


def _conv_mixer(x, state, norm_g, w_in, conv_w, w_out, layer, j, *, tm, tn, seg, carried):
    m, d = x.shape
    nn = d // tn
    nseg = tm // seg
    g_spec = lambda n: pl.BlockSpec((None, None, 1, d), lambda i, c: (layer, n, 0, 0))
    once = dict(pipeline_mode=pl.Buffered(1)) if nn == 1 else {}
    w_spec = lambda sec: pl.BlockSpec((None, d, tn), lambda i, c: (j, 0, c + sec * nn), **once)
    if carried:
        st_spec = pl.BlockSpec((1, 2, d), lambda i, c: (0, 0, 0))
    else:
        st_spec = pl.BlockSpec((nseg, 2, tn), lambda i, c: (i, 0, c))
    return pl.pallas_call(
        functools.partial(_conv_kernel, seg=seg, carried=carried),
        grid=(m // tm, nn),
        in_specs=[
            pl.BlockSpec((tm, d), lambda i, c: (i, 0)),
            g_spec(2),
            g_spec(3),
            w_spec(0),
            w_spec(1),
            w_spec(2),
            pl.BlockSpec((None, 3, tn), lambda i, c: (j, 0, c)),
            pl.BlockSpec((None, tn, d), lambda i, c: (j, c, 0), **once),
            st_spec,
        ],
        out_specs=[pl.BlockSpec((tm, d), lambda i, c: (i, 0)), st_spec],
        out_shape=[
            jax.ShapeDtypeStruct((m, d), F32),
            jax.ShapeDtypeStruct(state.shape, F32),
        ],
        scratch_shapes=[pltpu.VMEM((tm, d), BF16), pltpu.VMEM((2, d), F32)],
        compiler_params=_params("arbitrary", "arbitrary"),
        name="conv_mixer",
    )(x, norm_g, norm_g, w_in, w_in, w_in, conv_w, w_out, state)


def _rope_col(xc, cos, sin_signed):
    lane = lax.broadcasted_iota(jnp.int32, xc.shape, 1)
    first_half = (lane % HEAD_DIM) < HEAD_DIM // 2
    swapped = jnp.where(first_half,
                        pltpu.roll(xc, LANES - HEAD_DIM // 2, 1),
                        pltpu.roll(xc, HEAD_DIM // 2, 1))
    return xc * cos + swapped * sin_signed


def _qkv_kernel(x_ref, g_ref, w_ref, cos_ref, sin_ref, q_ref, k_ref, v_ref, *, chunk):
    n_q = q_ref.shape[0] * LANES
    kv_w = k_ref.shape[1]
    h = _rms(x_ref[...], g_ref[...]).astype(BF16)
    cos = cos_ref[...]
    sin = sin_ref[...]
    for c0 in range(0, n_q, chunk):
        y = jnp.dot(h, w_ref[:, c0:c0 + chunk], preferred_element_type=F32)
        for c in range(chunk // LANES):
            qc = _rope_col(y[:, c * LANES:(c + 1) * LANES], cos, sin) * (HEAD_DIM ** -0.5)
            q_ref[c0 // LANES + c] = qc.astype(BF16)
    y = jnp.dot(h, w_ref[:, n_q:], preferred_element_type=F32)
    for c in range(kv_w // LANES):
        k_ref[:, c * LANES:(c + 1) * LANES] = _rope_col(y[:, c * LANES:(c + 1) * LANES], cos, sin)
    v_ref[...] = y[:, kv_w:]


def _qkv_project(x, norm_g, w_qkv, cos, sin, layer, j, *, tm, n_heads, n_kv):
    m, d = x.shape
    kv_w = n_kv * HEAD_DIM
    n_cols = n_heads // HEADS_PER_COL
    return pl.pallas_call(
        functools.partial(_qkv_kernel, chunk=2 * kv_w),
        grid=(m // tm,),
        in_specs=[
            pl.BlockSpec((tm, d), lambda i: (i, 0)),
            pl.BlockSpec((None, None, 1, d), lambda i: (layer, 2, 0, 0)),
            pl.BlockSpec((None, d, w_qkv.shape[2]), lambda i: (j, 0, 0)),
            pl.BlockSpec((tm, LANES), lambda i: (i, 0)),
            pl.BlockSpec((tm, LANES), lambda i: (i, 0)),
        ],
        out_specs=[
            pl.BlockSpec((n_cols, tm, LANES), lambda i: (0, i, 0)),
            pl.BlockSpec((tm, kv_w), lambda i: (i, 0)),
            pl.BlockSpec((tm, kv_w), lambda i: (i, 0)),
        ],
        out_shape=[
            jax.ShapeDtypeStruct((n_cols, m, LANES), BF16),
            jax.ShapeDtypeStruct((m, kv_w), F32),
            jax.ShapeDtypeStruct((m, kv_w), F32),
        ],
        compiler_params=_params("parallel"),
        name="qkv_rope",
    )(x, norm_g, w_qkv, cos, sin)


def _block_diag(col, rolled, first):
    lane = lax.broadcasted_iota(jnp.int32, col.shape, 1)
    lo = lane < HEAD_DIM
    zero = jnp.zeros_like(col)
    if first:
        return jnp.concatenate([jnp.where(lo, col, zero), jnp.where(lo, zero, rolled)], axis=0)
    return jnp.concatenate([jnp.where(lo, rolled, zero), jnp.where(lo, zero, col)], axis=0)


def _attn_kernel(sink_ref, q_ref, kp_ref, kc_ref, vp_ref, vc_ref, x_ref, g_ref, wo_ref, o_ref,
                 att_ref, *, sq, n_kv, streaming, lag, q_pos0, k_pos0):
    i = pl.program_id(0)
    t = jnp.minimum(i, pl.num_programs(0) - 1 - lag)
    cur = lax.rem(i, 2) if lag else 0
    tq, d = x_ref.shape
    nsub = tq // sq
    nk_real = WINDOW + sq
    nk = -(-nk_real // LANES) * LANES
    assert nk == nk_real or not streaming
    n_cols = q_ref.shape[0]
    cols_per_kv = n_cols // n_kv
    n_blocks = nsub * n_kv
    blocks_per_chunk = n_blocks // (d // ATTN_OUT_COLS)

    if lag:
        @pl.when(i == 0)
        def _():
            att_ref[1] = jnp.zeros((tq, d), BF16)

    if streaming:
        k_all = jnp.concatenate([kp_ref[...], kc_ref[...]], axis=0)
        v_all = jnp.concatenate([vp_ref[...], vc_ref[...]], axis=0)

    row = lax.broadcasted_iota(jnp.int32, (sq, nk), 0)
    colk = lax.broadcasted_iota(jnp.int32, (sq, nk), 1)

    @functools.cache
    def sub_tile(s):
        if streaming:
            q0 = t * tq + s * sq
            k0 = q0 - WINDOW
            ks = k_all[s * sq:s * sq + nk]
            vs = v_all[s * sq:s * sq + nk]
        else:
            q0 = q_pos0
            k0 = k_pos0
            pad = [jnp.zeros((nk - nk_real, kp_ref.shape[2]), F32)] if nk > nk_real else []
            ks = jnp.concatenate([kp_ref[s], kc_ref[s * sq:(s + 1) * sq]] + pad, axis=0)
            vs = jnp.concatenate([vp_ref[s], vc_ref[s * sq:(s + 1) * sq]] + pad, axis=0)
        kpos = k0 + colk
        q_chunk = lax.shift_right_arithmetic(q0 + row, CHUNK_SHIFT)
        k_chunk = lax.shift_right_arithmetic(kpos, CHUNK_SHIFT)
        valid = (k_chunk >= q_chunk - LOOKBACK_CHUNKS) & (k_chunk <= q_chunk)
        if streaming:
            valid = valid & (kpos >= 0)
        if nk > nk_real:
            valid = valid & (colk < nk_real)
        return ks, vs, valid

    def scores(s, kv):
        ks, _, _ = sub_tile(s)
        c, first = divmod(kv, HEADS_PER_COL)
        kcol = ks[:, c * LANES:(c + 1) * LANES]
        bdk = _block_diag(kcol, pltpu.roll(kcol, HEAD_DIM, 1), first == 0).astype(BF16)
        qs = q_ref[kv * cols_per_kv:(kv + 1) * cols_per_kv, s * sq:(s + 1) * sq, :]
        qs = qs.reshape(cols_per_kv * sq, LANES)
        return lax.dot_general(qs, bdk, (((1,), (1,)), ((), ())), preferred_element_type=F32)

    def attend(s, kv, sc):
        _, vs, valid = sub_tile(s)
        c, first = divmod(kv, HEADS_PER_COL)
        vcol = vs[:, c * LANES:(c + 1) * LANES]
        bdv = _block_diag(vcol, pltpu.roll(vcol, HEAD_DIM, 1), first == 0).astype(BF16)
        p_rows = []
        for p in range(cols_per_kv):
            halves = []
            for e in range(HEADS_PER_COL):
                sink = sink_ref[(kv * cols_per_kv + p) * HEADS_PER_COL + e]
                blk = sc[p * sq:(p + 1) * sq, e * nk:(e + 1) * nk]
                blk = jnp.where(valid, blk, -jnp.inf)
                mx = jnp.maximum(jnp.max(blk, axis=-1, keepdims=True), sink)
                ex = jnp.exp(blk - mx)
                den = jnp.sum(ex, axis=-1, keepdims=True) + jnp.exp(sink - mx)
                halves.append((ex * (1.0 / den)).astype(BF16))
            p_rows.append(jnp.concatenate(halves, axis=1))
        pm = jnp.concatenate(p_rows, axis=0)
        ov = jnp.dot(pm, bdv, preferred_element_type=F32)
        for p in range(cols_per_kv):
            cc = kv * cols_per_kv + p
            att_ref[cur, s * sq:(s + 1) * sq, cc * LANES:(cc + 1) * LANES] = (
                ov[p * sq:(p + 1) * sq].astype(BF16))

    blocks = [(s, kv) for s in range(nsub) for kv in range(n_kv)]
    sc_next = scores(*blocks[0])
    for b, (s, kv) in enumerate(blocks):
        sc = sc_next
        if b + 1 < n_blocks:
            sc_next = scores(*blocks[b + 1])
        if lag and (b + 1) % blocks_per_chunk == 0:
            oc = ((b + 1) // blocks_per_chunk - 1) * ATTN_OUT_COLS
            o_ref[:, oc:oc + ATTN_OUT_COLS] = jnp.dot(
                att_ref[1 - cur], wo_ref[:, oc:oc + ATTN_OUT_COLS], preferred_element_type=F32)
        attend(s, kv, sc)

    if not lag:
        o_ref[...] = jnp.dot(att_ref[0], wo_ref[...], preferred_element_type=F32)
    o_ref[...] = x_ref[...] + _rms(o_ref[...], g_ref[...])


def _attention(x, q, k_prev, k, v_prev, v, sinks, norm_g, w_o, layer, j, *, tq, sq, n_kv,
               streaming):
    m, d = x.shape
    n_cols = q.shape[0]
    kv_w = k.shape[1]
    nsub = tq // sq
    n_tiles = m // tq
    lag = int(n_tiles >= ATTN_MIN_LAGGED_TILES)
    att_tile = lambda i: jnp.minimum(i, n_tiles - 1)
    out_tile = lambda i: jnp.maximum(i - lag, 0)
    if streaming:
        per = tq // WINDOW
        prev_spec = pl.BlockSpec((WINDOW, kv_w), lambda i: (jnp.maximum(att_tile(i) * per - 1, 0), 0))
    else:
        prev_spec = pl.BlockSpec((nsub, WINDOW, kv_w), lambda i: (att_tile(i), 0, 0))
    cur_spec = pl.BlockSpec((tq, kv_w), lambda i: (att_tile(i), 0))
    kern = functools.partial(_attn_kernel, sq=sq, n_kv=n_kv, streaming=streaming, lag=lag,
                             q_pos0=PAST_LEN, k_pos0=PAST_LEN - WINDOW)
    return pl.pallas_call(
        kern,
        grid=(n_tiles + lag,),
        in_specs=[
            pl.BlockSpec(memory_space=pltpu.SMEM),
            pl.BlockSpec((n_cols, tq, LANES), lambda i: (0, att_tile(i), 0)),
            prev_spec, cur_spec, prev_spec, cur_spec,
            pl.BlockSpec((tq, d), lambda i: (out_tile(i), 0)),
            pl.BlockSpec((None, None, 1, d), lambda i: (layer, 3, 0, 0)),
            pl.BlockSpec((None, d, d), lambda i: (j, 0, 0)),
        ],
        out_specs=pl.BlockSpec((tq, d), lambda i: (out_tile(i), 0)),
        out_shape=jax.ShapeDtypeStruct((m, d), F32),
        scratch_shapes=[pltpu.VMEM((1 + lag, tq, d), BF16)],
        compiler_params=_params("arbitrary"),
        name="swa_attention",
    )(sinks, q, k_prev, k, v_prev, v, x, norm_g, w_o)


def _rope_tables(pos):
    half = HEAD_DIM // 2
    inv_freq = ROPE_THETA ** (-jnp.arange(half, dtype=F32) / half)
    ang = pos.astype(F32)[:, None] * inv_freq[None, :]
    cos = jnp.cos(ang)
    sin = jnp.sin(ang)
    cos_t = jnp.tile(cos, (1, LANES // half))
    sin_t = jnp.tile(jnp.concatenate([-sin, sin], axis=1), (1, HEADS_PER_COL))
    return cos_t, sin_t


def kernel(x_prompt, x_sample, state_conv, cache_k, cache_v, norm_g, w_ffn_in, w_ffn_out,
           w_conv_in, w_conv, w_conv_out, w_qkv, w_attn_out, attn_sinks):
    batch, seq, d = x_prompt.shape
    dec_batch, dec_seq, _ = x_sample.shape
    depth = norm_g.shape[0]
    n_heads = attn_sinks.shape[1]
    n_kv = cache_k.shape[3]
    kv_w = n_kv * HEAD_DIM
    assert batch == 1 and dec_seq == CHUNK and cache_k.shape[2] == WINDOW

    g4 = norm_g.reshape(depth, norm_g.shape[1], 1, d)
    ffn_order = [(layer, slot) for layer in range(depth) for slot in range(2)]
    ffn_w = {}
    w_conv_in_b = w_conv_in.astype(BF16)
    w_conv_out_b = w_conv_out.astype(BF16)
    w_qkv_b = w_qkv.astype(BF16)
    w_attn_out_b = w_attn_out.astype(BF16)

    xp = x_prompt.reshape(seq, d)
    xs = x_sample.reshape(dec_batch * dec_seq, d)
    cos_p, sin_p = _rope_tables(jnp.arange(seq))
    cos_s, sin_s = _rope_tables(jnp.tile(PAST_LEN + jnp.arange(dec_seq), dec_batch))

    tm = ROW_TILE
    conv_p, conv_s, k_p, v_p, k_s, v_s = [], [], [], [], [], []
    for layer in range(depth):
        j = layer // 2
        def ffn_pair(xp, xs, slot, n_pre, n_post):
            ffn = functools.partial(_ffn, norm_g=g4, layer=layer, n_pre=n_pre, n_post=n_post, tm=FFN_TM)
            here = ffn_order.index((layer, slot))
            if here == 0:
                f32_w = ((w_ffn_in, (0, 0), 0), (w_ffn_in, (0, 0), w_ffn_out.shape[2]),
                         (w_ffn_out, (0, 0), 0))
                xs, ffn_w[(layer, slot)] = ffn(xs, weights=f32_w, tf=FFN_TF_F32)
            else:
                xs = ffn(xs, weights=ffn_w[(layer, slot)], tf=FFN_TF)
            if here + 1 < len(ffn_order):
                xp, ffn_w[ffn_order[here + 1]] = ffn(
                    xp, weights=ffn_w[(layer, slot)], tf=FFN_TF,
                    next_w=(w_ffn_in, w_ffn_out) + ffn_order[here + 1])
            else:
                xp = ffn(xp, weights=ffn_w[(layer, slot)], tf=FFN_TF)
            return xp, xs

        xp, xs = ffn_pair(xp, xs, 0, 0, 1)
        if layer % 2 == 0:
            mix = functools.partial(_conv_mixer, norm_g=g4, w_in=w_conv_in_b, conv_w=w_conv,
                                    w_out=w_conv_out_b, layer=layer, j=j, tm=tm)
            xp, st = mix(xp, jnp.zeros((1, 2, d), F32), seg=tm, carried=True, tn=d)
            conv_p.append(st)
            xs, st = mix(xs, state_conv[j], seg=dec_seq, carried=False, tn=CONV_TN)
            conv_s.append(st)
        else:
            proj = functools.partial(_qkv_project, norm_g=g4, w_qkv=w_qkv_b, layer=layer, j=j,
                                     tm=tm, n_heads=n_heads, n_kv=n_kv)
            att = functools.partial(_attention, sinks=attn_sinks[j], norm_g=g4, w_o=w_attn_out_b,
                                    layer=layer, j=j, n_kv=n_kv)
            q, k, v = proj(xp, cos=cos_p, sin=sin_p)
            xp = att(xp, q, k, k, v, v, tq=tm, sq=WINDOW, streaming=True)
            k_p.append(k[seq - WINDOW:].reshape(1, WINDOW, n_kv, HEAD_DIM))
            v_p.append(v[seq - WINDOW:].reshape(1, WINDOW, n_kv, HEAD_DIM))
            q, k, v = proj(xs, cos=cos_s, sin=sin_s)
            ck = cache_k[j].reshape(dec_batch, WINDOW, kv_w)
            cv = cache_v[j].reshape(dec_batch, WINDOW, kv_w)
            xs = att(xs, q, ck, k, cv, v, tq=tm, sq=dec_seq, streaming=False)
            k_new = jnp.concatenate([ck, k.reshape(dec_batch, dec_seq, kv_w)], axis=1)[:, -WINDOW:]
            v_new = jnp.concatenate([cv, v.reshape(dec_batch, dec_seq, kv_w)], axis=1)[:, -WINDOW:]
            k_s.append(k_new.reshape(dec_batch, WINDOW, n_kv, HEAD_DIM))
            v_s.append(v_new.reshape(dec_batch, WINDOW, n_kv, HEAD_DIM))
        xp, xs = ffn_pair(xp, xs, 1, 4, 5)

    return (xp.reshape(batch, seq, d), xs.reshape(dec_batch, dec_seq, d),
            jnp.stack(conv_p), jnp.stack(conv_s),
            jnp.stack(k_p), jnp.stack(v_p), jnp.stack(k_s), jnp.stack(v_s))
```

```python
import functools

import jax
import jax.numpy as jnp
from jax import lax
from jax.experimental import pallas as pl
from jax.experimental.pallas import tpu as pltpu

F32 = jnp.float32
BF16 = jnp.bfloat16

CHUNK = 64
CHUNK_SHIFT = CHUNK.bit_length() - 1
assert 1 << CHUNK_SHIFT == CHUNK
LOOKBACK_CHUNKS = 2
WINDOW = LOOKBACK_CHUNKS * CHUNK
HEAD_DIM = 64
PAST_LEN = 2048
ROPE_THETA = 10000.0
NORM_EPS = 1e-6

LANES = 128
HEADS_PER_COL = LANES // HEAD_DIM
VMEM_LIMIT = 62 * 1024 * 1024
ROW_TILE = 512
FFN_TM = 1024
FFN_ROWS = 512
FFN_TF = 512
FFN_TF_F32 = 256
CONV_TN = 512
CONV_COLS = 256
ATTN_OUT_COLS = 256
ATTN_MIN_LAGGED_TILES = 4


def _rms(x, g):
    ms = jnp.mean(x * x, axis=-1, keepdims=True)
    return x * lax.rsqrt(ms + NORM_EPS) * g


def _params(*sem):
    return pltpu.CompilerParams(dimension_semantics=sem, vmem_limit_bytes=VMEM_LIMIT)


def _ffn_kernel(x_ref, gpre_ref, gpost_ref, wg_ref, wu_ref, wo_ref, o_ref, *rest, cast_own):
    own_out, h_ref = rest[:-1], rest[-1]
    j = pl.program_id(1)
    last = pl.num_programs(1) - 1

    def step(first, final):
        wg, wu, wo = wg_ref[...], wu_ref[...], wo_ref[...]
        if cast_own:
            wg, wu, wo = wg.astype(BF16), wu.astype(BF16), wo.astype(BF16)
            for w, dst in zip((wg, wu, wo), own_out):
                dst[...] = w
        for r in range(0, x_ref.shape[0], FFN_ROWS):
            rows = slice(r, r + FFN_ROWS)
            if first:
                h = _rms(x_ref[rows], gpre_ref[...]).astype(BF16)
                h_ref[rows] = h
            else:
                h = h_ref[rows]
            gate = jnp.dot(h, wg, preferred_element_type=F32)
            up = jnp.dot(h, wu, preferred_element_type=F32)
            act = (gate * jax.nn.sigmoid(gate) * up).astype(BF16)
            part = jnp.dot(act, wo, preferred_element_type=F32)
            if first:
                o_ref[rows] = part
            elif final:
                o_ref[rows] = x_ref[rows] + 0.5 * _rms(o_ref[rows] + part, gpost_ref[...])
            else:
                o_ref[rows] += part

    pl.when(j == 0)(functools.partial(step, True, False))
    pl.when((j > 0) & (j < last))(functools.partial(step, False, False))
    pl.when(j == last)(functools.partial(step, False, True))


def _ffn(x, norm_g, weights, layer, n_pre, n_post, *, tm, tf):
    m, d = x.shape
    (wg, wg_lead, wg_off), (wu, wu_lead, wu_off), (wo, wo_lead, _) = weights
    f = wo.shape[-2]
    cast_own = wg.dtype == F32
    assert not cast_own or m == tm
    g_spec = lambda n: pl.BlockSpec((None, None, 1, d), lambda i, j: (layer, n, 0, 0))
    col_spec = lambda lead, off: pl.BlockSpec((None,) * len(lead) + (d, tf),
                                              lambda i, j: lead + (0, j + off // tf))
    row_spec = lambda lead: pl.BlockSpec((None,) * len(lead) + (tf, d), lambda i, j: lead + (j, 0))
    out_specs = [pl.BlockSpec((tm, d), lambda i, j: (i, 0))]
    out_shape = [jax.ShapeDtypeStruct((m, d), F32)]
    if cast_own:
        out_specs += [col_spec((), 0), col_spec((), 0), row_spec(())]
        out_shape += [jax.ShapeDtypeStruct((d, f), BF16), jax.ShapeDtypeStruct((d, f), BF16),
                      jax.ShapeDtypeStruct((f, d), BF16)]
    outs = pl.pallas_call(
        functools.partial(_ffn_kernel, cast_own=cast_own),
        grid=(m // tm, f // tf),
        in_specs=[
            pl.BlockSpec((tm, d), lambda i, j: (i, 0)),
            g_spec(n_pre),
            g_spec(n_post),
            col_spec(wg_lead, wg_off),
            col_spec(wu_lead, wu_off),
            row_spec(wo_lead),
        ],
        out_specs=out_specs,
        out_shape=out_shape,
        scratch_shapes=[pltpu.VMEM((tm, d), BF16)],
        compiler_params=_params("parallel", "arbitrary"),
        name="ffn",
    )(x, norm_g, norm_g, wg, wu, wo)
    if cast_own:
        return outs[0], ((outs[1], (), 0), (outs[2], (), 0), (outs[3], (), 0))
    return outs[0]


def _conv_kernel(x_ref, gpre_ref, gpost_ref, wb_ref, wc_ref, wh_ref, cw_ref, wo_ref, st_ref,
                 o_ref, ns_ref, h_ref, carry_ref, *, seg, carried):
    i = pl.program_id(0)
    n = pl.program_id(1)
    last = pl.num_programs(1) - 1
    tm = x_ref.shape[0]
    tn = wb_ref.shape[1]
    tc = CONV_COLS
    cols = pl.ds(pl.multiple_of(n * tn, tn), tn)

    if carried:
        @pl.when(i == 0)
        def _():
            carry_ref[:, cols] = st_ref[0, :, cols]

    def gated_conv(c0, b_gate, c_gate, hh):
        gcols = pl.ds(pl.multiple_of(n * tn + c0, tc), tc)
        cw = cw_ref[:, c0:c0 + tc]
        w0, w1, w2 = cw[0:1], cw[1:2], cw[2:3]
        row = lax.broadcasted_iota(jnp.int32, (seg, tc), 0)
        u = c_gate * hh
        convs = []
        for s in range(tm // seg):
            us = u[s * seg:(s + 1) * seg]
            prev = carry_ref[:, gcols] if carried else st_ref[s, :, c0:c0 + tc]
            p1 = jnp.where(row == 0, prev[1:2], pltpu.roll(us, 1, 0))
            p2 = jnp.where(row == 0, prev[0:1], jnp.where(row == 1, prev[1:2], pltpu.roll(us, 2, 0)))
            convs.append(p2 * w0 + p1 * w1 + us * w2)
            if carried:
                carry_ref[:, gcols] = us[seg - 2:seg]
                ns_ref[0, :, gcols] = us[seg - 2:seg]
            else:
                ns_ref[s, :, c0:c0 + tc] = us[seg - 2:seg]
        conv = convs[0] if len(convs) == 1 else jnp.concatenate(convs, axis=0)
        return (b_gate * conv).astype(BF16)

    def out_proj(assign, final, c0, z):
        part = jnp.dot(z, wo_ref[c0:c0 + tc, :], preferred_element_type=F32)
        if assign:
            o_ref[...] = part
        elif final:
            o_ref[...] = x_ref[...] + _rms(o_ref[...] + part, gpost_ref[...])
        else:
            o_ref[...] += part

    def step(first, final):
        if first:
            h = _rms(x_ref[...], gpre_ref[...]).astype(BF16)
            h_ref[...] = h
        else:
            h = h_ref[...]
        pending = None
        for c0 in range(0, tn, tc):
            b_gate = jnp.dot(h, wb_ref[:, c0:c0 + tc], preferred_element_type=F32)
            c_gate = jnp.dot(h, wc_ref[:, c0:c0 + tc], preferred_element_type=F32)
            hh = jnp.dot(h, wh_ref[:, c0:c0 + tc], preferred_element_type=F32)
            if pending is not None:
                out_proj(first and pending[0] == 0, False, *pending)
            pending = (c0, gated_conv(c0, b_gate, c_gate, hh))
        out_proj(first and pending[0] == 0, final, *pending)

    if tn == o_ref.shape[1]:
        step(True, True)
    else:
        pl.when(n == 0)(functools.partial(step, True, False))
        pl.when((n > 0) & (n < last))(functools.partial(step, False, False))
        pl.when(n == last)(functools.partial(step, False, True))


def _conv_mixer(x, state, norm_g, w_in, conv_w, w_out, layer, j, *, tm, tn, seg, carried):
    m, d = x.shape
    nn = d // tn
    nseg = tm // seg
    g_spec = lambda n: pl.BlockSpec((None, None, 1, d), lambda i, c: (layer, n, 0, 0))
    once = dict(pipeline_mode=pl.Buffered(1)) if nn == 1 else {}
    w_spec = lambda sec: pl.BlockSpec((None, d, tn), lambda i, c: (j, 0, c + sec * nn), **once)
    if carried:
        st_spec = pl.BlockSpec((1, 2, d), lambda i, c: (0, 0, 0))
    else:
        st_spec = pl.BlockSpec((nseg, 2, tn), lambda i, c: (i, 0, c))
    return pl.pallas_call(
        functools.partial(_conv_kernel, seg=seg, carried=carried),
        grid=(m // tm, nn),
        in_specs=[
            pl.BlockSpec((tm, d), lambda i, c: (i, 0)),
            g_spec(2),
            g_spec(3),
            w_spec(0),
            w_spec(1),
            w_spec(2),
            pl.BlockSpec((None, 3, tn), lambda i, c: (j, 0, c)),
            pl.BlockSpec((None, tn, d), lambda i, c: (j, c, 0), **once),
            st_spec,
        ],
        out_specs=[pl.BlockSpec((tm, d), lambda i, c: (i, 0)), st_spec],
        out_shape=[
            jax.ShapeDtypeStruct((m, d), F32),
            jax.ShapeDtypeStruct(state.shape, F32),
        ],
        scratch_shapes=[pltpu.VMEM((tm, d), BF16), pltpu.VMEM((2, d), F32)],
        compiler_params=_params("arbitrary", "arbitrary"),
        name="conv_mixer",
    )(x, norm_g, norm_g, w_in, w_in, w_in, conv_w, w_out, state)


def _rope_col(xc, cos, sin_signed):
    lane = lax.broadcasted_iota(jnp.int32, xc.shape, 1)
    first_half = (lane % HEAD_DIM) < HEAD_DIM // 2
    swapped = jnp.where(first_half,
                        pltpu.roll(xc, LANES - HEAD_DIM // 2, 1),
                        pltpu.roll(xc, HEAD_DIM // 2, 1))
    return xc * cos + swapped * sin_signed


def _qkv_kernel(x_ref, g_ref, w_ref, cos_ref, sin_ref, q_ref, k_ref, v_ref, *, chunk):
    n_q = q_ref.shape[0] * LANES
    kv_w = k_ref.shape[1]
    h = _rms(x_ref[...], g_ref[...]).astype(BF16)
    cos = cos_ref[...]
    sin = sin_ref[...]
    for c0 in range(0, n_q, chunk):
        y = jnp.dot(h, w_ref[:, c0:c0 + chunk], preferred_element_type=F32)
        for c in range(chunk // LANES):
            qc = _rope_col(y[:, c * LANES:(c + 1) * LANES], cos, sin) * (HEAD_DIM ** -0.5)
            q_ref[c0 // LANES + c] = qc.astype(BF16)
    y = jnp.dot(h, w_ref[:, n_q:], preferred_element_type=F32)
    for c in range(kv_w // LANES):
        k_ref[:, c * LANES:(c + 1) * LANES] = _rope_col(y[:, c * LANES:(c + 1) * LANES], cos, sin)
    v_ref[...] = y[:, kv_w:]


def _qkv_project(x, norm_g, w_qkv, cos, sin, layer, j, *, tm, n_heads, n_kv):
    m, d = x.shape
    kv_w = n_kv * HEAD_DIM
    n_cols = n_heads // HEADS_PER_COL
    return pl.pallas_call(
        functools.partial(_qkv_kernel, chunk=2 * kv_w),
        grid=(m // tm,),
        in_specs=[
            pl.BlockSpec((tm, d), lambda i: (i, 0)),
            pl.BlockSpec((None, None, 1, d), lambda i: (layer, 2, 0, 0)),
            pl.BlockSpec((None, d, w_qkv.shape[2]), lambda i: (j, 0, 0)),
            pl.BlockSpec((tm, LANES), lambda i: (i, 0)),
            pl.BlockSpec((tm, LANES), lambda i: (i, 0)),
        ],
        out_specs=[
            pl.BlockSpec((n_cols, tm, LANES), lambda i: (0, i, 0)),
            pl.BlockSpec((tm, kv_w), lambda i: (i, 0)),
            pl.BlockSpec((tm, kv_w), lambda i: (i, 0)),
        ],
        out_shape=[
            jax.ShapeDtypeStruct((n_cols, m, LANES), BF16),
            jax.ShapeDtypeStruct((m, kv_w), F32),
            jax.ShapeDtypeStruct((m, kv_w), F32),
        ],
        compiler_params=_params("parallel"),
        name="qkv_rope",
    )(x, norm_g, w_qkv, cos, sin)


def _block_diag(col, rolled, first):
    lane = lax.broadcasted_iota(jnp.int32, col.shape, 1)
    lo = lane < HEAD_DIM
    zero = jnp.zeros_like(col)
    if first:
        return jnp.concatenate([jnp.where(lo, col, zero), jnp.where(lo, zero, rolled)], axis=0)
    return jnp.concatenate([jnp.where(lo, rolled, zero), jnp.where(lo, zero, col)], axis=0)


def _attn_kernel(sink_ref, q_ref, kp_ref, kc_ref, vp_ref, vc_ref, x_ref, g_ref, wo_ref, o_ref,
                 att_ref, *, sq, n_kv, streaming, lag, q_pos0, k_pos0):
    i = pl.program_id(0)
    t = jnp.minimum(i, pl.num_programs(0) - 1 - lag)
    cur = lax.rem(i, 2) if lag else 0
    tq, d = x_ref.shape
    nsub = tq // sq
    nk_real = WINDOW + sq
    nk = -(-nk_real // LANES) * LANES
    assert nk == nk_real or not streaming
    n_cols = q_ref.shape[0]
    cols_per_kv = n_cols // n_kv
    n_blocks = nsub * n_kv
    blocks_per_chunk = n_blocks // (d // ATTN_OUT_COLS)

    if lag:
        @pl.when(i == 0)
        def _():
            att_ref[1] = jnp.zeros((tq, d), BF16)

    if streaming:
        k_all = jnp.concatenate([kp_ref[...], kc_ref[...]], axis=0)
        v_all = jnp.concatenate([vp_ref[...], vc_ref[...]], axis=0)

    row = lax.broadcasted_iota(jnp.int32, (sq, nk), 0)
    colk = lax.broadcasted_iota(jnp.int32, (sq, nk), 1)

    @functools.cache
    def sub_tile(s):
        if streaming:
            q0 = t * tq + s * sq
            k0 = q0 - WINDOW
            ks = k_all[s * sq:s * sq + nk]
            vs = v_all[s * sq:s * sq + nk]
        else:
            q0 = q_pos0
            k0 = k_pos0
            pad = [jnp.zeros((nk - nk_real, kp_ref.shape[2]), F32)] if nk > nk_real else []
            ks = jnp.concatenate([kp_ref[s], kc_ref[s * sq:(s + 1) * sq]] + pad, axis=0)
            vs = jnp.concatenate([vp_ref[s], vc_ref[s * sq:(s + 1) * sq]] + pad, axis=0)
        kpos = k0 + colk
        q_chunk = lax.shift_right_arithmetic(q0 + row, CHUNK_SHIFT)
        k_chunk = lax.shift_right_arithmetic(kpos, CHUNK_SHIFT)
        valid = (k_chunk >= q_chunk - LOOKBACK_CHUNKS) & (k_chunk <= q_chunk)
        if streaming:
            valid = valid & (kpos >= 0)
        if nk > nk_real:
            valid = valid & (colk < nk_real)
        return ks, vs, valid

    def scores(s, kv):
        ks, _, _ = sub_tile(s)
        c, first = divmod(kv, HEADS_PER_COL)
        kcol = ks[:, c * LANES:(c + 1) * LANES]
        bdk = _block_diag(kcol, pltpu.roll(kcol, HEAD_DIM, 1), first == 0).astype(BF16)
        qs = q_ref[kv * cols_per_kv:(kv + 1) * cols_per_kv, s * sq:(s + 1) * sq, :]
        qs = qs.reshape(cols_per_kv * sq, LANES)
        return lax.dot_general(qs, bdk, (((1,), (1,)), ((), ())), preferred_element_type=F32)

    def attend(s, kv, sc):
        _, vs, valid = sub_tile(s)
        c, first = divmod(kv, HEADS_PER_COL)
        vcol = vs[:, c * LANES:(c + 1) * LANES]
        bdv = _block_diag(vcol, pltpu.roll(vcol, HEAD_DIM, 1), first == 0).astype(BF16)
        p_rows = []
        for p in range(cols_per_kv):
            halves = []
            for e in range(HEADS_PER_COL):
                sink = sink_ref[(kv * cols_per_kv + p) * HEADS_PER_COL + e]
                blk = sc[p * sq:(p + 1) * sq, e * nk:(e + 1) * nk]
                blk = jnp.where(valid, blk, -jnp.inf)
                mx = jnp.maximum(jnp.max(blk, axis=-1, keepdims=True), sink)
                ex = jnp.exp(blk - mx)
                den = jnp.sum(ex, axis=-1, keepdims=True) + jnp.exp(sink - mx)
                halves.append((ex * (1.0 / den)).astype(BF16))
            p_rows.append(jnp.concatenate(halves, axis=1))
        pm = jnp.concatenate(p_rows, axis=0)
        ov = jnp.dot(pm, bdv, preferred_element_type=F32)
        for p in range(cols_per_kv):
            cc = kv * cols_per_kv + p
            att_ref[cur, s * sq:(s + 1) * sq, cc * LANES:(cc + 1) * LANES] = (
                ov[p * sq:(p + 1) * sq].astype(BF16))

    blocks = [(s, kv) for s in range(nsub) for kv in range(n_kv)]
    sc_next = scores(*blocks[0])
    for b, (s, kv) in enumerate(blocks):
        sc = sc_next
        if b + 1 < n_blocks:
            sc_next = scores(*blocks[b + 1])
        if lag and (b + 1) % blocks_per_chunk == 0:
            oc = ((b + 1) // blocks_per_chunk - 1) * ATTN_OUT_COLS
            o_ref[:, oc:oc + ATTN_OUT_COLS] = jnp.dot(
                att_ref[1 - cur], wo_ref[:, oc:oc + ATTN_OUT_COLS], preferred_element_type=F32)
        attend(s, kv, sc)

    if not lag:
        o_ref[...] = jnp.dot(att_ref[0], wo_ref[...], preferred_element_type=F32)
    o_ref[...] = x_ref[...] + _rms(o_ref[...], g_ref[...])


def _attention(x, q, k_prev, k, v_prev, v, sinks, norm_g, w_o, layer, j, *, tq, sq, n_kv,
               streaming):
    m, d = x.shape
    n_cols = q.shape[0]
    kv_w = k.shape[1]
    nsub = tq // sq
    n_tiles = m // tq
    lag = int(n_tiles >= ATTN_MIN_LAGGED_TILES)
    att_tile = lambda i: jnp.minimum(i, n_tiles - 1)
    out_tile = lambda i: jnp.maximum(i - lag, 0)
    if streaming:
        per = tq // WINDOW
        prev_spec = pl.BlockSpec((WINDOW, kv_w), lambda i: (jnp.maximum(att_tile(i) * per - 1, 0), 0))
    else:
        prev_spec = pl.BlockSpec((nsub, WINDOW, kv_w), lambda i: (att_tile(i), 0, 0))
    cur_spec = pl.BlockSpec((tq, kv_w), lambda i: (att_tile(i), 0))
    kern = functools.partial(_attn_kernel, sq=sq, n_kv=n_kv, streaming=streaming, lag=lag,
                             q_pos0=PAST_LEN, k_pos0=PAST_LEN - WINDOW)
    return pl.pallas_call(
        kern,
        grid=(n_tiles + lag,),
        in_specs=[
            pl.BlockSpec(memory_space=pltpu.SMEM),
            pl.BlockSpec((n_cols, tq, LANES), lambda i: (0, att_tile(i), 0)),
            prev_spec, cur_spec, prev_spec, cur_spec,
            pl.BlockSpec((tq, d), lambda i: (out_tile(i), 0)),
            pl.BlockSpec((None, None, 1, d), lambda i: (layer, 3, 0, 0)),
            pl.BlockSpec((None, d, d), lambda i: (j, 0, 0)),
        ],
        out_specs=pl.BlockSpec((tq, d), lambda i: (out_tile(i), 0)),
        out_shape=jax.ShapeDtypeStruct((m, d), F32),
        scratch_shapes=[pltpu.VMEM((1 + lag, tq, d), BF16)],
        compiler_params=_params("arbitrary"),
        name="swa_attention",
    )(sinks, q, k_prev, k, v_prev, v, x, norm_g, w_o)


def _rope_tables(pos):
    half = HEAD_DIM // 2
    inv_freq = ROPE_THETA ** (-jnp.arange(half, dtype=F32) / half)
    inv_freq_lanes = jnp.tile(inv_freq, LANES // half)
    sign_lanes = jnp.tile(jnp.concatenate([-jnp.ones(half, F32), jnp.ones(half, F32)]), HEADS_PER_COL)
    ang = pos.astype(F32)[:, None] * inv_freq_lanes[None, :]
    return jnp.cos(ang), sign_lanes[None, :] * jnp.sin(ang)


def kernel(x_prompt, x_sample, state_conv, cache_k, cache_v, norm_g, w_ffn_in, w_ffn_out,
           w_conv_in, w_conv, w_conv_out, w_qkv, w_attn_out, attn_sinks):
    batch, seq, d = x_prompt.shape
    dec_batch, dec_seq, _ = x_sample.shape
    depth = norm_g.shape[0]
    n_heads = attn_sinks.shape[1]
    n_kv = cache_k.shape[3]
    kv_w = n_kv * HEAD_DIM
    assert batch == 1 and dec_seq == CHUNK and cache_k.shape[2] == WINDOW

    g4 = norm_g.reshape(depth, norm_g.shape[1], 1, d)
    w_conv_in_b = w_conv_in.astype(BF16)
    w_conv_out_b = w_conv_out.astype(BF16)
    w_qkv_b = w_qkv.astype(BF16)
    w_attn_out_b = w_attn_out.astype(BF16)

    xp = x_prompt.reshape(seq, d)
    xs = x_sample.reshape(dec_batch * dec_seq, d)
    cos_p, sin_p = _rope_tables(jnp.arange(seq))
    cos_s, sin_s = _rope_tables(jnp.tile(PAST_LEN + jnp.arange(dec_seq), dec_batch))

    tm = ROW_TILE
    conv_p, conv_s, k_p, v_p, k_s, v_s = [], [], [], [], [], []
    for layer in range(depth):
        j = layer // 2
        def ffn_pair(xp, xs, slot, n_pre, n_post):
            ffn = functools.partial(_ffn, norm_g=g4, layer=layer, n_pre=n_pre, n_post=n_post, tm=FFN_TM)
            f32_w = ((w_ffn_in, (layer, slot), 0), (w_ffn_in, (layer, slot), w_ffn_out.shape[2]),
                     (w_ffn_out, (layer, slot), 0))
            xs, bf16_w = ffn(xs, weights=f32_w, tf=FFN_TF_F32)
            return ffn(xp, weights=bf16_w, tf=FFN_TF), xs

        xp, xs = ffn_pair(xp, xs, 0, 0, 1)
        if layer % 2 == 0:
            mix = functools.partial(_conv_mixer, norm_g=g4, w_in=w_conv_in_b, conv_w=w_conv,
                                    w_out=w_conv_out_b, layer=layer, j=j, tm=tm)
            xp, st = mix(xp, jnp.zeros((1, 2, d), F32), seg=tm, carried=True, tn=d)
            conv_p.append(st)
            xs, st = mix(xs, state_conv[j], seg=dec_seq, carried=False, tn=CONV_TN)
            conv_s.append(st)
        else:
            proj = functools.partial(_qkv_project, norm_g=g4, w_qkv=w_qkv_b, layer=layer, j=j,
                                     tm=tm, n_heads=n_heads, n_kv=n_kv)
            att = functools.partial(_attention, sinks=attn_sinks[j], norm_g=g4, w_o=w_attn_out_b,
                                    layer=layer, j=j, n_kv=n_kv)
            q, k, v = proj(xp, cos=cos_p, sin=sin_p)
            xp = att(xp, q, k, k, v, v, tq=tm, sq=WINDOW, streaming=True)
            k_p.append(k[seq - WINDOW:].reshape(1, WINDOW, n_kv, HEAD_DIM))
            v_p.append(v[seq - WINDOW:].reshape(1, WINDOW, n_kv, HEAD_DIM))
            q, k, v = proj(xs, cos=cos_s, sin=sin_s)
            ck = cache_k[j].reshape(dec_batch, WINDOW, kv_w)
            cv = cache_v[j].reshape(dec_batch, WINDOW, kv_w)
            xs = att(xs, q, ck, k, cv, v, tq=tm, sq=dec_seq, streaming=False)
            k_new = jnp.concatenate([ck, k.reshape(dec_batch, dec_seq, kv_w)], axis=1)[:, -WINDOW:]
            v_new = jnp.concatenate([cv, v.reshape(dec_batch, dec_seq, kv_w)], axis=1)[:, -WINDOW:]
            k_s.append(k_new.reshape(dec_batch, WINDOW, n_kv, HEAD_DIM))
            v_s.append(v_new.reshape(dec_batch, WINDOW, n_kv, HEAD_DIM))
        xp, xs = ffn_pair(xp, xs, 1, 4, 5)

    return (xp.reshape(batch, seq, d), xs.reshape(dec_batch, dec_seq, d),
            jnp.stack(conv_p), jnp.stack(conv_s),
            jnp.stack(k_p), jnp.stack(v_p), jnp.stack(k_s), jnp.stack(v_s))
```

```python
import functools

import jax
import jax.numpy as jnp
from jax import lax
from jax.experimental import pallas as pl
from jax.experimental.pallas import tpu as pltpu

F32 = jnp.float32
BF16 = jnp.bfloat16

CHUNK = 64
CHUNK_SHIFT = CHUNK.bit_length() - 1
assert 1 << CHUNK_SHIFT == CHUNK
LOOKBACK_CHUNKS = 2
WINDOW = LOOKBACK_CHUNKS * CHUNK
HEAD_DIM = 64
PAST_LEN = 2048
ROPE_THETA = 10000.0
NORM_EPS = 1e-6

LANES = 128
HEADS_PER_COL = LANES // HEAD_DIM
VMEM_LIMIT = 62 * 1024 * 1024
ROW_TILE = 512
FFN_TM = 1024
FFN_ROWS = 512
FFN_TF = 512
FFN_TF_F32 = 256
CONV_TN = 512
CONV_COLS = 256
ATTN_OUT_COLS = 256
ATTN_MIN_LAGGED_TILES = 4


def _rms(x, g):
    ms = jnp.mean(x * x, axis=-1, keepdims=True)
    return x * lax.rsqrt(ms + NORM_EPS) * g


def _params(*sem):
    return pltpu.CompilerParams(dimension_semantics=sem, vmem_limit_bytes=VMEM_LIMIT)


def _ffn_kernel(x_ref, gpre_ref, gpost_ref, wg_ref, wu_ref, wo_ref, o_ref, *rest, cast_own):
    own_out, h_ref = rest[:-1], rest[-1]
    j = pl.program_id(1)
    last = pl.num_programs(1) - 1

    def step(first, final):
        wg, wu, wo = wg_ref[...], wu_ref[...], wo_ref[...]
        if cast_own:
            wg, wu, wo = wg.astype(BF16), wu.astype(BF16), wo.astype(BF16)
            for w, dst in zip((wg, wu, wo), own_out):
                dst[...] = w
        for r in range(0, x_ref.shape[0], FFN_ROWS):
            rows = slice(r, r + FFN_ROWS)
            if first:
                h = _rms(x_ref[rows], gpre_ref[...]).astype(BF16)
                h_ref[rows] = h
            else:
                h = h_ref[rows]
            gate = jnp.dot(h, wg, preferred_element_type=F32)
            up = jnp.dot(h, wu, preferred_element_type=F32)
            act = (gate * jax.nn.sigmoid(gate) * up).astype(BF16)
            part = jnp.dot(act, wo, preferred_element_type=F32)
            if first:
                o_ref[rows] = part
            elif final:
                o_ref[rows] = x_ref[rows] + 0.5 * _rms(o_ref[rows] + part, gpost_ref[...])
            else:
                o_ref[rows] += part

    pl.when(j == 0)(functools.partial(step, True, False))
    pl.when((j > 0) & (j < last))(functools.partial(step, False, False))
    pl.when(j == last)(functools.partial(step, False, True))


def _ffn(x, norm_g, weights, layer, n_pre, n_post, *, tm, tf, tile_cols=None):
    m, d = x.shape
    cast_own = weights[0].dtype == F32
    g_spec = lambda n: pl.BlockSpec((None, None, 1, d), lambda i, j: (layer, n, 0, 0))
    out_specs = [pl.BlockSpec((tm, d), lambda i, j: (i, 0))]
    out_shape = [jax.ShapeDtypeStruct((m, d), F32)]
    if cast_own:
        w_in, w_out, slot = weights
        f = w_out.shape[2]
        nf, per_tile = f // tf, tile_cols // tf
        assert m == tm and per_tile * tf == tile_cols
        w_specs = [
            pl.BlockSpec((None, None, d, tf), lambda i, j: (layer, slot, 0, j)),
            pl.BlockSpec((None, None, d, tf), lambda i, j: (layer, slot, 0, j + nf)),
            pl.BlockSpec((None, None, tf, d), lambda i, j: (layer, slot, j, 0)),
        ]
        operands = [w_in, w_in, w_out]
        tile_spec = pl.BlockSpec((None, d, tf), lambda i, j: (j // per_tile, 0, j % per_tile))
        out_specs += [tile_spec, tile_spec, pl.BlockSpec((tf, d), lambda i, j: (j, 0))]
        tiles = jax.ShapeDtypeStruct((f // tile_cols, d, tile_cols), BF16)
        out_shape += [tiles, tiles, jax.ShapeDtypeStruct((f, d), BF16)]
    else:
        gate, up, down = operands = list(weights)
        f = down.shape[0]
        assert gate.shape[2] == tf
        tile_spec = pl.BlockSpec((None, d, tf), lambda i, j: (j, 0, 0))
        w_specs = [tile_spec, tile_spec, pl.BlockSpec((tf, d), lambda i, j: (j, 0))]
    outs = pl.pallas_call(
        functools.partial(_ffn_kernel, cast_own=cast_own),
        grid=(m // tm, f // tf),
        in_specs=[pl.BlockSpec((tm, d), lambda i, j: (i, 0)), g_spec(n_pre), g_spec(n_post)] + w_specs,
        out_specs=out_specs,
        out_shape=out_shape,
        scratch_shapes=[pltpu.VMEM((tm, d), BF16)],
        compiler_params=_params("parallel", "arbitrary"),
        name="ffn",
    )(x, norm_g, norm_g, *operands)
    return (outs[0], tuple(outs[1:])) if cast_own else outs[0]


def _conv_kernel(x_ref, gpre_ref, gpost_ref, wb_ref, wc_ref, wh_ref, cw_ref, wo_ref, st_ref,
                 o_ref, ns_ref, h_ref, carry_ref, *, seg, carried):
    i = pl.program_id(0)
    n = pl.program_id(1)
    last = pl.num_programs(1) - 1
    tm = x_ref.shape[0]
    tn = wb_ref.shape[1]
    tc = CONV_COLS
    cols = pl.ds(pl.multiple_of(n * tn, tn), tn)

    if carried:
        @pl.when(i == 0)
        def _():
            carry_ref[:, cols] = st_ref[0, :, cols]

    def gated_conv(c0, b_gate, c_gate, hh):
        gcols = pl.ds(pl.multiple_of(n * tn + c0, tc), tc)
        cw = cw_ref[:, c0:c0 + tc]
        w0, w1, w2 = cw[0:1], cw[1:2], cw[2:3]
        row = lax.broadcasted_iota(jnp.int32, (seg, tc), 0)
        u = c_gate * hh
        convs = []
        for s in range(tm // seg):
            us = u[s * seg:(s + 1) * seg]
            prev = carry_ref[:, gcols] if carried else st_ref[s, :, c0:c0 + tc]
            p1 = jnp.where(row == 0, prev[1:2], pltpu.roll(us, 1, 0))
            p2 = jnp.where(row == 0, prev[0:1], jnp.where(row == 1, prev[1:2], pltpu.roll(us, 2, 0)))
            convs.append(p2 * w0 + p1 * w1 + us * w2)
            if carried:
                carry_ref[:, gcols] = us[seg - 2:seg]
                ns_ref[0, :, gcols] = us[seg - 2:seg]
            else:
                ns_ref[s, :, c0:c0 + tc] = us[seg - 2:seg]
        conv = convs[0] if len(convs) == 1 else jnp.concatenate(convs, axis=0)
        return (b_gate * conv).astype(BF16)

    def out_proj(assign, final, c0, z):
        part = jnp.dot(z, wo_ref[c0:c0 + tc, :], preferred_element_type=F32)
        if assign:
            o_ref[...] = part
        elif final:
            o_ref[...] = x_ref[...] + _rms(o_ref[...] + part, gpost_ref[...])
        else:
            o_ref[...] += part

    def step(first, final):
        if first:
            h = _rms(x_ref[...], gpre_ref[...]).astype(BF16)
            h_ref[...] = h
        else:
            h = h_ref[...]
        pending = None
        for c0 in range(0, tn, tc):
            b_gate = jnp.dot(h, wb_ref[:, c0:c0 + tc], preferred_element_type=F32)
            c_gate = jnp.dot(h, wc_ref[:, c0:c0 + tc], preferred_element_type=F32)
            hh = jnp.dot(h, wh_ref[:, c0:c0 + tc], preferred_element_type=F32)
            if pending is not None:
                out_proj(first and pending[0] == 0, False, *pending)
            pending = (c0, gated_conv(c0, b_gate, c_gate, hh))
        out_proj(first and pending[0] == 0, final, *pending)

    if tn == o_ref.shape[1]:
        step(True, True)
    else:
        pl.when(n == 0)(functools.partial(step, True, False))
        pl.when((n > 0) & (n < last))(functools.partial(step, False, False))
        pl.when(n == last)(functools.partial(step, False, True))


def _conv_mixer(x, state, norm_g, w_in, conv_w, w_out, layer, j, *, tm, tn, seg, carried):
    m, d = x.shape
    nn = d // tn
    nseg = tm // seg
    g_spec = lambda n: pl.BlockSpec((None, None, 1, d), lambda i, c: (layer, n, 0, 0))
    once = dict(pipeline_mode=pl.Buffered(1)) if nn == 1 else {}
    w_spec = lambda sec: pl.BlockSpec((None, d, tn), lambda i, c: (j, 0, c + sec * nn), **once)
    if carried:
        st_spec = pl.BlockSpec((1, 2, d), lambda i, c: (0, 0, 0))
    else:
        st_spec = pl.BlockSpec((nseg, 2, tn), lambda i, c: (i, 0, c))
    return pl.pallas_call(
        functools.partial(_conv_kernel, seg=seg, carried=carried),
        grid=(m // tm, nn),
        in_specs=[
            pl.BlockSpec((tm, d), lambda i, c: (i, 0)),
            g_spec(2),
            g_spec(3),
            w_spec(0),
            w_spec(1),
            w_spec(2),
            pl.BlockSpec((None, 3, tn), lambda i, c: (j, 0, c)),
            pl.BlockSpec((None, tn, d), lambda i, c: (j, c, 0), **once),
            st_spec,
        ],
        out_specs=[pl.BlockSpec((tm, d), lambda i, c: (i, 0)), st_spec],
        out_shape=[
            jax.ShapeDtypeStruct((m, d), F32),
            jax.ShapeDtypeStruct(state.shape, F32),
        ],
        scratch_shapes=[pltpu.VMEM((tm, d), BF16), pltpu.VMEM((2, d), F32)],
        compiler_params=_params("arbitrary", "arbitrary"),
        name="conv_mixer",
    )(x, norm_g, norm_g, w_in, w_in, w_in, conv_w, w_out, state)


def _rope_col(xc, cos, sin_signed):
    lane = lax.broadcasted_iota(jnp.int32, xc.shape, 1)
    first_half = (lane % HEAD_DIM) < HEAD_DIM // 2
    swapped = jnp.where(first_half,
                        pltpu.roll(xc, LANES - HEAD_DIM // 2, 1),
                        pltpu.roll(xc, HEAD_DIM // 2, 1))
    return xc * cos + swapped * sin_signed


def _qkv_kernel(x_ref, g_ref, w_ref, cos_ref, sin_ref, q_ref, k_ref, v_ref, *, chunk):
    n_q = q_ref.shape[0] * LANES
    kv_w = k_ref.shape[1]
    h = _rms(x_ref[...], g_ref[...]).astype(BF16)
    cos = cos_ref[...]
    sin = sin_ref[...]
    for c0 in range(0, n_q, chunk):
        y = jnp.dot(h, w_ref[:, c0:c0 + chunk], preferred_element_type=F32)
        for c in range(chunk // LANES):
            qc = _rope_col(y[:, c * LANES:(c + 1) * LANES], cos, sin) * (HEAD_DIM ** -0.5)
            q_ref[c0 // LANES + c] = qc.astype(BF16)
    y = jnp.dot(h, w_ref[:, n_q:], preferred_element_type=F32)
    for c in range(kv_w // LANES):
        k_ref[:, c * LANES:(c + 1) * LANES] = _rope_col(y[:, c * LANES:(c + 1) * LANES], cos, sin)
    v_ref[...] = y[:, kv_w:]


def _qkv_project(x, norm_g, w_qkv, cos, sin, layer, j, *, tm, n_heads, n_kv):
    m, d = x.shape
    kv_w = n_kv * HEAD_DIM
    n_cols = n_heads // HEADS_PER_COL
    return pl.pallas_call(
        functools.partial(_qkv_kernel, chunk=2 * kv_w),
        grid=(m // tm,),
        in_specs=[
            pl.BlockSpec((tm, d), lambda i: (i, 0)),
            pl.BlockSpec((None, None, 1, d), lambda i: (layer, 2, 0, 0)),
            pl.BlockSpec((None, d, w_qkv.shape[2]), lambda i: (j, 0, 0)),
            pl.BlockSpec((tm, LANES), lambda i: (i, 0)),
            pl.BlockSpec((tm, LANES), lambda i: (i, 0)),
        ],
        out_specs=[
            pl.BlockSpec((n_cols, tm, LANES), lambda i: (0, i, 0)),
            pl.BlockSpec((tm, kv_w), lambda i: (i, 0)),
            pl.BlockSpec((tm, kv_w), lambda i: (i, 0)),
        ],
        out_shape=[
            jax.ShapeDtypeStruct((n_cols, m, LANES), BF16),
            jax.ShapeDtypeStruct((m, kv_w), F32),
            jax.ShapeDtypeStruct((m, kv_w), F32),
        ],
        compiler_params=_params("parallel"),
        name="qkv_rope",
    )(x, norm_g, w_qkv, cos, sin)


def _block_diag(col, rolled, first):
    lane = lax.broadcasted_iota(jnp.int32, col.shape, 1)
    lo = lane < HEAD_DIM
    zero = jnp.zeros_like(col)
    if first:
        return jnp.concatenate([jnp.where(lo, col, zero), jnp.where(lo, zero, rolled)], axis=0)
    return jnp.concatenate([jnp.where(lo, rolled, zero), jnp.where(lo, zero, col)], axis=0)


def _attn_kernel(sink_ref, q_ref, kp_ref, kc_ref, vp_ref, vc_ref, x_ref, g_ref, wo_ref, o_ref,
                 att_ref, *, sq, n_kv, streaming, lag, q_pos0, k_pos0):
    i = pl.program_id(0)
    t = jnp.minimum(i, pl.num_programs(0) - 1 - lag)
    cur = lax.rem(i, 2) if lag else 0
    tq, d = x_ref.shape
    nsub = tq // sq
    nk_real = WINDOW + sq
    nk = -(-nk_real // LANES) * LANES
    assert nk == nk_real or not streaming
    n_cols = q_ref.shape[0]
    cols_per_kv = n_cols // n_kv
    n_blocks = nsub * n_kv
    blocks_per_chunk = n_blocks // (d // ATTN_OUT_COLS)

    if lag:
        @pl.when(i == 0)
        def _():
            att_ref[1] = jnp.zeros((tq, d), BF16)

    if streaming:
        k_all = jnp.concatenate([kp_ref[...], kc_ref[...]], axis=0)
        v_all = jnp.concatenate([vp_ref[...], vc_ref[...]], axis=0)

    row = lax.broadcasted_iota(jnp.int32, (sq, nk), 0)
    colk = lax.broadcasted_iota(jnp.int32, (sq, nk), 1)

    @functools.cache
    def sub_tile(s):
        if streaming:
            q0 = t * tq + s * sq
            k0 = q0 - WINDOW
            ks = k_all[s * sq:s * sq + nk]
            vs = v_all[s * sq:s * sq + nk]
        else:
            q0 = q_pos0
            k0 = k_pos0
            pad = [jnp.zeros((nk - nk_real, kp_ref.shape[2]), F32)] if nk > nk_real else []
            ks = jnp.concatenate([kp_ref[s], kc_ref[s * sq:(s + 1) * sq]] + pad, axis=0)
            vs = jnp.concatenate([vp_ref[s], vc_ref[s * sq:(s + 1) * sq]] + pad, axis=0)
        kpos = k0 + colk
        q_chunk = lax.shift_right_arithmetic(q0 + row, CHUNK_SHIFT)
        k_chunk = lax.shift_right_arithmetic(kpos, CHUNK_SHIFT)
        valid = (k_chunk >= q_chunk - LOOKBACK_CHUNKS) & (k_chunk <= q_chunk)
        if streaming:
            valid = valid & (kpos >= 0)
        if nk > nk_real:
            valid = valid & (colk < nk_real)
        return ks, vs, valid

    def scores(s, kv):
        ks, _, _ = sub_tile(s)
        c, first = divmod(kv, HEADS_PER_COL)
        kcol = ks[:, c * LANES:(c + 1) * LANES]
        bdk = _block_diag(kcol, pltpu.roll(kcol, HEAD_DIM, 1), first == 0).astype(BF16)
        qs = q_ref[kv * cols_per_kv:(kv + 1) * cols_per_kv, s * sq:(s + 1) * sq, :]
        qs = qs.reshape(cols_per_kv * sq, LANES)
        return lax.dot_general(qs, bdk, (((1,), (1,)), ((), ())), preferred_element_type=F32)

    def attend(s, kv, sc):
        _, vs, valid = sub_tile(s)
        c, first = divmod(kv, HEADS_PER_COL)
        vcol = vs[:, c * LANES:(c + 1) * LANES]
        bdv = _block_diag(vcol, pltpu.roll(vcol, HEAD_DIM, 1), first == 0).astype(BF16)
        p_rows = []
        for p in range(cols_per_kv):
            halves = []
            for e in range(HEADS_PER_COL):
                sink = sink_ref[(kv * cols_per_kv + p) * HEADS_PER_COL + e]
                blk = sc[p * sq:(p + 1) * sq, e * nk:(e + 1) * nk]
                blk = jnp.where(valid, blk, -jnp.inf)
                mx = jnp.maximum(jnp.max(blk, axis=-1, keepdims=True), sink)
                ex = jnp.exp(blk - mx)
                den = jnp.sum(ex, axis=-1, keepdims=True) + jnp.exp(sink - mx)
                halves.append((ex * (1.0 / den)).astype(BF16))
            p_rows.append(jnp.concatenate(halves, axis=1))
        pm = jnp.concatenate(p_rows, axis=0)
        ov = jnp.dot(pm, bdv, preferred_element_type=F32)
        for p in range(cols_per_kv):
            cc = kv * cols_per_kv + p
            att_ref[cur, s * sq:(s + 1) * sq, cc * LANES:(cc + 1) * LANES] = (
                ov[p * sq:(p + 1) * sq].astype(BF16))

    blocks = [(s, kv) for s in range(nsub) for kv in range(n_kv)]
    sc_next = scores(*blocks[0])
    for b, (s, kv) in enumerate(blocks):
        sc = sc_next
        if b + 1 < n_blocks:
            sc_next = scores(*blocks[b + 1])
        if lag and (b + 1) % blocks_per_chunk == 0:
            oc = ((b + 1) // blocks_per_chunk - 1) * ATTN_OUT_COLS
            o_ref[:, oc:oc + ATTN_OUT_COLS] = jnp.dot(
                att_ref[1 - cur], wo_ref[:, oc:oc + ATTN_OUT_COLS], preferred_element_type=F32)
        attend(s, kv, sc)

    if not lag:
        o_ref[...] = jnp.dot(att_ref[0], wo_ref[...], preferred_element_type=F32)
    o_ref[...] = x_ref[...] + _rms(o_ref[...], g_ref[...])


def _attention(x, q, k_prev, k, v_prev, v, sinks, norm_g, w_o, layer, j, *, tq, sq, n_kv,
               streaming):
    m, d = x.shape
    n_cols = q.shape[0]
    kv_w = k.shape[1]
    nsub = tq // sq
    n_tiles = m // tq
    lag = int(n_tiles >= ATTN_MIN_LAGGED_TILES)
    att_tile = lambda i: jnp.minimum(i, n_tiles - 1)
    out_tile = lambda i: jnp.maximum(i - lag, 0)
    if streaming:
        per = tq // WINDOW
        prev_spec = pl.BlockSpec((WINDOW, kv_w), lambda i: (jnp.maximum(att_tile(i) * per - 1, 0), 0))
    else:
        prev_spec = pl.BlockSpec((nsub, WINDOW, kv_w), lambda i: (att_tile(i), 0, 0))
    cur_spec = pl.BlockSpec((tq, kv_w), lambda i: (att_tile(i), 0))
    kern = functools.partial(_attn_kernel, sq=sq, n_kv=n_kv, streaming=streaming, lag=lag,
                             q_pos0=PAST_LEN, k_pos0=PAST_LEN - WINDOW)
    return pl.pallas_call(
        kern,
        grid=(n_tiles + lag,),
        in_specs=[
            pl.BlockSpec(memory_space=pltpu.SMEM),
            pl.BlockSpec((n_cols, tq, LANES), lambda i: (0, att_tile(i), 0)),
            prev_spec, cur_spec, prev_spec, cur_spec,
            pl.BlockSpec((tq, d), lambda i: (out_tile(i), 0)),
            pl.BlockSpec((None, None, 1, d), lambda i: (layer, 3, 0, 0)),
            pl.BlockSpec((None, d, d), lambda i: (j, 0, 0)),
        ],
        out_specs=pl.BlockSpec((tq, d), lambda i: (out_tile(i), 0)),
        out_shape=jax.ShapeDtypeStruct((m, d), F32),
        scratch_shapes=[pltpu.VMEM((1 + lag, tq, d), BF16)],
        compiler_params=_params("arbitrary"),
        name="swa_attention",
    )(sinks, q, k_prev, k, v_prev, v, x, norm_g, w_o)


def _rope_tables(pos):
    half = HEAD_DIM // 2
    inv_freq = ROPE_THETA ** (-jnp.arange(half, dtype=F32) / half)
    inv_freq_lanes = jnp.tile(inv_freq, LANES // half)
    sign_lanes = jnp.tile(jnp.concatenate([-jnp.ones(half, F32), jnp.ones(half, F32)]), HEADS_PER_COL)
    ang = pos.astype(F32)[:, None] * inv_freq_lanes[None, :]
    return jnp.cos(ang), sign_lanes[None, :] * jnp.sin(ang)


def kernel(x_prompt, x_sample, state_conv, cache_k, cache_v, norm_g, w_ffn_in, w_ffn_out,
           w_conv_in, w_conv, w_conv_out, w_qkv, w_attn_out, attn_sinks):
    batch, seq, d = x_prompt.shape
    dec_batch, dec_seq, _ = x_sample.shape
    depth = norm_g.shape[0]
    n_heads = attn_sinks.shape[1]
    n_kv = cache_k.shape[3]
    kv_w = n_kv * HEAD_DIM
    assert batch == 1 and dec_seq == CHUNK and cache_k.shape[2] == WINDOW

    g4 = norm_g.reshape(depth, norm_g.shape[1], 1, d)
    w_conv_in_b = w_conv_in.astype(BF16)
    w_conv_out_b = w_conv_out.astype(BF16)
    w_qkv_b = w_qkv.astype(BF16)
    w_attn_out_b = w_attn_out.astype(BF16)

    xp = x_prompt.reshape(seq, d)
    xs = x_sample.reshape(dec_batch * dec_seq, d)
    cos_p, sin_p = _rope_tables(jnp.arange(seq))
    cos_s, sin_s = _rope_tables(jnp.tile(PAST_LEN + jnp.arange(dec_seq), dec_batch))

    tm = ROW_TILE
    conv_p, conv_s, k_p, v_p, k_s, v_s = [], [], [], [], [], []
    for layer in range(depth):
        j = layer // 2
        def ffn_pair(xp, xs, slot, n_pre, n_post):
            ffn = functools.partial(_ffn, norm_g=g4, layer=layer, n_pre=n_pre, n_post=n_post, tm=FFN_TM)
            xs, bf16_w = ffn(xs, weights=(w_ffn_in, w_ffn_out, slot), tf=FFN_TF_F32, tile_cols=FFN_TF)
            return ffn(xp, weights=bf16_w, tf=FFN_TF), xs

        xp, xs = ffn_pair(xp, xs, 0, 0, 1)
        if layer % 2 == 0:
            mix = functools.partial(_conv_mixer, norm_g=g4, w_in=w_conv_in_b, conv_w=w_conv,
                                    w_out=w_conv_out_b, layer=layer, j=j, tm=tm)
            xp, st = mix(xp, jnp.zeros((1, 2, d), F32), seg=tm, carried=True, tn=d)
            conv_p.append(st)
            xs, st = mix(xs, state_conv[j], seg=dec_seq, carried=False, tn=CONV_TN)
            conv_s.append(st)
        else:
            proj = functools.partial(_qkv_project, norm_g=g4, w_qkv=w_qkv_b, layer=layer, j=j,
                                     tm=tm, n_heads=n_heads, n_kv=n_kv)
            att = functools.partial(_attention, sinks=attn_sinks[j], norm_g=g4, w_o=w_attn_out_b,
                                    layer=layer, j=j, n_kv=n_kv)
            q, k, v = proj(xp, cos=cos_p, sin=sin_p)
            xp = att(xp, q, k, k, v, v, tq=tm, sq=WINDOW, streaming=True)
            k_p.append(k[seq - WINDOW:].reshape(1, WINDOW, n_kv, HEAD_DIM))
            v_p.append(v[seq - WINDOW:].reshape(1, WINDOW, n_kv, HEAD_DIM))
            q, k, v = proj(xs, cos=cos_s, sin=sin_s)
            ck = cache_k[j].reshape(dec_batch, WINDOW, kv_w)
            cv = cache_v[j].reshape(dec_batch, WINDOW, kv_w)
            xs = att(xs, q, ck, k, cv, v, tq=tm, sq=dec_seq, streaming=False)
            k_new = jnp.concatenate([ck, k.reshape(dec_batch, dec_seq, kv_w)], axis=1)[:, -WINDOW:]
            v_new = jnp.concatenate([cv, v.reshape(dec_batch, dec_seq, kv_w)], axis=1)[:, -WINDOW:]
            k_s.append(k_new.reshape(dec_batch, WINDOW, n_kv, HEAD_DIM))
            v_s.append(v_new.reshape(dec_batch, WINDOW, n_kv, HEAD_DIM))
        xp, xs = ffn_pair(xp, xs, 1, 4, 5)

    return (xp.reshape(batch, seq, d), xs.reshape(dec_batch, dec_seq, d),
            jnp.stack(conv_p), jnp.stack(conv_s),
            jnp.stack(k_p), jnp.stack(v_p), jnp.stack(k_s), jnp.stack(v_s))
```

```python
import functools

import jax
import jax.numpy as jnp
from jax import lax
from jax.experimental import pallas as pl
from jax.experimental.pallas import tpu as pltpu

F32 = jnp.float32
BF16 = jnp.bfloat16

CHUNK = 64
CHUNK_SHIFT = CHUNK.bit_length() - 1
assert 1 << CHUNK_SHIFT == CHUNK
LOOKBACK_CHUNKS = 2
WINDOW = LOOKBACK_CHUNKS * CHUNK
HEAD_DIM = 64
PAST_LEN = 2048
ROPE_THETA = 10000.0
NORM_EPS = 1e-6

LANES = 128
HEADS_PER_COL = LANES // HEAD_DIM
VMEM_LIMIT = 62 * 1024 * 1024
ROW_TILE = 512
FFN_TM = 1024
FFN_ROWS = 512
FFN_TF = 512
FFN_TF_F32 = 256
CONV_TN = 512
CONV_COLS = 256
ATTN_OUT_COLS = 256
ATTN_MIN_LAGGED_TILES = 4


def _rms(x, g):
    ms = jnp.mean(x * x, axis=-1, keepdims=True)
    return x * lax.rsqrt(ms + NORM_EPS) * g


def _params(*sem):
    return pltpu.CompilerParams(dimension_semantics=sem, vmem_limit_bytes=VMEM_LIMIT)


def _ffn_kernel(x_ref, gpre_ref, gpost_ref, wg_ref, wu_ref, wo_ref, o_ref, *rest, cast_own):
    own_out, h_ref = rest[:-1], rest[-1]
    j = pl.program_id(1)
    last = pl.num_programs(1) - 1

    def step(first, final):
        wg, wu, wo = wg_ref[...], wu_ref[...], wo_ref[...]
        if cast_own:
            wg, wu, wo = wg.astype(BF16), wu.astype(BF16), wo.astype(BF16)
            for w, dst in zip((wg, wu, wo), own_out):
                dst[...] = w
        group = FFN_ROWS if first or final else x_ref.shape[0]
        for r in range(0, x_ref.shape[0], group):
            rows = slice(r, r + group)
            if first:
                h = _rms(x_ref[rows], gpre_ref[...]).astype(BF16)
                h_ref[rows] = h
            else:
                h = h_ref[rows]
            gate = jnp.dot(h, wg, preferred_element_type=F32)
            up = jnp.dot(h, wu, preferred_element_type=F32)
            act = (gate * jax.nn.sigmoid(gate) * up).astype(BF16)
            part = jnp.dot(act, wo, preferred_element_type=F32)
            if first:
                o_ref[rows] = part
            elif final:
                o_ref[rows] = x_ref[rows] + 0.5 * _rms(o_ref[rows] + part, gpost_ref[...])
            else:
                o_ref[rows] += part

    pl.when(j == 0)(functools.partial(step, True, False))
    pl.when((j > 0) & (j < last))(functools.partial(step, False, False))
    pl.when(j == last)(functools.partial(step, False, True))


def _ffn(x, norm_g, weights, layer, n_pre, n_post, *, tm, tf, tile_cols=None):
    m, d = x.shape
    cast_own = weights[0].dtype == F32
    g_spec = lambda n: pl.BlockSpec((None, None, 1, d), lambda i, j: (layer, n, 0, 0))
    out_specs = [pl.BlockSpec((tm, d), lambda i, j: (i, 0))]
    out_shape = [jax.ShapeDtypeStruct((m, d), F32)]
    if cast_own:
        w_in, w_out, slot = weights
        f = w_out.shape[2]
        nf, per_tile = f // tf, tile_cols // tf
        assert m == tm and per_tile * tf == tile_cols
        w_specs = [
            pl.BlockSpec((None, None, d, tf), lambda i, j: (layer, slot, 0, j)),
            pl.BlockSpec((None, None, d, tf), lambda i, j: (layer, slot, 0, j + nf)),
            pl.BlockSpec((None, None, tf, d), lambda i, j: (layer, slot, j, 0)),
        ]
        operands = [w_in, w_in, w_out]
        tile_spec = pl.BlockSpec((None, d, tf), lambda i, j: (j // per_tile, 0, j % per_tile))
        out_specs += [tile_spec, tile_spec, pl.BlockSpec((tf, d), lambda i, j: (j, 0))]
        tiles = jax.ShapeDtypeStruct((f // tile_cols, d, tile_cols), BF16)
        out_shape += [tiles, tiles, jax.ShapeDtypeStruct((f, d), BF16)]
    else:
        gate, up, down = operands = list(weights)
        f = down.shape[0]
        assert gate.shape[2] == tf
        tile_spec = pl.BlockSpec((None, d, tf), lambda i, j: (j, 0, 0))
        w_specs = [tile_spec, tile_spec, pl.BlockSpec((tf, d), lambda i, j: (j, 0))]
    outs = pl.pallas_call(
        functools.partial(_ffn_kernel, cast_own=cast_own),
        grid=(m // tm, f // tf),
        in_specs=[pl.BlockSpec((tm, d), lambda i, j: (i, 0)), g_spec(n_pre), g_spec(n_post)] + w_specs,
        out_specs=out_specs,
        out_shape=out_shape,
        scratch_shapes=[pltpu.VMEM((tm, d), BF16)],
        compiler_params=_params("parallel", "arbitrary"),
        name="ffn",
    )(x, norm_g, norm_g, *operands)
    return (outs[0], tuple(outs[1:])) if cast_own else outs[0]


def _conv_kernel(x_ref, gpre_ref, gpost_ref, wb_ref, wc_ref, wh_ref, cw_ref, wo_ref, st_ref,
                 o_ref, ns_ref, h_ref, carry_ref, *, seg, carried):
    i = pl.program_id(0)
    n = pl.program_id(1)
    last = pl.num_programs(1) - 1
    tm = x_ref.shape[0]
    tn = wb_ref.shape[1]
    tc = CONV_COLS
    cols = pl.ds(pl.multiple_of(n * tn, tn), tn)

    if carried:
        @pl.when(i == 0)
        def _():
            carry_ref[:, cols] = st_ref[0, :, cols]

    def gated_conv(c0, b_gate, c_gate, hh):
        gcols = pl.ds(pl.multiple_of(n * tn + c0, tc), tc)
        cw = cw_ref[:, c0:c0 + tc]
        w0, w1, w2 = cw[0:1], cw[1:2], cw[2:3]
        row = lax.broadcasted_iota(jnp.int32, (seg, tc), 0)
        u = c_gate * hh
        convs = []
        for s in range(tm // seg):
            us = u[s * seg:(s + 1) * seg]
            prev = carry_ref[:, gcols] if carried else st_ref[s, :, c0:c0 + tc]
            p1 = jnp.where(row == 0, prev[1:2], pltpu.roll(us, 1, 0))
            p2 = jnp.where(row == 0, prev[0:1], jnp.where(row == 1, prev[1:2], pltpu.roll(us, 2, 0)))
            convs.append(p2 * w0 + p1 * w1 + us * w2)
            if carried:
                carry_ref[:, gcols] = us[seg - 2:seg]
                ns_ref[0, :, gcols] = us[seg - 2:seg]
            else:
                ns_ref[s, :, c0:c0 + tc] = us[seg - 2:seg]
        conv = convs[0] if len(convs) == 1 else jnp.concatenate(convs, axis=0)
        return (b_gate * conv).astype(BF16)

    def out_proj(assign, final, c0, z):
        part = jnp.dot(z, wo_ref[c0:c0 + tc, :], preferred_element_type=F32)
        if assign:
            o_ref[...] = part
        elif final:
            o_ref[...] = x_ref[...] + _rms(o_ref[...] + part, gpost_ref[...])
        else:
            o_ref[...] += part

    def step(first, final):
        if first:
            h = _rms(x_ref[...], gpre_ref[...]).astype(BF16)
            h_ref[...] = h
        else:
            h = h_ref[...]
        pending = None
        for c0 in range(0, tn, tc):
            b_gate = jnp.dot(h, wb_ref[:, c0:c0 + tc], preferred_element_type=F32)
            c_gate = jnp.dot(h, wc_ref[:, c0:c0 + tc], preferred_element_type=F32)
            hh = jnp.dot(h, wh_ref[:, c0:c0 + tc], preferred_element_type=F32)
            if pending is not None:
                out_proj(first and pending[0] == 0, False, *pending)
            pending = (c0, gated_conv(c0, b_gate, c_gate, hh))
        out_proj(first and pending[0] == 0, final, *pending)

    if tn == o_ref.shape[1]:
        step(True, True)
    else:
        pl.when(n == 0)(functools.partial(step, True, False))
        pl.when((n > 0) & (n < last))(functools.partial(step, False, False))
        pl.when(n == last)(functools.partial(step, False, True))


def _conv_mixer(x, state, norm_g, w_in, conv_w, w_out, layer, j, *, tm, tn, seg, carried):
    m, d = x.shape
    nn = d // tn
    nseg = tm // seg
    g_spec = lambda n: pl.BlockSpec((None, None, 1, d), lambda i, c: (layer, n, 0, 0))
    once = dict(pipeline_mode=pl.Buffered(1)) if nn == 1 else {}
    w_spec = lambda sec: pl.BlockSpec((None, d, tn), lambda i, c: (j, 0, c + sec * nn), **once)
    if carried:
        st_spec = pl.BlockSpec((1, 2, d), lambda i, c: (0, 0, 0))
    else:
        st_spec = pl.BlockSpec((nseg, 2, tn), lambda i, c: (i, 0, c))
    return pl.pallas_call(
        functools.partial(_conv_kernel, seg=seg, carried=carried),
        grid=(m // tm, nn),
        in_specs=[
            pl.BlockSpec((tm, d), lambda i, c: (i, 0)),
            g_spec(2),
            g_spec(3),
            w_spec(0),
            w_spec(1),
            w_spec(2),
            pl.BlockSpec((None, 3, tn), lambda i, c: (j, 0, c)),
            pl.BlockSpec((None, tn, d), lambda i, c: (j, c, 0), **once),
            st_spec,
        ],
        out_specs=[pl.BlockSpec((tm, d), lambda i, c: (i, 0)), st_spec],
        out_shape=[
            jax.ShapeDtypeStruct((m, d), F32),
            jax.ShapeDtypeStruct(state.shape, F32),
        ],
        scratch_shapes=[pltpu.VMEM((tm, d), BF16), pltpu.VMEM((2, d), F32)],
        compiler_params=_params("arbitrary", "arbitrary"),
        name="conv_mixer",
    )(x, norm_g, norm_g, w_in, w_in, w_in, conv_w, w_out, state)


def _rope_col(xc, cos, sin_signed):
    lane = lax.broadcasted_iota(jnp.int32, xc.shape, 1)
    first_half = (lane % HEAD_DIM) < HEAD_DIM // 2
    swapped = jnp.where(first_half,
                        pltpu.roll(xc, LANES - HEAD_DIM // 2, 1),
                        pltpu.roll(xc, HEAD_DIM // 2, 1))
    return xc * cos + swapped * sin_signed


def _qkv_kernel(x_ref, g_ref, w_ref, cos_ref, sin_ref, q_ref, k_ref, v_ref, *, chunk):
    n_q = q_ref.shape[0] * LANES
    kv_w = k_ref.shape[1]
    h = _rms(x_ref[...], g_ref[...]).astype(BF16)
    cos = cos_ref[...]
    sin = sin_ref[...]
    for c0 in range(0, n_q, chunk):
        y = jnp.dot(h, w_ref[:, c0:c0 + chunk], preferred_element_type=F32)
        for c in range(chunk // LANES):
            qc = _rope_col(y[:, c * LANES:(c + 1) * LANES], cos, sin) * (HEAD_DIM ** -0.5)
            q_ref[c0 // LANES + c] = qc.astype(BF16)
    y = jnp.dot(h, w_ref[:, n_q:], preferred_element_type=F32)
    for c in range(kv_w // LANES):
        k_ref[:, c * LANES:(c + 1) * LANES] = _rope_col(y[:, c * LANES:(c + 1) * LANES], cos, sin)
    v_ref[...] = y[:, kv_w:]


def _qkv_project(x, norm_g, w_qkv, cos, sin, layer, j, *, tm, n_heads, n_kv):
    m, d = x.shape
    kv_w = n_kv * HEAD_DIM
    n_cols = n_heads // HEADS_PER_COL
    return pl.pallas_call(
        functools.partial(_qkv_kernel, chunk=2 * kv_w),
        grid=(m // tm,),
        in_specs=[
            pl.BlockSpec((tm, d), lambda i: (i, 0)),
            pl.BlockSpec((None, None, 1, d), lambda i: (layer, 2, 0, 0)),
            pl.BlockSpec((None, d, w_qkv.shape[2]), lambda i: (j, 0, 0)),
            pl.BlockSpec((tm, LANES), lambda i: (i, 0)),
            pl.BlockSpec((tm, LANES), lambda i: (i, 0)),
        ],
        out_specs=[
            pl.BlockSpec((n_cols, tm, LANES), lambda i: (0, i, 0)),
            pl.BlockSpec((tm, kv_w), lambda i: (i, 0)),
            pl.BlockSpec((tm, kv_w), lambda i: (i, 0)),
        ],
        out_shape=[
            jax.ShapeDtypeStruct((n_cols, m, LANES), BF16),
            jax.ShapeDtypeStruct((m, kv_w), F32),
            jax.ShapeDtypeStruct((m, kv_w), F32),
        ],
        compiler_params=_params("parallel"),
        name="qkv_rope",
    )(x, norm_g, w_qkv, cos, sin)


def _block_diag(col, rolled, first):
    lane = lax.broadcasted_iota(jnp.int32, col.shape, 1)
    lo = lane < HEAD_DIM
    zero = jnp.zeros_like(col)
    if first:
        return jnp.concatenate([jnp.where(lo, col, zero), jnp.where(lo, zero, rolled)], axis=0)
    return jnp.concatenate([jnp.where(lo, rolled, zero), jnp.where(lo, zero, col)], axis=0)


def _attn_kernel(sink_ref, q_ref, kp_ref, kc_ref, vp_ref, vc_ref, x_ref, g_ref, wo_ref, o_ref,
                 att_ref, *, sq, n_kv, streaming, lag, q_pos0, k_pos0):
    i = pl.program_id(0)
    t = jnp.minimum(i, pl.num_programs(0) - 1 - lag)
    cur = lax.rem(i, 2) if lag else 0
    tq, d = x_ref.shape
    nsub = tq // sq
    nk_real = WINDOW + sq
    nk = -(-nk_real // LANES) * LANES
    assert nk == nk_real or not streaming
    n_cols = q_ref.shape[0]
    cols_per_kv = n_cols // n_kv
    n_blocks = nsub * n_kv
    blocks_per_chunk = n_blocks // (d // ATTN_OUT_COLS)

    if lag:
        @pl.when(i == 0)
        def _():
            att_ref[1] = jnp.zeros((tq, d), BF16)

    if streaming:
        k_all = jnp.concatenate([kp_ref[...], kc_ref[...]], axis=0)
        v_all = jnp.concatenate([vp_ref[...], vc_ref[...]], axis=0)

    row = lax.broadcasted_iota(jnp.int32, (sq, nk), 0)
    colk = lax.broadcasted_iota(jnp.int32, (sq, nk), 1)

    @functools.cache
    def sub_tile(s):
        if streaming:
            q0 = t * tq + s * sq
            k0 = q0 - WINDOW
            ks = k_all[s * sq:s * sq + nk]
            vs = v_all[s * sq:s * sq + nk]
        else:
            q0 = q_pos0
            k0 = k_pos0
            pad = [jnp.zeros((nk - nk_real, kp_ref.shape[2]), F32)] if nk > nk_real else []
            ks = jnp.concatenate([kp_ref[s], kc_ref[s * sq:(s + 1) * sq]] + pad, axis=0)
            vs = jnp.concatenate([vp_ref[s], vc_ref[s * sq:(s + 1) * sq]] + pad, axis=0)
        kpos = k0 + colk
        q_chunk = lax.shift_right_arithmetic(q0 + row, CHUNK_SHIFT)
        k_chunk = lax.shift_right_arithmetic(kpos, CHUNK_SHIFT)
        valid = (k_chunk >= q_chunk - LOOKBACK_CHUNKS) & (k_chunk <= q_chunk)
        if streaming:
            valid = valid & (kpos >= 0)
        if nk > nk_real:
            valid = valid & (colk < nk_real)
        return ks, vs, valid

    def scores(s, kv):
        ks, _, _ = sub_tile(s)
        c, first = divmod(kv, HEADS_PER_COL)
        kcol = ks[:, c * LANES:(c + 1) * LANES]
        bdk = _block_diag(kcol, pltpu.roll(kcol, HEAD_DIM, 1), first == 0).astype(BF16)
        qs = q_ref[kv * cols_per_kv:(kv + 1) * cols_per_kv, s * sq:(s + 1) * sq, :]
        qs = qs.reshape(cols_per_kv * sq, LANES)
        return lax.dot_general(qs, bdk, (((1,), (1,)), ((), ())), preferred_element_type=F32)

    def attend(s, kv, sc):
        _, vs, valid = sub_tile(s)
        c, first = divmod(kv, HEADS_PER_COL)
        vcol = vs[:, c * LANES:(c + 1) * LANES]
        bdv = _block_diag(vcol, pltpu.roll(vcol, HEAD_DIM, 1), first == 0).astype(BF16)
        p_rows = []
        for p in range(cols_per_kv):
            halves = []
            for e in range(HEADS_PER_COL):
                sink = sink_ref[(kv * cols_per_kv + p) * HEADS_PER_COL + e]
                blk = sc[p * sq:(p + 1) * sq, e * nk:(e + 1) * nk]
                blk = jnp.where(valid, blk, -jnp.inf)
                mx = jnp.maximum(jnp.max(blk, axis=-1, keepdims=True), sink)
                ex = jnp.exp(blk - mx)
                den = jnp.sum(ex, axis=-1, keepdims=True) + jnp.exp(sink - mx)
                halves.append((ex * (1.0 / den)).astype(BF16))
            p_rows.append(jnp.concatenate(halves, axis=1))
        pm = jnp.concatenate(p_rows, axis=0)
        ov = jnp.dot(pm, bdv, preferred_element_type=F32)
        for p in range(cols_per_kv):
            cc = kv * cols_per_kv + p
            att_ref[cur, s * sq:(s + 1) * sq, cc * LANES:(cc + 1) * LANES] = (
                ov[p * sq:(p + 1) * sq].astype(BF16))

    blocks = [(s, kv) for s in range(nsub) for kv in range(n_kv)]
    sc_next = scores(*blocks[0])
    for b, (s, kv) in enumerate(blocks):
        sc = sc_next
        if b + 1 < n_blocks:
            sc_next = scores(*blocks[b + 1])
        if lag and (b + 1) % blocks_per_chunk == 0:
            oc = ((b + 1) // blocks_per_chunk - 1) * ATTN_OUT_COLS
            o_ref[:, oc:oc + ATTN_OUT_COLS] = jnp.dot(
                att_ref[1 - cur], wo_ref[:, oc:oc + ATTN_OUT_COLS], preferred_element_type=F32)
        attend(s, kv, sc)

    if not lag:
        o_ref[...] = jnp.dot(att_ref[0], wo_ref[...], preferred_element_type=F32)
    o_ref[...] = x_ref[...] + _rms(o_ref[...], g_ref[...])


def _attention(x, q, k_prev, k, v_prev, v, sinks, norm_g, w_o, layer, j, *, tq, sq, n_kv,
               streaming):
    m, d = x.shape
    n_cols = q.shape[0]
    kv_w = k.shape[1]
    nsub = tq // sq
    n_tiles = m // tq
    lag = int(n_tiles >= ATTN_MIN_LAGGED_TILES)
    att_tile = lambda i: jnp.minimum(i, n_tiles - 1)
    out_tile = lambda i: jnp.maximum(i - lag, 0)
    if streaming:
        per = tq // WINDOW
        prev_spec = pl.BlockSpec((WINDOW, kv_w), lambda i: (jnp.maximum(att_tile(i) * per - 1, 0), 0))
    else:
        prev_spec = pl.BlockSpec((nsub, WINDOW, kv_w), lambda i: (att_tile(i), 0, 0))
    cur_spec = pl.BlockSpec((tq, kv_w), lambda i: (att_tile(i), 0))
    kern = functools.partial(_attn_kernel, sq=sq, n_kv=n_kv, streaming=streaming, lag=lag,
                             q_pos0=PAST_LEN, k_pos0=PAST_LEN - WINDOW)
    return pl.pallas_call(
        kern,
        grid=(n_tiles + lag,),
        in_specs=[
            pl.BlockSpec(memory_space=pltpu.SMEM),
            pl.BlockSpec((n_cols, tq, LANES), lambda i: (0, att_tile(i), 0)),
            prev_spec, cur_spec, prev_spec, cur_spec,
            pl.BlockSpec((tq, d), lambda i: (out_tile(i), 0)),
            pl.BlockSpec((None, None, 1, d), lambda i: (layer, 3, 0, 0)),
            pl.BlockSpec((None, d, d), lambda i: (j, 0, 0)),
        ],
        out_specs=pl.BlockSpec((tq, d), lambda i: (out_tile(i), 0)),
        out_shape=jax.ShapeDtypeStruct((m, d), F32),
        scratch_shapes=[pltpu.VMEM((1 + lag, tq, d), BF16)],
        compiler_params=_params("arbitrary"),
        name="swa_attention",
    )(sinks, q, k_prev, k, v_prev, v, x, norm_g, w_o)


def _rope_tables(pos):
    half = HEAD_DIM // 2
    inv_freq = ROPE_THETA ** (-jnp.arange(half, dtype=F32) / half)
    inv_freq_lanes = jnp.tile(inv_freq, LANES // half)
    sign_lanes = jnp.tile(jnp.concatenate([-jnp.ones(half, F32), jnp.ones(half, F32)]), HEADS_PER_COL)
    ang = pos.astype(F32)[:, None] * inv_freq_lanes[None, :]
    return jnp.cos(ang), sign_lanes[None, :] * jnp.sin(ang)


def kernel(x_prompt, x_sample, state_conv, cache_k, cache_v, norm_g, w_ffn_in, w_ffn_out,
           w_conv_in, w_conv, w_conv_out, w_qkv, w_attn_out, attn_sinks):
    batch, seq, d = x_prompt.shape
    dec_batch, dec_seq, _ = x_sample.shape
    depth = norm_g.shape[0]
    n_heads = attn_sinks.shape[1]
    n_kv = cache_k.shape[3]
    kv_w = n_kv * HEAD_DIM
    assert batch == 1 and dec_seq == CHUNK and cache_k.shape[2] == WINDOW

    g4 = norm_g.reshape(depth, norm_g.shape[1], 1, d)
    w_conv_in_b = w_conv_in.astype(BF16)
    w_conv_out_b = w_conv_out.astype(BF16)
    w_qkv_b = w_qkv.astype(BF16)
    w_attn_out_b = w_attn_out.astype(BF16)

    xp = x_prompt.reshape(seq, d)
    xs = x_sample.reshape(dec_batch * dec_seq, d)
    cos_p, sin_p = _rope_tables(jnp.arange(seq))
    cos_s, sin_s = _rope_tables(jnp.tile(PAST_LEN + jnp.arange(dec_seq), dec_batch))

    tm = ROW_TILE
    conv_p, conv_s, k_p, v_p, k_s, v_s = [], [], [], [], [], []
    for layer in range(depth):
        j = layer // 2
        def ffn_pair(xp, xs, slot, n_pre, n_post):
            ffn = functools.partial(_ffn, norm_g=g4, layer=layer, n_pre=n_pre, n_post=n_post, tm=FFN_TM)
            xs, bf16_w = ffn(xs, weights=(w_ffn_in, w_ffn_out, slot), tf=FFN_TF_F32, tile_cols=FFN_TF)
            return ffn(xp, weights=bf16_w, tf=FFN_TF), xs

        xp, xs = ffn_pair(xp, xs, 0, 0, 1)
        if layer % 2 == 0:
            mix = functools.partial(_conv_mixer, norm_g=g4, w_in=w_conv_in_b, conv_w=w_conv,
                                    w_out=w_conv_out_b, layer=layer, j=j, tm=tm)
            xp, st = mix(xp, jnp.zeros((1, 2, d), F32), seg=tm, carried=True, tn=d)
            conv_p.append(st)
            xs, st = mix(xs, state_conv[j], seg=dec_seq, carried=False, tn=CONV_TN)
            conv_s.append(st)
        else:
            proj = functools.partial(_qkv_project, norm_g=g4, w_qkv=w_qkv_b, layer=layer, j=j,
                                     tm=tm, n_heads=n_heads, n_kv=n_kv)
            att = functools.partial(_attention, sinks=attn_sinks[j], norm_g=g4, w_o=w_attn_out_b,
                                    layer=layer, j=j, n_kv=n_kv)
            q, k, v = proj(xp, cos=cos_p, sin=sin_p)
            xp = att(xp, q, k, k, v, v, tq=tm, sq=WINDOW, streaming=True)
            k_p.append(k[seq - WINDOW:].reshape(1, WINDOW, n_kv, HEAD_DIM))
            v_p.append(v[seq - WINDOW:].reshape(1, WINDOW, n_kv, HEAD_DIM))
            q, k, v = proj(xs, cos=cos_s, sin=sin_s)
            ck = cache_k[j].reshape(dec_batch, WINDOW, kv_w)
            cv = cache_v[j].reshape(dec_batch, WINDOW, kv_w)
            xs = att(xs, q, ck, k, cv, v, tq=tm, sq=dec_seq, streaming=False)
            k_new = jnp.concatenate([ck, k.reshape(dec_batch, dec_seq, kv_w)], axis=1)[:, -WINDOW:]
            v_new = jnp.concatenate([cv, v.reshape(dec_batch, dec_seq, kv_w)], axis=1)[:, -WINDOW:]
            k_s.append(k_new.reshape(dec_batch, WINDOW, n_kv, HEAD_DIM))
            v_s.append(v_new.reshape(dec_batch, WINDOW, n_kv, HEAD_DIM))
        xp, xs = ffn_pair(xp, xs, 1, 4, 5)

    return (xp.reshape(batch, seq, d), xs.reshape(dec_batch, dec_seq, d),
            jnp.stack(conv_p), jnp.stack(conv_s),
            jnp.stack(k_p), jnp.stack(v_p), jnp.stack(k_s), jnp.stack(v_s))
```

```python
import functools

import jax
import jax.numpy as jnp
from jax import lax
from jax.experimental import pallas as pl
from jax.experimental.pallas import tpu as pltpu

F32 = jnp.float32
BF16 = jnp.bfloat16

CHUNK = 64
CHUNK_SHIFT = CHUNK.bit_length() - 1
assert 1 << CHUNK_SHIFT == CHUNK
LOOKBACK_CHUNKS = 2
WINDOW = LOOKBACK_CHUNKS * CHUNK
HEAD_DIM = 64
PAST_LEN = 2048
ROPE_THETA = 10000.0
NORM_EPS = 1e-6

LANES = 128
HEADS_PER_COL = LANES // HEAD_DIM
VMEM_LIMIT = 62 * 1024 * 1024
ROW_TILE = 512
FFN_TM = 1024
FFN_ROWS = 512
FFN_TF = 512
FFN_TF_F32 = 256
CONV_TN = 512
CONV_COLS = 256
ATTN_OUT_COLS = 256
ATTN_MIN_LAGGED_TILES = 4


def _rms(x, g):
    ms = jnp.mean(x * x, axis=-1, keepdims=True)
    return x * lax.rsqrt(ms + NORM_EPS) * g


def _params(*sem):
    return pltpu.CompilerParams(dimension_semantics=sem, vmem_limit_bytes=VMEM_LIMIT)


def _ffn_kernel(x_ref, gpre_ref, gpost_ref, wg_ref, wu_ref, wo_ref, o_ref, *rest, cast_own):
    own_out, h_ref = rest[:-1], rest[-1]
    j = pl.program_id(1)
    last = pl.num_programs(1) - 1

    def step(first, final):
        wg, wu, wo = wg_ref[...], wu_ref[...], wo_ref[...]
        if cast_own:
            wg, wu, wo = wg.astype(BF16), wu.astype(BF16), wo.astype(BF16)
            for w, dst in zip((wg, wu, wo), own_out):
                dst[...] = w
        group = FFN_ROWS if first or final else x_ref.shape[0]
        for r in range(0, x_ref.shape[0], group):
            rows = slice(r, r + group)
            if first:
                h = _rms(x_ref[rows], gpre_ref[...]).astype(BF16)
                h_ref[rows] = h
            else:
                h = h_ref[rows]
            gate = jnp.dot(h, wg, preferred_element_type=F32)
            up = jnp.dot(h, wu, preferred_element_type=F32)
            act = (gate * jax.nn.sigmoid(gate) * up).astype(BF16)
            part = jnp.dot(act, wo, preferred_element_type=F32)
            if first:
                o_ref[rows] = part
            elif final:
                o_ref[rows] = x_ref[rows] + 0.5 * _rms(o_ref[rows] + part, gpost_ref[...])
            else:
                o_ref[rows] += part

    pl.when(j == 0)(functools.partial(step, True, False))
    pl.when((j > 0) & (j < last))(functools.partial(step, False, False))
    pl.when(j == last)(functools.partial(step, False, True))


def _ffn(x, norm_g, weights, layer, n_pre, n_post, *, tm, tf, tile_cols=None):
    m, d = x.shape
    cast_own = weights[0].dtype == F32
    g_spec = lambda n: pl.BlockSpec((None, None, 1, d), lambda i, j: (layer, n, 0, 0))
    out_specs = [pl.BlockSpec((tm, d), lambda i, j: (i, 0))]
    out_shape = [jax.ShapeDtypeStruct((m, d), F32)]
    if cast_own:
        w_in, w_out, slot = weights
        f = w_out.shape[2]
        nf, per_tile = f // tf, tile_cols // tf
        assert m == tm and per_tile * tf == tile_cols
        w_specs = [
            pl.BlockSpec((None, None, d, tf), lambda i, j: (layer, slot, 0, j)),
            pl.BlockSpec((None, None, d, tf), lambda i, j: (layer, slot, 0, j + nf)),
            pl.BlockSpec((None, None, tf, d), lambda i, j: (layer, slot, j, 0)),
        ]
        operands = [w_in, w_in, w_out]
        tile_spec = pl.BlockSpec((None, d, tf), lambda i, j: (j // per_tile, 0, j % per_tile))
        out_specs += [tile_spec, tile_spec, pl.BlockSpec((tf, d), lambda i, j: (j, 0))]
        tiles = jax.ShapeDtypeStruct((f // tile_cols, d, tile_cols), BF16)
        out_shape += [tiles, tiles, jax.ShapeDtypeStruct((f, d), BF16)]
    else:
        gate, up, down = operands = list(weights)
        f = down.shape[0]
        assert gate.shape[2] == tf
        tile_spec = pl.BlockSpec((None, d, tf), lambda i, j: (j, 0, 0))
        w_specs = [tile_spec, tile_spec, pl.BlockSpec((tf, d), lambda i, j: (j, 0))]
    outs = pl.pallas_call(
        functools.partial(_ffn_kernel, cast_own=cast_own),
        grid=(m // tm, f // tf),
        in_specs=[pl.BlockSpec((tm, d), lambda i, j: (i, 0)), g_spec(n_pre), g_spec(n_post)] + w_specs,
        out_specs=out_specs,
        out_shape=out_shape,
        scratch_shapes=[pltpu.VMEM((tm, d), BF16)],
        compiler_params=_params("parallel", "arbitrary"),
        name="ffn",
    )(x, norm_g, norm_g, *operands)
    return (outs[0], tuple(outs[1:])) if cast_own else outs[0]


def _conv_kernel(x_ref, gpre_ref, gpost_ref, wb_ref, wc_ref, wh_ref, cw_ref, wo_ref, st_ref,
                 o_ref, ns_ref, h_ref, carry_ref, *, seg, carried):
    i = pl.program_id(0)
    n = pl.program_id(1)
    last = pl.num_programs(1) - 1
    tm = x_ref.shape[0]
    tn = wb_ref.shape[1]
    tc = CONV_COLS
    cols = pl.ds(pl.multiple_of(n * tn, tn), tn)

    if carried:
        @pl.when(i == 0)
        def _():
            carry_ref[:, cols] = st_ref[0, :, cols]

    def gated_conv(c0, b_gate, c_gate, hh):
        gcols = pl.ds(pl.multiple_of(n * tn + c0, tc), tc)
        cw = cw_ref[:, c0:c0 + tc]
        w0, w1, w2 = cw[0:1], cw[1:2], cw[2:3]
        row = lax.broadcasted_iota(jnp.int32, (seg, tc), 0)
        u = c_gate * hh
        convs = []
        for s in range(tm // seg):
            us = u[s * seg:(s + 1) * seg]
            prev = carry_ref[:, gcols] if carried else st_ref[s, :, c0:c0 + tc]
            p1 = jnp.where(row == 0, prev[1:2], pltpu.roll(us, 1, 0))
            p2 = jnp.where(row == 0, prev[0:1], jnp.where(row == 1, prev[1:2], pltpu.roll(us, 2, 0)))
            convs.append(p2 * w0 + p1 * w1 + us * w2)
            if carried:
                carry_ref[:, gcols] = us[seg - 2:seg]
                ns_ref[0, :, gcols] = us[seg - 2:seg]
            else:
                ns_ref[s, :, c0:c0 + tc] = us[seg - 2:seg]
        conv = convs[0] if len(convs) == 1 else jnp.concatenate(convs, axis=0)
        return (b_gate * conv).astype(BF16)

    def out_proj(assign, final, c0, z):
        part = jnp.dot(z, wo_ref[c0:c0 + tc, :], preferred_element_type=F32)
        if assign:
            o_ref[...] = part
        elif final:
            o_ref[...] = x_ref[...] + _rms(o_ref[...] + part, gpost_ref[...])
        else:
            o_ref[...] += part

    def step(first, final):
        if first:
            h = _rms(x_ref[...], gpre_ref[...]).astype(BF16)
            h_ref[...] = h
        else:
            h = h_ref[...]
        pending = None
        for c0 in range(0, tn, tc):
            b_gate = jnp.dot(h, wb_ref[:, c0:c0 + tc], preferred_element_type=F32)
            c_gate = jnp.dot(h, wc_ref[:, c0:c0 + tc], preferred_element_type=F32)
            hh = jnp.dot(h, wh_ref[:, c0:c0 + tc], preferred_element_type=F32)
            if pending is not None:
                out_proj(first and pending[0] == 0, False, *pending)
            pending = (c0, gated_conv(c0, b_gate, c_gate, hh))
        out_proj(first and pending[0] == 0, final, *pending)

    if tn == o_ref.shape[1]:
        step(True, True)
    else:
        pl.when(n == 0)(functools.partial(step, True, False))
        pl.when((n > 0) & (n < last))(functools.partial(step, False, False))
        pl.when(n == last)(functools.partial(step, False, True))


def _conv_mixer(x, state, norm_g, w_in, conv_w, w_out, layer, j, *, tm, tn, seg, carried):
    m, d = x.shape
    nn = d // tn
    nseg = tm // seg
    g_spec = lambda n: pl.BlockSpec((None, None, 1, d), lambda i, c: (layer, n, 0, 0))
    once = dict(pipeline_mode=pl.Buffered(1)) if nn == 1 else {}
    w_spec = lambda sec: pl.BlockSpec((None, d, tn), lambda i, c: (j, 0, c + sec * nn), **once)
    if carried:
        st_spec = pl.BlockSpec((1, 2, d), lambda i, c: (0, 0, 0))
    else:
        st_spec = pl.BlockSpec((nseg, 2, tn), lambda i, c: (i, 0, c))
    return pl.pallas_call(
        functools.partial(_conv_kernel, seg=seg, carried=carried),
        grid=(m // tm, nn),
        in_specs=[
            pl.BlockSpec((tm, d), lambda i, c: (i, 0)),
            g_spec(2),
            g_spec(3),
            w_spec(0),
            w_spec(1),
            w_spec(2),
            pl.BlockSpec((None, 3, tn), lambda i, c: (j, 0, c)),
            pl.BlockSpec((None, tn, d), lambda i, c: (j, c, 0), **once),
            st_spec,
        ],
        out_specs=[pl.BlockSpec((tm, d), lambda i, c: (i, 0)), st_spec],
        out_shape=[
            jax.ShapeDtypeStruct((m, d), F32),
            jax.ShapeDtypeStruct(state.shape, F32),
        ],
        scratch_shapes=[pltpu.VMEM((tm, d), BF16), pltpu.VMEM((2, d), F32)],
        compiler_params=_params("arbitrary", "arbitrary"),
        name="conv_mixer",
    )(x, norm_g, norm_g, w_in, w_in, w_in, conv_w, w_out, state)


def _rope_col(xc, cos, sin_signed):
    lane = lax.broadcasted_iota(jnp.int32, xc.shape, 1)
    first_half = (lane % HEAD_DIM) < HEAD_DIM // 2
    swapped = jnp.where(first_half,
                        pltpu.roll(xc, LANES - HEAD_DIM // 2, 1),
                        pltpu.roll(xc, HEAD_DIM // 2, 1))
    return xc * cos + swapped * sin_signed


def _qkv_kernel(x_ref, g_ref, w_ref, cos_ref, sin_ref, q_ref, k_ref, v_ref, *, chunk):
    n_q = q_ref.shape[0] * LANES
    kv_w = k_ref.shape[1]
    h = _rms(x_ref[...], g_ref[...]).astype(BF16)
    cos = cos_ref[...]
    sin = sin_ref[...]
    for c0 in range(0, n_q, chunk):
        y = jnp.dot(h, w_ref[:, c0:c0 + chunk], preferred_element_type=F32)
        for c in range(chunk // LANES):
            qc = _rope_col(y[:, c * LANES:(c + 1) * LANES], cos, sin) * (HEAD_DIM ** -0.5)
            q_ref[c0 // LANES + c] = qc.astype(BF16)
    y = jnp.dot(h, w_ref[:, n_q:], preferred_element_type=F32)
    for c in range(kv_w // LANES):
        k_ref[:, c * LANES:(c + 1) * LANES] = _rope_col(y[:, c * LANES:(c + 1) * LANES], cos, sin)
    v_ref[...] = y[:, kv_w:]


def _qkv_project(x, norm_g, w_qkv, cos, sin, layer, j, *, tm, n_heads, n_kv):
    m, d = x.shape
    kv_w = n_kv * HEAD_DIM
    n_cols = n_heads // HEADS_PER_COL
    return pl.pallas_call(
        functools.partial(_qkv_kernel, chunk=2 * kv_w),
        grid=(m // tm,),
        in_specs=[
            pl.BlockSpec((tm, d), lambda i: (i, 0)),
            pl.BlockSpec((None, None, 1, d), lambda i: (layer, 2, 0, 0)),
            pl.BlockSpec((None, d, w_qkv.shape[2]), lambda i: (j, 0, 0)),
            pl.BlockSpec((tm, LANES), lambda i: (i, 0)),
            pl.BlockSpec((tm, LANES), lambda i: (i, 0)),
        ],
        out_specs=[
            pl.BlockSpec((n_cols, tm, LANES), lambda i: (0, i, 0)),
            pl.BlockSpec((tm, kv_w), lambda i: (i, 0)),
            pl.BlockSpec((tm, kv_w), lambda i: (i, 0)),
        ],
        out_shape=[
            jax.ShapeDtypeStruct((n_cols, m, LANES), BF16),
            jax.ShapeDtypeStruct((m, kv_w), F32),
            jax.ShapeDtypeStruct((m, kv_w), F32),
        ],
        compiler_params=_params("parallel"),
        name="qkv_rope",
    )(x, norm_g, w_qkv, cos, sin)


def _block_diag(col, rolled, first):
    lane = lax.broadcasted_iota(jnp.int32, col.shape, 1)
    lo = lane < HEAD_DIM
    zero = jnp.zeros_like(col)
    if first:
        return jnp.concatenate([jnp.where(lo, col, zero), jnp.where(lo, zero, rolled)], axis=0)
    return jnp.concatenate([jnp.where(lo, rolled, zero), jnp.where(lo, zero, col)], axis=0)


def _attn_kernel(sink_ref, q_ref, kp_ref, kc_ref, vp_ref, vc_ref, x_ref, g_ref, wo_ref, o_ref,
                 att_ref, *, sq, n_kv, streaming, lag, q_pos0, k_pos0):
    i = pl.program_id(0)
    t = jnp.minimum(i, pl.num_programs(0) - 1 - lag)
    cur = lax.rem(i, 2) if lag else 0
    tq, d = x_ref.shape
    nsub = tq // sq
    nk_real = WINDOW + sq
    nk = -(-nk_real // LANES) * LANES
    assert nk == nk_real or not streaming
    n_cols = q_ref.shape[0]
    cols_per_kv = n_cols // n_kv
    n_blocks = nsub * n_kv
    blocks_per_chunk = n_blocks // (d // ATTN_OUT_COLS)

    if lag:
        @pl.when(i == 0)
        def _():
            att_ref[1] = jnp.zeros((tq, d), BF16)

    if streaming:
        k_all = jnp.concatenate([kp_ref[...], kc_ref[...]], axis=0)
        v_all = jnp.concatenate([vp_ref[...], vc_ref[...]], axis=0)

    row = lax.broadcasted_iota(jnp.int32, (sq, nk), 0)
    colk = lax.broadcasted_iota(jnp.int32, (sq, nk), 1)

    @functools.cache
    def sub_tile(s):
        if streaming:
            q0 = t * tq + s * sq
            k0 = q0 - WINDOW
            ks = k_all[s * sq:s * sq + nk]
            vs = v_all[s * sq:s * sq + nk]
        else:
            q0 = q_pos0
            k0 = k_pos0
            pad = [jnp.zeros((nk - nk_real, kp_ref.shape[2]), F32)] if nk > nk_real else []
            ks = jnp.concatenate([kp_ref[s], kc_ref[s * sq:(s + 1) * sq]] + pad, axis=0)
            vs = jnp.concatenate([vp_ref[s], vc_ref[s * sq:(s + 1) * sq]] + pad, axis=0)
        kpos = k0 + colk
        q_chunk = lax.shift_right_arithmetic(q0 + row, CHUNK_SHIFT)
        k_chunk = lax.shift_right_arithmetic(kpos, CHUNK_SHIFT)
        valid = (k_chunk >= q_chunk - LOOKBACK_CHUNKS) & (k_chunk <= q_chunk)
        if streaming:
            valid = valid & (kpos >= 0)
        if nk > nk_real:
            valid = valid & (colk < nk_real)
        return ks, vs, valid

    def scores(s, kv):
        ks, _, _ = sub_tile(s)
        c, first = divmod(kv, HEADS_PER_COL)
        kcol = ks[:, c * LANES:(c + 1) * LANES]
        bdk = _block_diag(kcol, pltpu.roll(kcol, HEAD_DIM, 1), first == 0).astype(BF16)
        qs = q_ref[kv * cols_per_kv:(kv + 1) * cols_per_kv, s * sq:(s + 1) * sq, :]
        qs = qs.reshape(cols_per_kv * sq, LANES)
        return lax.dot_general(qs, bdk, (((1,), (1,)), ((), ())), preferred_element_type=F32)

    def attend(s, kv, sc):
        _, vs, valid = sub_tile(s)
        c, first = divmod(kv, HEADS_PER_COL)
        vcol = vs[:, c * LANES:(c + 1) * LANES]
        bdv = _block_diag(vcol, pltpu.roll(vcol, HEAD_DIM, 1), first == 0).astype(BF16)
        p_rows = []
        for p in range(cols_per_kv):
            halves = []
            for e in range(HEADS_PER_COL):
                sink = sink_ref[(kv * cols_per_kv + p) * HEADS_PER_COL + e]
                blk = sc[p * sq:(p + 1) * sq, e * nk:(e + 1) * nk]
                blk = jnp.where(valid, blk, -jnp.inf)
                mx = jnp.maximum(jnp.max(blk, axis=-1, keepdims=True), sink)
                ex = jnp.exp(blk - mx)
                den = jnp.sum(ex, axis=-1, keepdims=True) + jnp.exp(sink - mx)
                halves.append((ex * (1.0 / den)).astype(BF16))
            p_rows.append(jnp.concatenate(halves, axis=1))
        pm = jnp.concatenate(p_rows, axis=0)
        ov = jnp.dot(pm, bdv, preferred_element_type=F32)
        for p in range(cols_per_kv):
            cc = kv * cols_per_kv + p
            att_ref[cur, s * sq:(s + 1) * sq, cc * LANES:(cc + 1) * LANES] = (
                ov[p * sq:(p + 1) * sq].astype(BF16))

    blocks = [(s, kv) for s in range(nsub) for kv in range(n_kv)]
    sc_next = scores(*blocks[0])
    for b, (s, kv) in enumerate(blocks):
        sc = sc_next
        if b + 1 < n_blocks:
            sc_next = scores(*blocks[b + 1])
        if lag and (b + 1) % blocks_per_chunk == 0:
            oc = ((b + 1) // blocks_per_chunk - 1) * ATTN_OUT_COLS
            o_ref[:, oc:oc + ATTN_OUT_COLS] = jnp.dot(
                att_ref[1 - cur], wo_ref[:, oc:oc + ATTN_OUT_COLS], preferred_element_type=F32)
        attend(s, kv, sc)

    if not lag:
        o_ref[...] = jnp.dot(att_ref[0], wo_ref[...], preferred_element_type=F32)
    o_ref[...] = x_ref[...] + _rms(o_ref[...], g_ref[...])


def _attention(x, q, k_prev, k, v_prev, v, sinks, norm_g, w_o, layer, j, *, tq, sq, n_kv,
               streaming):
    m, d = x.shape
    n_cols = q.shape[0]
    kv_w = k.shape[1]
    nsub = tq // sq
    n_tiles = m // tq
    lag = int(n_tiles >= ATTN_MIN_LAGGED_TILES)
    att_tile = lambda i: jnp.minimum(i, n_tiles - 1)
    out_tile = lambda i: jnp.maximum(i - lag, 0)
    if streaming:
        per = tq // WINDOW
        prev_spec = pl.BlockSpec((WINDOW, kv_w), lambda i: (jnp.maximum(att_tile(i) * per - 1, 0), 0))
    else:
        prev_spec = pl.BlockSpec((nsub, WINDOW, kv_w), lambda i: (att_tile(i), 0, 0))
    cur_spec = pl.BlockSpec((tq, kv_w), lambda i: (att_tile(i), 0))
    kern = functools.partial(_attn_kernel, sq=sq, n_kv=n_kv, streaming=streaming, lag=lag,
                             q_pos0=PAST_LEN, k_pos0=PAST_LEN - WINDOW)
    return pl.pallas_call(
        kern,
        grid=(n_tiles + lag,),
        in_specs=[
            pl.BlockSpec(memory_space=pltpu.SMEM),
            pl.BlockSpec((n_cols, tq, LANES), lambda i: (0, att_tile(i), 0)),
            prev_spec, cur_spec, prev_spec, cur_spec,
            pl.BlockSpec((tq, d), lambda i: (out_tile(i), 0)),
            pl.BlockSpec((None, None, 1, d), lambda i: (layer, 3, 0, 0)),
            pl.BlockSpec((None, d, d), lambda i: (j, 0, 0)),
        ],
        out_specs=pl.BlockSpec((tq, d), lambda i: (out_tile(i), 0)),
        out_shape=jax.ShapeDtypeStruct((m, d), F32),
        scratch_shapes=[pltpu.VMEM((1 + lag, tq, d), BF16)],
        compiler_params=_params("arbitrary"),
        name="swa_attention",
    )(sinks, q, k_prev, k, v_prev, v, x, norm_g, w_o)


def _rope_tables(pos):
    half = HEAD_DIM // 2
    inv_freq = ROPE_THETA ** (-jnp.arange(half, dtype=F32) / half)
    ang = pos.astype(F32)[:, None] * inv_freq[None, :]
    cos = jnp.cos(ang)
    sin = jnp.sin(ang)
    cos_t = jnp.concatenate([cos] * (LANES // half), axis=1)
    sin_t = jnp.concatenate([-sin, sin] * HEADS_PER_COL, axis=1)
    return cos_t, sin_t


def kernel(x_prompt, x_sample, state_conv, cache_k, cache_v, norm_g, w_ffn_in, w_ffn_out,
           w_conv_in, w_conv, w_conv_out, w_qkv, w_attn_out, attn_sinks):
    batch, seq, d = x_prompt.shape
    dec_batch, dec_seq, _ = x_sample.shape
    depth = norm_g.shape[0]
    n_heads = attn_sinks.shape[1]
    n_kv = cache_k.shape[3]
    kv_w = n_kv * HEAD_DIM
    assert batch == 1 and dec_seq == CHUNK and cache_k.shape[2] == WINDOW

    g4 = norm_g.reshape(depth, norm_g.shape[1], 1, d)
    w_conv_in_b = w_conv_in.astype(BF16)
    w_conv_out_b = w_conv_out.astype(BF16)
    w_qkv_b = w_qkv.astype(BF16)
    w_attn_out_b = w_attn_out.astype(BF16)

    xp = x_prompt.reshape(seq, d)
    xs = x_sample.reshape(dec_batch * dec_seq, d)
    cos_p, sin_p = _rope_tables(jnp.arange(seq))
    cos_s, sin_s = _rope_tables(jnp.tile(PAST_LEN + jnp.arange(dec_seq), dec_batch))

    tm = ROW_TILE
    conv_p, conv_s, k_p, v_p, k_s, v_s = [], [], [], [], [], []
    for layer in range(depth):
        j = layer // 2
        def ffn_pair(xp, xs, slot, n_pre, n_post):
            ffn = functools.partial(_ffn, norm_g=g4, layer=layer, n_pre=n_pre, n_post=n_post, tm=FFN_TM)
            xs, bf16_w = ffn(xs, weights=(w_ffn_in, w_ffn_out, slot), tf=FFN_TF_F32, tile_cols=FFN_TF)
            return ffn(xp, weights=bf16_w, tf=FFN_TF), xs

        xp, xs = ffn_pair(xp, xs, 0, 0, 1)
        if layer % 2 == 0:
            mix = functools.partial(_conv_mixer, norm_g=g4, w_in=w_conv_in_b, conv_w=w_conv,
                                    w_out=w_conv_out_b, layer=layer, j=j, tm=tm)
            xp, st = mix(xp, jnp.zeros((1, 2, d), F32), seg=tm, carried=True, tn=d)
            conv_p.append(st)
            xs, st = mix(xs, state_conv[j], seg=dec_seq, carried=False, tn=CONV_TN)
            conv_s.append(st)
        else:
            proj = functools.partial(_qkv_project, norm_g=g4, w_qkv=w_qkv_b, layer=layer, j=j,
                                     tm=tm, n_heads=n_heads, n_kv=n_kv)
            att = functools.partial(_attention, sinks=attn_sinks[j], norm_g=g4, w_o=w_attn_out_b,
                                    layer=layer, j=j, n_kv=n_kv)
            q, k, v = proj(xp, cos=cos_p, sin=sin_p)
            xp = att(xp, q, k, k, v, v, tq=tm, sq=WINDOW, streaming=True)
            k_p.append(k[seq - WINDOW:].reshape(1, WINDOW, n_kv, HEAD_DIM))
            v_p.append(v[seq - WINDOW:].reshape(1, WINDOW, n_kv, HEAD_DIM))
            q, k, v = proj(xs, cos=cos_s, sin=sin_s)
            ck = cache_k[j].reshape(dec_batch, WINDOW, kv_w)
            cv = cache_v[j].reshape(dec_batch, WINDOW, kv_w)
            xs = att(xs, q, ck, k, cv, v, tq=tm, sq=dec_seq, streaming=False)
            k_new = jnp.concatenate([ck, k.reshape(dec_batch, dec_seq, kv_w)], axis=1)[:, -WINDOW:]
            v_new = jnp.concatenate([cv, v.reshape(dec_batch, dec_seq, kv_w)], axis=1)[:, -WINDOW:]
            k_s.append(k_new.reshape(dec_batch, WINDOW, n_kv, HEAD_DIM))
            v_s.append(v_new.reshape(dec_batch, WINDOW, n_kv, HEAD_DIM))
        xp, xs = ffn_pair(xp, xs, 1, 4, 5)

    return (xp.reshape(batch, seq, d), xs.reshape(dec_batch, dec_seq, d),
            jnp.stack(conv_p), jnp.stack(conv_s),
            jnp.stack(k_p), jnp.stack(v_p), jnp.stack(k_s), jnp.stack(v_s))
```

```python
import functools

import jax
import jax.numpy as jnp
from jax import lax
from jax.experimental import pallas as pl
from jax.experimental.pallas import tpu as pltpu

F32 = jnp.float32
BF16 = jnp.bfloat16

CHUNK = 64
CHUNK_SHIFT = CHUNK.bit_length() - 1
assert 1 << CHUNK_SHIFT == CHUNK
LOOKBACK_CHUNKS = 2
WINDOW = LOOKBACK_CHUNKS * CHUNK
HEAD_DIM = 64
PAST_LEN = 2048
ROPE_THETA = 10000.0
NORM_EPS = 1e-6

LANES = 128
HEADS_PER_COL = LANES // HEAD_DIM
VMEM_LIMIT = 62 * 1024 * 1024
ROW_TILE = 512
FFN_TM = 1024
FFN_ROWS = 512
FFN_POST_ROWS = 256
FFN_TF = 512
FFN_TF_F32 = 256
CONV_TN = 512
CONV_COLS = 256
ATTN_OUT_COLS = 256
ATTN_MIN_LAGGED_TILES = 4


def _rms(x, g):
    ms = jnp.mean(x * x, axis=-1, keepdims=True)
    return x * lax.rsqrt(ms + NORM_EPS) * g


def _params(*sem):
    return pltpu.CompilerParams(dimension_semantics=sem, vmem_limit_bytes=VMEM_LIMIT)


def _ffn_kernel(x_ref, gpre_ref, gpost_ref, wg_ref, wu_ref, wo_ref, o_ref, *rest, cast_own):
    own_out, h_ref = rest[:-1], rest[-1]
    j = pl.program_id(1)
    last = pl.num_programs(1) - 1

    def step(first, final):
        wg, wu, wo = wg_ref[...], wu_ref[...], wo_ref[...]
        if cast_own:
            wg, wu, wo = wg.astype(BF16), wu.astype(BF16), wo.astype(BF16)
            for w, dst in zip((wg, wu, wo), own_out):
                dst[...] = w
        group = FFN_ROWS if first or final else x_ref.shape[0]
        for r in range(0, x_ref.shape[0], group):
            rows = slice(r, r + group)
            if first:
                h = _rms(x_ref[rows], gpre_ref[...]).astype(BF16)
                h_ref[rows] = h
            else:
                h = h_ref[rows]
            gate = jnp.dot(h, wg, preferred_element_type=F32)
            up = jnp.dot(h, wu, preferred_element_type=F32)
            act = (gate * jax.nn.sigmoid(gate) * up).astype(BF16)
            if final:
                for h0 in range(0, group, FFN_POST_ROWS):
                    sub = slice(r + h0, r + h0 + FFN_POST_ROWS)
                    part = jnp.dot(act[h0:h0 + FFN_POST_ROWS], wo, preferred_element_type=F32)
                    o_ref[sub] = x_ref[sub] + 0.5 * _rms(o_ref[sub] + part, gpost_ref[...])
                continue
            part = jnp.dot(act, wo, preferred_element_type=F32)
            if first:
                o_ref[rows] = part
            else:
                o_ref[rows] += part

    pl.when(j == 0)(functools.partial(step, True, False))
    pl.when((j > 0) & (j < last))(functools.partial(step, False, False))
    pl.when(j == last)(functools.partial(step, False, True))


def _ffn(x, norm_g, weights, layer, n_pre, n_post, *, tm, tf, tile_cols=None):
    m, d = x.shape
    cast_own = weights[0].dtype == F32
    g_spec = lambda n: pl.BlockSpec((None, None, 1, d), lambda i, j: (layer, n, 0, 0))
    out_specs = [pl.BlockSpec((tm, d), lambda i, j: (i, 0))]
    out_shape = [jax.ShapeDtypeStruct((m, d), F32)]
    if cast_own:
        w_in, w_out, slot = weights
        f = w_out.shape[2]
        nf, per_tile = f // tf, tile_cols // tf
        assert m == tm and per_tile * tf == tile_cols
        w_specs = [
            pl.BlockSpec((None, None, d, tf), lambda i, j: (layer, slot, 0, j)),
            pl.BlockSpec((None, None, d, tf), lambda i, j: (layer, slot, 0, j + nf)),
            pl.BlockSpec((None, None, tf, d), lambda i, j: (layer, slot, j, 0)),
        ]
        operands = [w_in, w_in, w_out]
        tile_spec = pl.BlockSpec((None, d, tf), lambda i, j: (j // per_tile, 0, j % per_tile))
        out_specs += [tile_spec, tile_spec, pl.BlockSpec((tf, d), lambda i, j: (j, 0))]
        tiles = jax.ShapeDtypeStruct((f // tile_cols, d, tile_cols), BF16)
        out_shape += [tiles, tiles, jax.ShapeDtypeStruct((f, d), BF16)]
    else:
        gate, up, down = operands = list(weights)
        f = down.shape[0]
        assert gate.shape[2] == tf
        tile_spec = pl.BlockSpec((None, d, tf), lambda i, j: (j, 0, 0))
        w_specs = [tile_spec, tile_spec, pl.BlockSpec((tf, d), lambda i, j: (j, 0))]
    outs = pl.pallas_call(
        functools.partial(_ffn_kernel, cast_own=cast_own),
        grid=(m // tm, f // tf),
        in_specs=[pl.BlockSpec((tm, d), lambda i, j: (i, 0)), g_spec(n_pre), g_spec(n_post)] + w_specs,
        out_specs=out_specs,
        out_shape=out_shape,
        scratch_shapes=[pltpu.VMEM((tm, d), BF16)],
        compiler_params=_params("parallel", "arbitrary"),
        name="ffn",
    )(x, norm_g, norm_g, *operands)
    return (outs[0], tuple(outs[1:])) if cast_own else outs[0]


def _conv_kernel(x_ref, gpre_ref, gpost_ref, wb_ref, wc_ref, wh_ref, cw_ref, wo_ref, st_ref,
                 o_ref, ns_ref, h_ref, carry_ref, *, seg, carried):
    i = pl.program_id(0)
    n = pl.program_id(1)
    last = pl.num_programs(1) - 1
    tm = x_ref.shape[0]
    tn = wb_ref.shape[1]
    tc = CONV_COLS
    cols = pl.ds(pl.multiple_of(n * tn, tn), tn)

    if carried:
        @pl.when(i == 0)
        def _():
            carry_ref[:, cols] = st_ref[0, :, cols]

    def gated_conv(c0, b_gate, c_gate, hh):
        gcols = pl.ds(pl.multiple_of(n * tn + c0, tc), tc)
        cw = cw_ref[:, c0:c0 + tc]
        w0, w1, w2 = cw[0:1], cw[1:2], cw[2:3]
        row = lax.broadcasted_iota(jnp.int32, (seg, tc), 0)
        u = c_gate * hh
        convs = []
        for s in range(tm // seg):
            us = u[s * seg:(s + 1) * seg]
            prev = carry_ref[:, gcols] if carried else st_ref[s, :, c0:c0 + tc]
            p1 = jnp.where(row == 0, prev[1:2], pltpu.roll(us, 1, 0))
            p2 = jnp.where(row == 0, prev[0:1], jnp.where(row == 1, prev[1:2], pltpu.roll(us, 2, 0)))
            convs.append(p2 * w0 + p1 * w1 + us * w2)
            if carried:
                carry_ref[:, gcols] = us[seg - 2:seg]
                ns_ref[0, :, gcols] = us[seg - 2:seg]
            else:
                ns_ref[s, :, c0:c0 + tc] = us[seg - 2:seg]
        conv = convs[0] if len(convs) == 1 else jnp.concatenate(convs, axis=0)
        return (b_gate * conv).astype(BF16)

    def out_proj(assign, final, c0, z):
        part = jnp.dot(z, wo_ref[c0:c0 + tc, :], preferred_element_type=F32)
        if assign:
            o_ref[...] = part
        elif final:
            o_ref[...] = x_ref[...] + _rms(o_ref[...] + part, gpost_ref[...])
        else:
            o_ref[...] += part

    def step(first, final):
        if first:
            h = _rms(x_ref[...], gpre_ref[...]).astype(BF16)
            h_ref[...] = h
        else:
            h = h_ref[...]
        pending = None
        for c0 in range(0, tn, tc):
            b_gate = jnp.dot(h, wb_ref[:, c0:c0 + tc], preferred_element_type=F32)
            c_gate = jnp.dot(h, wc_ref[:, c0:c0 + tc], preferred_element_type=F32)
            hh = jnp.dot(h, wh_ref[:, c0:c0 + tc], preferred_element_type=F32)
            if pending is not None:
                out_proj(first and pending[0] == 0, False, *pending)
            pending = (c0, gated_conv(c0, b_gate, c_gate, hh))
        out_proj(first and pending[0] == 0, final, *pending)

    if tn == o_ref.shape[1]:
        step(True, True)
    else:
        pl.when(n == 0)(functools.partial(step, True, False))
        pl.when((n > 0) & (n < last))(functools.partial(step, False, False))
        pl.when(n == last)(functools.partial(step, False, True))


def _conv_mixer(x, state, norm_g, w_in, conv_w, w_out, layer, j, *, tm, tn, seg, carried):
    m, d = x.shape
    nn = d // tn
    nseg = tm // seg
    g_spec = lambda n: pl.BlockSpec((None, None, 1, d), lambda i, c: (layer, n, 0, 0))
    once = dict(pipeline_mode=pl.Buffered(1)) if nn == 1 else {}
    w_spec = lambda sec: pl.BlockSpec((None, d, tn), lambda i, c: (j, 0, c + sec * nn), **once)
    if carried:
        st_spec = pl.BlockSpec((1, 2, d), lambda i, c: (0, 0, 0))
    else:
        st_spec = pl.BlockSpec((nseg, 2, tn), lambda i, c: (i, 0, c))
    return pl.pallas_call(
        functools.partial(_conv_kernel, seg=seg, carried=carried),
        grid=(m // tm, nn),
        in_specs=[
            pl.BlockSpec((tm, d), lambda i, c: (i, 0)),
            g_spec(2),
            g_spec(3),
            w_spec(0),
            w_spec(1),
            w_spec(2),
            pl.BlockSpec((None, 3, tn), lambda i, c: (j, 0, c)),
            pl.BlockSpec((None, tn, d), lambda i, c: (j, c, 0), **once),
            st_spec,
        ],
        out_specs=[pl.BlockSpec((tm, d), lambda i, c: (i, 0)), st_spec],
        out_shape=[
            jax.ShapeDtypeStruct((m, d), F32),
            jax.ShapeDtypeStruct(state.shape, F32),
        ],
        scratch_shapes=[pltpu.VMEM((tm, d), BF16), pltpu.VMEM((2, d), F32)],
        compiler_params=_params("arbitrary", "arbitrary"),
        name="conv_mixer",
    )(x, norm_g, norm_g, w_in, w_in, w_in, conv_w, w_out, state)


def _rope_col(xc, cos, sin_signed):
    lane = lax.broadcasted_iota(jnp.int32, xc.shape, 1)
    first_half = (lane % HEAD_DIM) < HEAD_DIM // 2
    swapped = jnp.where(first_half,
                        pltpu.roll(xc, LANES - HEAD_DIM // 2, 1),
                        pltpu.roll(xc, HEAD_DIM // 2, 1))
    return xc * cos + swapped * sin_signed


def _qkv_kernel(x_ref, g_ref, w_ref, cos_ref, sin_ref, q_ref, k_ref, v_ref, *, chunk):
    n_q = q_ref.shape[0] * LANES
    kv_w = k_ref.shape[1]
    h = _rms(x_ref[...], g_ref[...]).astype(BF16)
    cos = cos_ref[...]
    sin = sin_ref[...]
    for c0 in range(0, n_q, chunk):
        y = jnp.dot(h, w_ref[:, c0:c0 + chunk], preferred_element_type=F32)
        for c in range(chunk // LANES):
            qc = _rope_col(y[:, c * LANES:(c + 1) * LANES], cos, sin) * (HEAD_DIM ** -0.5)
            q_ref[c0 // LANES + c] = qc.astype(BF16)
    y = jnp.dot(h, w_ref[:, n_q:], preferred_element_type=F32)
    for c in range(kv_w // LANES):
        k_ref[:, c * LANES:(c + 1) * LANES] = _rope_col(y[:, c * LANES:(c + 1) * LANES], cos, sin)
    v_ref[...] = y[:, kv_w:]


def _qkv_project(x, norm_g, w_qkv, cos, sin, layer, j, *, tm, n_heads, n_kv):
    m, d = x.shape
    kv_w = n_kv * HEAD_DIM
    n_cols = n_heads // HEADS_PER_COL
    return pl.pallas_call(
        functools.partial(_qkv_kernel, chunk=2 * kv_w),
        grid=(m // tm,),
        in_specs=[
            pl.BlockSpec((tm, d), lambda i: (i, 0)),
            pl.BlockSpec((None, None, 1, d), lambda i: (layer, 2, 0, 0)),
            pl.BlockSpec((None, d, w_qkv.shape[2]), lambda i: (j, 0, 0)),
            pl.BlockSpec((tm, LANES), lambda i: (i, 0)),
            pl.BlockSpec((tm, LANES), lambda i: (i, 0)),
        ],
        out_specs=[
            pl.BlockSpec((n_cols, tm, LANES), lambda i: (0, i, 0)),
            pl.BlockSpec((tm, kv_w), lambda i: (i, 0)),
            pl.BlockSpec((tm, kv_w), lambda i: (i, 0)),
        ],
        out_shape=[
            jax.ShapeDtypeStruct((n_cols, m, LANES), BF16),
            jax.ShapeDtypeStruct((m, kv_w), F32),
            jax.ShapeDtypeStruct((m, kv_w), F32),
        ],
        compiler_params=_params("parallel"),
        name="qkv_rope",
    )(x, norm_g, w_qkv, cos, sin)


def _block_diag(col, rolled, first):
    lane = lax.broadcasted_iota(jnp.int32, col.shape, 1)
    lo = lane < HEAD_DIM
    zero = jnp.zeros_like(col)
    if first:
        return jnp.concatenate([jnp.where(lo, col, zero), jnp.where(lo, zero, rolled)], axis=0)
    return jnp.concatenate([jnp.where(lo, rolled, zero), jnp.where(lo, zero, col)], axis=0)


def _attn_kernel(sink_ref, q_ref, kp_ref, kc_ref, vp_ref, vc_ref, x_ref, g_ref, wo_ref, o_ref,
                 att_ref, *, sq, n_kv, streaming, lag, q_pos0, k_pos0):
    i = pl.program_id(0)
    t = jnp.minimum(i, pl.num_programs(0) - 1 - lag)
    cur = lax.rem(i, 2) if lag else 0
    tq, d = x_ref.shape
    nsub = tq // sq
    nk_real = WINDOW + sq
    nk = -(-nk_real // LANES) * LANES
    assert nk == nk_real or not streaming
    n_cols = q_ref.shape[0]
    cols_per_kv = n_cols // n_kv
    n_blocks = nsub * n_kv
    blocks_per_chunk = n_blocks // (d // ATTN_OUT_COLS)

    if lag:
        @pl.when(i == 0)
        def _():
            att_ref[1] = jnp.zeros((tq, d), BF16)

    if streaming:
        k_all = jnp.concatenate([kp_ref[...], kc_ref[...]], axis=0)
        v_all = jnp.concatenate([vp_ref[...], vc_ref[...]], axis=0)

    row = lax.broadcasted_iota(jnp.int32, (sq, nk), 0)
    colk = lax.broadcasted_iota(jnp.int32, (sq, nk), 1)

    @functools.cache
    def sub_tile(s):
        if streaming:
            q0 = t * tq + s * sq
            k0 = q0 - WINDOW
            ks = k_all[s * sq:s * sq + nk]
            vs = v_all[s * sq:s * sq + nk]
        else:
            q0 = q_pos0
            k0 = k_pos0
            pad = [jnp.zeros((nk - nk_real, kp_ref.shape[2]), F32)] if nk > nk_real else []
            ks = jnp.concatenate([kp_ref[s], kc_ref[s * sq:(s + 1) * sq]] + pad, axis=0)
            vs = jnp.concatenate([vp_ref[s], vc_ref[s * sq:(s + 1) * sq]] + pad, axis=0)
        kpos = k0 + colk
        q_chunk = lax.shift_right_arithmetic(q0 + row, CHUNK_SHIFT)
        k_chunk = lax.shift_right_arithmetic(kpos, CHUNK_SHIFT)
        valid = (k_chunk >= q_chunk - LOOKBACK_CHUNKS) & (k_chunk <= q_chunk)
        if streaming:
            valid = valid & (kpos >= 0)
        if nk > nk_real:
            valid = valid & (colk < nk_real)
        return ks, vs, valid

    def scores(s, kv):
        ks, _, _ = sub_tile(s)
        c, first = divmod(kv, HEADS_PER_COL)
        kcol = ks[:, c * LANES:(c + 1) * LANES]
        bdk = _block_diag(kcol, pltpu.roll(kcol, HEAD_DIM, 1), first == 0).astype(BF16)
        qs = q_ref[kv * cols_per_kv:(kv + 1) * cols_per_kv, s * sq:(s + 1) * sq, :]
        qs = qs.reshape(cols_per_kv * sq, LANES)
        return lax.dot_general(qs, bdk, (((1,), (1,)), ((), ())), preferred_element_type=F32)

    def attend(s, kv, sc):
        _, vs, valid = sub_tile(s)
        c, first = divmod(kv, HEADS_PER_COL)
        vcol = vs[:, c * LANES:(c + 1) * LANES]
        bdv = _block_diag(vcol, pltpu.roll(vcol, HEAD_DIM, 1), first == 0).astype(BF16)
        p_rows = []
        for p in range(cols_per_kv):
            halves = []
            for e in range(HEADS_PER_COL):
                sink = sink_ref[(kv * cols_per_kv + p) * HEADS_PER_COL + e]
                blk = sc[p * sq:(p + 1) * sq, e * nk:(e + 1) * nk]
                blk = jnp.where(valid, blk, -jnp.inf)
                mx = jnp.maximum(jnp.max(blk, axis=-1, keepdims=True), sink)
                ex = jnp.exp(blk - mx)
                den = jnp.sum(ex, axis=-1, keepdims=True) + jnp.exp(sink - mx)
                halves.append((ex * (1.0 / den)).astype(BF16))
            p_rows.append(jnp.concatenate(halves, axis=1))
        pm = jnp.concatenate(p_rows, axis=0)
        ov = jnp.dot(pm, bdv, preferred_element_type=F32)
        for p in range(cols_per_kv):
            cc = kv * cols_per_kv + p
            att_ref[cur, s * sq:(s + 1) * sq, cc * LANES:(cc + 1) * LANES] = (
                ov[p * sq:(p + 1) * sq].astype(BF16))

    blocks = [(s, kv) for s in range(nsub) for kv in range(n_kv)]
    sc_next = scores(*blocks[0])
    for b, (s, kv) in enumerate(blocks):
        sc = sc_next
        if b + 1 < n_blocks:
            sc_next = scores(*blocks[b + 1])
        if lag and (b + 1) % blocks_per_chunk == 0:
            oc = ((b + 1) // blocks_per_chunk - 1) * ATTN_OUT_COLS
            o_ref[:, oc:oc + ATTN_OUT_COLS] = jnp.dot(
                att_ref[1 - cur], wo_ref[:, oc:oc + ATTN_OUT_COLS], preferred_element_type=F32)
        attend(s, kv, sc)

    if not lag:
        o_ref[...] = jnp.dot(att_ref[0], wo_ref[...], preferred_element_type=F32)
    o_ref[...] = x_ref[...] + _rms(o_ref[...], g_ref[...])


def _attention(x, q, k_prev, k, v_prev, v, sinks, norm_g, w_o, layer, j, *, tq, sq, n_kv,
               streaming):
    m, d = x.shape
    n_cols = q.shape[0]
    kv_w = k.shape[1]
    nsub = tq // sq
    n_tiles = m // tq
    lag = int(n_tiles >= ATTN_MIN_LAGGED_TILES)
    att_tile = lambda i: jnp.minimum(i, n_tiles - 1)
    out_tile = lambda i: jnp.maximum(i - lag, 0)
    if streaming:
        per = tq // WINDOW
        prev_spec = pl.BlockSpec((WINDOW, kv_w), lambda i: (jnp.maximum(att_tile(i) * per - 1, 0), 0))
    else:
        prev_spec = pl.BlockSpec((nsub, WINDOW, kv_w), lambda i: (att_tile(i), 0, 0))
    cur_spec = pl.BlockSpec((tq, kv_w), lambda i: (att_tile(i), 0))
    kern = functools.partial(_attn_kernel, sq=sq, n_kv=n_kv, streaming=streaming, lag=lag,
                             q_pos0=PAST_LEN, k_pos0=PAST_LEN - WINDOW)
    return pl.pallas_call(
        kern,
        grid=(n_tiles + lag,),
        in_specs=[
            pl.BlockSpec(memory_space=pltpu.SMEM),
            pl.BlockSpec((n_cols, tq, LANES), lambda i: (0, att_tile(i), 0)),
            prev_spec, cur_spec, prev_spec, cur_spec,
            pl.BlockSpec((tq, d), lambda i: (out_tile(i), 0)),
            pl.BlockSpec((None, None, 1, d), lambda i: (layer, 3, 0, 0)),
            pl.BlockSpec((None, d, d), lambda i: (j, 0, 0)),
        ],
        out_specs=pl.BlockSpec((tq, d), lambda i: (out_tile(i), 0)),
        out_shape=jax.ShapeDtypeStruct((m, d), F32),
        scratch_shapes=[pltpu.VMEM((1 + lag, tq, d), BF16)],
        compiler_params=_params("arbitrary"),
        name="swa_attention",
    )(sinks, q, k_prev, k, v_prev, v, x, norm_g, w_o)


def _rope_tables(pos):
    half = HEAD_DIM // 2
    inv_freq = ROPE_THETA ** (-jnp.arange(half, dtype=F32) / half)
    inv_freq_lanes = jnp.tile(inv_freq, LANES // half)
    sign_lanes = jnp.tile(jnp.concatenate([-jnp.ones(half, F32), jnp.ones(half, F32)]), HEADS_PER_COL)
    ang = pos.astype(F32)[:, None] * inv_freq_lanes[None, :]
    return jnp.cos(ang), sign_lanes[None, :] * jnp.sin(ang)


def kernel(x_prompt, x_sample, state_conv, cache_k, cache_v, norm_g, w_ffn_in, w_ffn_out,
           w_conv_in, w_conv, w_conv_out, w_qkv, w_attn_out, attn_sinks):
    batch, seq, d = x_prompt.shape
    dec_batch, dec_seq, _ = x_sample.shape
    depth = norm_g.shape[0]
    n_heads = attn_sinks.shape[1]
    n_kv = cache_k.shape[3]
    kv_w = n_kv * HEAD_DIM
    assert batch == 1 and dec_seq == CHUNK and cache_k.shape[2] == WINDOW

    g4 = norm_g.reshape(depth, norm_g.shape[1], 1, d)
    w_conv_in_b = w_conv_in.astype(BF16)
    w_conv_out_b = w_conv_out.astype(BF16)
    w_qkv_b = w_qkv.astype(BF16)
    w_attn_out_b = w_attn_out.astype(BF16)

    xp = x_prompt.reshape(seq, d)
    xs = x_sample.reshape(dec_batch * dec_seq, d)
    cos_p, sin_p = _rope_tables(jnp.arange(seq))
    cos_s, sin_s = _rope_tables(jnp.tile(PAST_LEN + jnp.arange(dec_seq), dec_batch))

    tm = ROW_TILE
    conv_p, conv_s, k_p, v_p, k_s, v_s = [], [], [], [], [], []
    for layer in range(depth):
        j = layer // 2
        def ffn_pair(xp, xs, slot, n_pre, n_post):
            ffn = functools.partial(_ffn, norm_g=g4, layer=layer, n_pre=n_pre, n_post=n_post, tm=FFN_TM)
            xs, bf16_w = ffn(xs, weights=(w_ffn_in, w_ffn_out, slot), tf=FFN_TF_F32, tile_cols=FFN_TF)
            return ffn(xp, weights=bf16_w, tf=FFN_TF), xs

        xp, xs = ffn_pair(xp, xs, 0, 0, 1)
        if layer % 2 == 0:
            mix = functools.partial(_conv_mixer, norm_g=g4, w_in=w_conv_in_b, conv_w=w_conv,
                                    w_out=w_conv_out_b, layer=layer, j=j, tm=tm)
            xp, st = mix(xp, jnp.zeros((1, 2, d), F32), seg=tm, carried=True, tn=d)
            conv_p.append(st)
            xs, st = mix(xs, state_conv[j], seg=dec_seq, carried=False, tn=CONV_TN)
            conv_s.append(st)
        else:
            proj = functools.partial(_qkv_project, norm_g=g4, w_qkv=w_qkv_b, layer=layer, j=j,
                                     tm=tm, n_heads=n_heads, n_kv=n_kv)
            att = functools.partial(_attention, sinks=attn_sinks[j], norm_g=g4, w_o=w_attn_out_b,
                                    layer=layer, j=j, n_kv=n_kv)
            q, k, v = proj(xp, cos=cos_p, sin=sin_p)
            xp = att(xp, q, k, k, v, v, tq=tm, sq=WINDOW, streaming=True)
            k_p.append(k[seq - WINDOW:].reshape(1, WINDOW, n_kv, HEAD_DIM))
            v_p.append(v[seq - WINDOW:].reshape(1, WINDOW, n_kv, HEAD_DIM))
            q, k, v = proj(xs, cos=cos_s, sin=sin_s)
            ck = cache_k[j].reshape(dec_batch, WINDOW, kv_w)
            cv = cache_v[j].reshape(dec_batch, WINDOW, kv_w)
            xs = att(xs, q, ck, k, cv, v, tq=tm, sq=dec_seq, streaming=False)
            k_new = jnp.concatenate([ck, k.reshape(dec_batch, dec_seq, kv_w)], axis=1)[:, -WINDOW:]
            v_new = jnp.concatenate([cv, v.reshape(dec_batch, dec_seq, kv_w)], axis=1)[:, -WINDOW:]
            k_s.append(k_new.reshape(dec_batch, WINDOW, n_kv, HEAD_DIM))
            v_s.append(v_new.reshape(dec_batch, WINDOW, n_kv, HEAD_DIM))
        xp, xs = ffn_pair(xp, xs, 1, 4, 5)

    return (xp.reshape(batch, seq, d), xs.reshape(dec_batch, dec_seq, d),
            jnp.stack(conv_p), jnp.stack(conv_s),
            jnp.stack(k_p), jnp.stack(v_p), jnp.stack(k_s), jnp.stack(v_s))
```

```python
import functools

import jax
import jax.numpy as jnp
from jax import lax
from jax.experimental import pallas as pl
from jax.experimental.pallas import tpu as pltpu

F32 = jnp.float32
BF16 = jnp.bfloat16

CHUNK = 64
CHUNK_SHIFT = CHUNK.bit_length() - 1
assert 1 << CHUNK_SHIFT == CHUNK
LOOKBACK_CHUNKS = 2
WINDOW = LOOKBACK_CHUNKS * CHUNK
HEAD_DIM = 64
PAST_LEN = 2048
ROPE_THETA = 10000.0
NORM_EPS = 1e-6

LANES = 128
HEADS_PER_COL = LANES // HEAD_DIM
VMEM_LIMIT = 62 * 1024 * 1024
ROW_TILE = 512
FFN_TM = 1024
FFN_ROWS = 512
FFN_POST_ROWS = 256
FFN_TF = 512
FFN_TF_F32 = 256
CONV_TN = 512
CONV_COLS = 256
ATTN_OUT_COLS = 256
ATTN_MIN_LAGGED_TILES = 4


def _rms(x, g):
    ms = jnp.mean(x * x, axis=-1, keepdims=True)
    return x * lax.rsqrt(ms + NORM_EPS) * g


def _params(*sem):
    return pltpu.CompilerParams(dimension_semantics=sem, vmem_limit_bytes=VMEM_LIMIT)


def _ffn_kernel(x_ref, gpre_ref, gpost_ref, wg_ref, wu_ref, wo_ref, o_ref, *rest, cast_own):
    own_out, h_ref = rest[:-1], rest[-1]
    j = pl.program_id(1)
    last = pl.num_programs(1) - 1

    def step(first, final):
        wg, wu, wo = wg_ref[...], wu_ref[...], wo_ref[...]
        if cast_own:
            wg, wu, wo = wg.astype(BF16), wu.astype(BF16), wo.astype(BF16)
            for w, dst in zip((wg, wu, wo), own_out):
                dst[...] = w
        group = FFN_ROWS if first else x_ref.shape[0]
        for r in range(0, x_ref.shape[0], group):
            rows = slice(r, r + group)
            if first:
                h = _rms(x_ref[rows], gpre_ref[...]).astype(BF16)
                h_ref[rows] = h
            else:
                h = h_ref[rows]
            gate = jnp.dot(h, wg, preferred_element_type=F32)
            up = jnp.dot(h, wu, preferred_element_type=F32)
            act = (gate * jax.nn.sigmoid(gate) * up).astype(BF16)
            if final:
                for h0 in range(0, group, FFN_POST_ROWS):
                    sub = slice(r + h0, r + h0 + FFN_POST_ROWS)
                    part = jnp.dot(act[h0:h0 + FFN_POST_ROWS], wo, preferred_element_type=F32)
                    o_ref[sub] = x_ref[sub] + 0.5 * _rms(o_ref[sub] + part, gpost_ref[...])
                continue
            part = jnp.dot(act, wo, preferred_element_type=F32)
            if first:
                o_ref[rows] = part
            else:
                o_ref[rows] += part

    pl.when(j == 0)(functools.partial(step, True, False))
    pl.when((j > 0) & (j < last))(functools.partial(step, False, False))
    pl.when(j == last)(functools.partial(step, False, True))


def _ffn(x, norm_g, weights, layer, n_pre, n_post, *, tm, tf, tile_cols=None):
    m, d = x.shape
    cast_own = weights[0].dtype == F32
    g_spec = lambda n: pl.BlockSpec((None, None, 1, d), lambda i, j: (layer, n, 0, 0))
    out_specs = [pl.BlockSpec((tm, d), lambda i, j: (i, 0))]
    out_shape = [jax.ShapeDtypeStruct((m, d), F32)]
    if cast_own:
        w_in, w_out, slot = weights
        f = w_out.shape[2]
        nf, per_tile = f // tf, tile_cols // tf
        assert m == tm and per_tile * tf == tile_cols
        w_specs = [
            pl.BlockSpec((None, None, d, tf), lambda i, j: (layer, slot, 0, j)),
            pl.BlockSpec((None, None, d, tf), lambda i, j: (layer, slot, 0, j + nf)),
            pl.BlockSpec((None, None, tf, d), lambda i, j: (layer, slot, j, 0)),
        ]
        operands = [w_in, w_in, w_out]
        tile_spec = pl.BlockSpec((None, d, tf), lambda i, j: (j // per_tile, 0, j % per_tile))
        out_specs += [tile_spec, tile_spec, pl.BlockSpec((tf, d), lambda i, j: (j, 0))]
        tiles = jax.ShapeDtypeStruct((f // tile_cols, d, tile_cols), BF16)
        out_shape += [tiles, tiles, jax.ShapeDtypeStruct((f, d), BF16)]
    else:
        gate, up, down = operands = list(weights)
        f = down.shape[0]
        assert gate.shape[2] == tf
        tile_spec = pl.BlockSpec((None, d, tf), lambda i, j: (j, 0, 0))
        w_specs = [tile_spec, tile_spec, pl.BlockSpec((tf, d), lambda i, j: (j, 0))]
    outs = pl.pallas_call(
        functools.partial(_ffn_kernel, cast_own=cast_own),
        grid=(m // tm, f // tf),
        in_specs=[pl.BlockSpec((tm, d), lambda i, j: (i, 0)), g_spec(n_pre), g_spec(n_post)] + w_specs,
        out_specs=out_specs,
        out_shape=out_shape,
        scratch_shapes=[pltpu.VMEM((tm, d), BF16)],
        compiler_params=_params("parallel", "arbitrary"),
        name="ffn",
    )(x, norm_g, norm_g, *operands)
    return (outs[0], tuple(outs[1:])) if cast_own else outs[0]


def _conv_kernel(x_ref, gpre_ref, gpost_ref, wb_ref, wc_ref, wh_ref, cw_ref, wo_ref, st_ref,
                 o_ref, ns_ref, h_ref, carry_ref, *, seg, carried):
    i = pl.program_id(0)
    n = pl.program_id(1)
    last = pl.num_programs(1) - 1
    tm = x_ref.shape[0]
    tn = wb_ref.shape[1]
    tc = CONV_COLS
    cols = pl.ds(pl.multiple_of(n * tn, tn), tn)

    if carried:
        @pl.when(i == 0)
        def _():
            carry_ref[:, cols] = st_ref[0, :, cols]

    def gated_conv(c0, b_gate, c_gate, hh):
        gcols = pl.ds(pl.multiple_of(n * tn + c0, tc), tc)
        cw = cw_ref[:, c0:c0 + tc]
        w0, w1, w2 = cw[0:1], cw[1:2], cw[2:3]
        row = lax.broadcasted_iota(jnp.int32, (seg, tc), 0)
        u = c_gate * hh
        convs = []
        for s in range(tm // seg):
            us = u[s * seg:(s + 1) * seg]
            prev = carry_ref[:, gcols] if carried else st_ref[s, :, c0:c0 + tc]
            p1 = jnp.where(row == 0, prev[1:2], pltpu.roll(us, 1, 0))
            p2 = jnp.where(row == 0, prev[0:1], jnp.where(row == 1, prev[1:2], pltpu.roll(us, 2, 0)))
            convs.append(p2 * w0 + p1 * w1 + us * w2)
            if carried:
                carry_ref[:, gcols] = us[seg - 2:seg]
                ns_ref[0, :, gcols] = us[seg - 2:seg]
            else:
                ns_ref[s, :, c0:c0 + tc] = us[seg - 2:seg]
        conv = convs[0] if len(convs) == 1 else jnp.concatenate(convs, axis=0)
        return (b_gate * conv).astype(BF16)

    def out_proj(assign, final, c0, z):
        part = jnp.dot(z, wo_ref[c0:c0 + tc, :], preferred_element_type=F32)
        if assign:
            o_ref[...] = part
        elif final:
            o_ref[...] = x_ref[...] + _rms(o_ref[...] + part, gpost_ref[...])
        else:
            o_ref[...] += part

    def step(first, final):
        if first:
            h = _rms(x_ref[...], gpre_ref[...]).astype(BF16)
            h_ref[...] = h
        else:
            h = h_ref[...]
        pending = None
        for c0 in range(0, tn, tc):
            b_gate = jnp.dot(h, wb_ref[:, c0:c0 + tc], preferred_element_type=F32)
            c_gate = jnp.dot(h, wc_ref[:, c0:c0 + tc], preferred_element_type=F32)
            hh = jnp.dot(h, wh_ref[:, c0:c0 + tc], preferred_element_type=F32)
            if pending is not None:
                out_proj(first and pending[0] == 0, False, *pending)
            pending = (c0, gated_conv(c0, b_gate, c_gate, hh))
        out_proj(first and pending[0] == 0, final, *pending)

    if tn == o_ref.shape[1]:
        step(True, True)
    else:
        pl.when(n == 0)(functools.partial(step, True, False))
        pl.when((n > 0) & (n < last))(functools.partial(step, False, False))
        pl.when(n == last)(functools.partial(step, False, True))


def _conv_mixer(x, state, norm_g, w_in, conv_w, w_out, layer, j, *, tm, tn, seg, carried):
    m, d = x.shape
    nn = d // tn
    nseg = tm // seg
    g_spec = lambda n: pl.BlockSpec((None, None, 1, d), lambda i, c: (layer, n, 0, 0))
    once = dict(pipeline_mode=pl.Buffered(1)) if nn == 1 else {}
    w_spec = lambda sec: pl.BlockSpec((None, d, tn), lambda i, c: (j, 0, c + sec * nn), **once)
    if carried:
        st_spec = pl.BlockSpec((1, 2, d), lambda i, c: (0, 0, 0))
    else:
        st_spec = pl.BlockSpec((nseg, 2, tn), lambda i, c: (i, 0, c))
    return pl.pallas_call(
        functools.partial(_conv_kernel, seg=seg, carried=carried),
        grid=(m // tm, nn),
        in_specs=[
            pl.BlockSpec((tm, d), lambda i, c: (i, 0)),
            g_spec(2),
            g_spec(3),
            w_spec(0),
            w_spec(1),
            w_spec(2),
            pl.BlockSpec((None, 3, tn), lambda i, c: (j, 0, c)),
            pl.BlockSpec((None, tn, d), lambda i, c: (j, c, 0), **once),
            st_spec,
        ],
        out_specs=[pl.BlockSpec((tm, d), lambda i, c: (i, 0)), st_spec],
        out_shape=[
            jax.ShapeDtypeStruct((m, d), F32),
            jax.ShapeDtypeStruct(state.shape, F32),
        ],
        scratch_shapes=[pltpu.VMEM((tm, d), BF16), pltpu.VMEM((2, d), F32)],
        compiler_params=_params("arbitrary", "arbitrary"),
        name="conv_mixer",
    )(x, norm_g, norm_g, w_in, w_in, w_in, conv_w, w_out, state)


def _rope_col(xc, cos, sin_signed):
    lane = lax.broadcasted_iota(jnp.int32, xc.shape, 1)
    first_half = (lane % HEAD_DIM) < HEAD_DIM // 2
    swapped = jnp.where(first_half,
                        pltpu.roll(xc, LANES - HEAD_DIM // 2, 1),
                        pltpu.roll(xc, HEAD_DIM // 2, 1))
    return xc * cos + swapped * sin_signed


def _qkv_kernel(x_ref, g_ref, w_ref, cos_ref, sin_ref, q_ref, k_ref, v_ref, *, chunk):
    n_q = q_ref.shape[0] * LANES
    kv_w = k_ref.shape[1]
    h = _rms(x_ref[...], g_ref[...]).astype(BF16)
    cos = cos_ref[...]
    sin = sin_ref[...]
    for c0 in range(0, n_q, chunk):
        y = jnp.dot(h, w_ref[:, c0:c0 + chunk], preferred_element_type=F32)
        for c in range(chunk // LANES):
            qc = _rope_col(y[:, c * LANES:(c + 1) * LANES], cos, sin) * (HEAD_DIM ** -0.5)
            q_ref[c0 // LANES + c] = qc.astype(BF16)
    y = jnp.dot(h, w_ref[:, n_q:], preferred_element_type=F32)
    for c in range(kv_w // LANES):
        k_ref[:, c * LANES:(c + 1) * LANES] = _rope_col(y[:, c * LANES:(c + 1) * LANES], cos, sin)
    v_ref[...] = y[:, kv_w:]


def _qkv_project(x, norm_g, w_qkv, cos, sin, layer, j, *, tm, n_heads, n_kv):
    m, d = x.shape
    kv_w = n_kv * HEAD_DIM
    n_cols = n_heads // HEADS_PER_COL
    return pl.pallas_call(
        functools.partial(_qkv_kernel, chunk=2 * kv_w),
        grid=(m // tm,),
        in_specs=[
            pl.BlockSpec((tm, d), lambda i: (i, 0)),
            pl.BlockSpec((None, None, 1, d), lambda i: (layer, 2, 0, 0)),
            pl.BlockSpec((None, d, w_qkv.shape[2]), lambda i: (j, 0, 0)),
            pl.BlockSpec((tm, LANES), lambda i: (i, 0)),
            pl.BlockSpec((tm, LANES), lambda i: (i, 0)),
        ],
        out_specs=[
            pl.BlockSpec((n_cols, tm, LANES), lambda i: (0, i, 0)),
            pl.BlockSpec((tm, kv_w), lambda i: (i, 0)),
            pl.BlockSpec((tm, kv_w), lambda i: (i, 0)),
        ],
        out_shape=[
            jax.ShapeDtypeStruct((n_cols, m, LANES), BF16),
            jax.ShapeDtypeStruct((m, kv_w), F32),
            jax.ShapeDtypeStruct((m, kv_w), F32),
        ],
        compiler_params=_params("parallel"),
        name="qkv_rope",
    )(x, norm_g, w_qkv, cos, sin)


def _block_diag(col, rolled, first):
    lane = lax.broadcasted_iota(jnp.int32, col.shape, 1)
    lo = lane < HEAD_DIM
    zero = jnp.zeros_like(col)
    if first:
        return jnp.concatenate([jnp.where(lo, col, zero), jnp.where(lo, zero, rolled)], axis=0)
    return jnp.concatenate([jnp.where(lo, rolled, zero), jnp.where(lo, zero, col)], axis=0)


def _attn_kernel(sink_ref, q_ref, kp_ref, kc_ref, vp_ref, vc_ref, x_ref, g_ref, wo_ref, o_ref,
                 att_ref, *, sq, n_kv, streaming, lag, q_pos0, k_pos0):
    i = pl.program_id(0)
    t = jnp.minimum(i, pl.num_programs(0) - 1 - lag)
    cur = lax.rem(i, 2) if lag else 0
    tq, d = x_ref.shape
    nsub = tq // sq
    nk_real = WINDOW + sq
    nk = -(-nk_real // LANES) * LANES
    assert nk == nk_real or not streaming
    n_cols = q_ref.shape[0]
    cols_per_kv = n_cols // n_kv
    n_blocks = nsub * n_kv
    blocks_per_chunk = n_blocks // (d // ATTN_OUT_COLS)

    if lag:
        @pl.when(i == 0)
        def _():
            att_ref[1] = jnp.zeros((tq, d), BF16)

    if streaming:
        k_all = jnp.concatenate([kp_ref[...], kc_ref[...]], axis=0)
        v_all = jnp.concatenate([vp_ref[...], vc_ref[...]], axis=0)

    row = lax.broadcasted_iota(jnp.int32, (sq, nk), 0)
    colk = lax.broadcasted_iota(jnp.int32, (sq, nk), 1)

    @functools.cache
    def sub_tile(s):
        if streaming:
            q0 = t * tq + s * sq
            k0 = q0 - WINDOW
            ks = k_all[s * sq:s * sq + nk]
            vs = v_all[s * sq:s * sq + nk]
        else:
            q0 = q_pos0
            k0 = k_pos0
            pad = [jnp.zeros((nk - nk_real, kp_ref.shape[2]), F32)] if nk > nk_real else []
            ks = jnp.concatenate([kp_ref[s], kc_ref[s * sq:(s + 1) * sq]] + pad, axis=0)
            vs = jnp.concatenate([vp_ref[s], vc_ref[s * sq:(s + 1) * sq]] + pad, axis=0)
        kpos = k0 + colk
        q_chunk = lax.shift_right_arithmetic(q0 + row, CHUNK_SHIFT)
        k_chunk = lax.shift_right_arithmetic(kpos, CHUNK_SHIFT)
        valid = (k_chunk >= q_chunk - LOOKBACK_CHUNKS) & (k_chunk <= q_chunk)
        if streaming:
            valid = valid & (kpos >= 0)
        if nk > nk_real:
            valid = valid & (colk < nk_real)
        return ks, vs, valid

    def scores(s, kv):
        ks, _, _ = sub_tile(s)
        c, first = divmod(kv, HEADS_PER_COL)
        kcol = ks[:, c * LANES:(c + 1) * LANES]
        bdk = _block_diag(kcol, pltpu.roll(kcol, HEAD_DIM, 1), first == 0).astype(BF16)
        qs = q_ref[kv * cols_per_kv:(kv + 1) * cols_per_kv, s * sq:(s + 1) * sq, :]
        qs = qs.reshape(cols_per_kv * sq, LANES)
        return lax.dot_general(qs, bdk, (((1,), (1,)), ((), ())), preferred_element_type=F32)

    def attend(s, kv, sc):
        _, vs, valid = sub_tile(s)
        c, first = divmod(kv, HEADS_PER_COL)
        vcol = vs[:, c * LANES:(c + 1) * LANES]
        bdv = _block_diag(vcol, pltpu.roll(vcol, HEAD_DIM, 1), first == 0).astype(BF16)
        p_rows = []
        for p in range(cols_per_kv):
            halves = []
            for e in range(HEADS_PER_COL):
                sink = sink_ref[(kv * cols_per_kv + p) * HEADS_PER_COL + e]
                blk = sc[p * sq:(p + 1) * sq, e * nk:(e + 1) * nk]
                blk = jnp.where(valid, blk, -jnp.inf)
                mx = jnp.maximum(jnp.max(blk, axis=-1, keepdims=True), sink)
                ex = jnp.exp(blk - mx)
                den = jnp.sum(ex, axis=-1, keepdims=True) + jnp.exp(sink - mx)
                halves.append((ex * (1.0 / den)).astype(BF16))
            p_rows.append(jnp.concatenate(halves, axis=1))
        pm = jnp.concatenate(p_rows, axis=0)
        ov = jnp.dot(pm, bdv, preferred_element_type=F32)
        for p in range(cols_per_kv):
            cc = kv * cols_per_kv + p
            att_ref[cur, s * sq:(s + 1) * sq, cc * LANES:(cc + 1) * LANES] = (
                ov[p * sq:(p + 1) * sq].astype(BF16))

    blocks = [(s, kv) for s in range(nsub) for kv in range(n_kv)]
    sc_next = scores(*blocks[0])
    for b, (s, kv) in enumerate(blocks):
        sc = sc_next
        if b + 1 < n_blocks:
            sc_next = scores(*blocks[b + 1])
        if lag and (b + 1) % blocks_per_chunk == 0:
            oc = ((b + 1) // blocks_per_chunk - 1) * ATTN_OUT_COLS
            o_ref[:, oc:oc + ATTN_OUT_COLS] = jnp.dot(
                att_ref[1 - cur], wo_ref[:, oc:oc + ATTN_OUT_COLS], preferred_element_type=F32)
        attend(s, kv, sc)

    if not lag:
        o_ref[...] = jnp.dot(att_ref[0], wo_ref[...], preferred_element_type=F32)
    o_ref[...] = x_ref[...] + _rms(o_ref[...], g_ref[...])


def _attention(x, q, k_prev, k, v_prev, v, sinks, norm_g, w_o, layer, j, *, tq, sq, n_kv,
               streaming):
    m, d = x.shape
    n_cols = q.shape[0]
    kv_w = k.shape[1]
    nsub = tq // sq
    n_tiles = m // tq
    lag = int(n_tiles >= ATTN_MIN_LAGGED_TILES)
    att_tile = lambda i: jnp.minimum(i, n_tiles - 1)
    out_tile = lambda i: jnp.maximum(i - lag, 0)
    if streaming:
        per = tq // WINDOW
        prev_spec = pl.BlockSpec((WINDOW, kv_w), lambda i: (jnp.maximum(att_tile(i) * per - 1, 0), 0))
    else:
        prev_spec = pl.BlockSpec((nsub, WINDOW, kv_w), lambda i: (att_tile(i), 0, 0))
    cur_spec = pl.BlockSpec((tq, kv_w), lambda i: (att_tile(i), 0))
    kern = functools.partial(_attn_kernel, sq=sq, n_kv=n_kv, streaming=streaming, lag=lag,
                             q_pos0=PAST_LEN, k_pos0=PAST_LEN - WINDOW)
    return pl.pallas_call(
        kern,
        grid=(n_tiles + lag,),
        in_specs=[
            pl.BlockSpec(memory_space=pltpu.SMEM),
            pl.BlockSpec((n_cols, tq, LANES), lambda i: (0, att_tile(i), 0)),
            prev_spec, cur_spec, prev_spec, cur_spec,
            pl.BlockSpec((tq, d), lambda i: (out_tile(i), 0)),
            pl.BlockSpec((None, None, 1, d), lambda i: (layer, 3, 0, 0)),
            pl.BlockSpec((None, d, d), lambda i: (j, 0, 0)),
        ],
        out_specs=pl.BlockSpec((tq, d), lambda i: (out_tile(i), 0)),
        out_shape=jax.ShapeDtypeStruct((m, d), F32),
        scratch_shapes=[pltpu.VMEM((1 + lag, tq, d), BF16)],
        compiler_params=_params("arbitrary"),
        name="swa_attention",
    )(sinks, q, k_prev, k, v_prev, v, x, norm_g, w_o)


def _rope_tables(pos):
    half = HEAD_DIM // 2
    inv_freq = ROPE_THETA ** (-jnp.arange(half, dtype=F32) / half)
    inv_freq_lanes = jnp.tile(inv_freq, LANES // half)
    sign_lanes = jnp.tile(jnp.concatenate([-jnp.ones(half, F32), jnp.ones(half, F32)]), HEADS_PER_COL)
    ang = pos.astype(F32)[:, None] * inv_freq_lanes[None, :]
    return jnp.cos(ang), sign_lanes[None, :] * jnp.sin(ang)


def kernel(x_prompt, x_sample, state_conv, cache_k, cache_v, norm_g, w_ffn_in, w_ffn_out,
           w_conv_in, w_conv, w_conv_out, w_qkv, w_attn_out, attn_sinks):
    batch, seq, d = x_prompt.shape
    dec_batch, dec_seq, _ = x_sample.shape
    depth = norm_g.shape[0]
    n_heads = attn_sinks.shape[1]
    n_kv = cache_k.shape[3]
    kv_w = n_kv * HEAD_DIM
    assert batch == 1 and dec_seq == CHUNK and cache_k.shape[2] == WINDOW

    g4 = norm_g.reshape(depth, norm_g.shape[1], 1, d)
    w_conv_in_b = w_conv_in.astype(BF16)
    w_conv_out_b = w_conv_out.astype(BF16)
    w_qkv_b = w_qkv.astype(BF16)
    w_attn_out_b = w_attn_out.astype(BF16)

    xp = x_prompt.reshape(seq, d)
    xs = x_sample.reshape(dec_batch * dec_seq, d)
    cos_p, sin_p = _rope_tables(jnp.arange(seq))
    cos_s, sin_s = _rope_tables(jnp.tile(PAST_LEN + jnp.arange(dec_seq), dec_batch))

    tm = ROW_TILE
    conv_p, conv_s, k_p, v_p, k_s, v_s = [], [], [], [], [], []
    for layer in range(depth):
        j = layer // 2
        def ffn_pair(xp, xs, slot, n_pre, n_post):
            ffn = functools.partial(_ffn, norm_g=g4, layer=layer, n_pre=n_pre, n_post=n_post, tm=FFN_TM)
            xs, bf16_w = ffn(xs, weights=(w_ffn_in, w_ffn_out, slot), tf=FFN_TF_F32, tile_cols=FFN_TF)
            return ffn(xp, weights=bf16_w, tf=FFN_TF), xs

        xp, xs = ffn_pair(xp, xs, 0, 0, 1)
        if layer % 2 == 0:
            mix = functools.partial(_conv_mixer, norm_g=g4, w_in=w_conv_in_b, conv_w=w_conv,
                                    w_out=w_conv_out_b, layer=layer, j=j, tm=tm)
            xp, st = mix(xp, jnp.zeros((1, 2, d), F32), seg=tm, carried=True, tn=d)
            conv_p.append(st)
            xs, st = mix(xs, state_conv[j], seg=dec_seq, carried=False, tn=CONV_TN)
            conv_s.append(st)
        else:
            proj = functools.partial(_qkv_project, norm_g=g4, w_qkv=w_qkv_b, layer=layer, j=j,
                                     tm=tm, n_heads=n_heads, n_kv=n_kv)
            att = functools.partial(_attention, sinks=attn_sinks[j], norm_g=g4, w_o=w_attn_out_b,
                                    layer=layer, j=j, n_kv=n_kv)
            q, k, v = proj(xp, cos=cos_p, sin=sin_p)
            xp = att(xp, q, k, k, v, v, tq=tm, sq=WINDOW, streaming=True)
            k_p.append(k[seq - WINDOW:].reshape(1, WINDOW, n_kv, HEAD_DIM))
            v_p.append(v[seq - WINDOW:].reshape(1, WINDOW, n_kv, HEAD_DIM))
            q, k, v = proj(xs, cos=cos_s, sin=sin_s)
            ck = cache_k[j].reshape(dec_batch, WINDOW, kv_w)
            cv = cache_v[j].reshape(dec_batch, WINDOW, kv_w)
            xs = att(xs, q, ck, k, cv, v, tq=tm, sq=dec_seq, streaming=False)
            k_new = jnp.concatenate([ck, k.reshape(dec_batch, dec_seq, kv_w)], axis=1)[:, -WINDOW:]
            v_new = jnp.concatenate([cv, v.reshape(dec_batch, dec_seq, kv_w)], axis=1)[:, -WINDOW:]
            k_s.append(k_new.reshape(dec_batch, WINDOW, n_kv, HEAD_DIM))
            v_s.append(v_new.reshape(dec_batch, WINDOW, n_kv, HEAD_DIM))
        xp, xs = ffn_pair(xp, xs, 1, 4, 5)

    return (xp.reshape(batch, seq, d), xs.reshape(dec_batch, dec_seq, d),
            jnp.stack(conv_p), jnp.stack(conv_s),
            jnp.stack(k_p), jnp.stack(v_p), jnp.stack(k_s), jnp.stack(v_s))
```

```python
import functools

import jax
import jax.numpy as jnp
from jax import lax
from jax.experimental import pallas as pl
from jax.experimental.pallas import tpu as pltpu

F32 = jnp.float32
BF16 = jnp.bfloat16

CHUNK = 64
CHUNK_SHIFT = CHUNK.bit_length() - 1
assert 1 << CHUNK_SHIFT == CHUNK
LOOKBACK_CHUNKS = 2
WINDOW = LOOKBACK_CHUNKS * CHUNK
HEAD_DIM = 64
PAST_LEN = 2048
ROPE_THETA = 10000.0
NORM_EPS = 1e-6

LANES = 128
HEADS_PER_COL = LANES // HEAD_DIM
VMEM_LIMIT = 62 * 1024 * 1024
ROW_TILE = 512
FFN_TM = 1024
FFN_ROWS = 512
FFN_POST_ROWS = 256
FFN_TF = 512
FFN_TF_F32 = 256
QKV_TM = 1024
CONV_TN = 512
CONV_COLS = 256
ATTN_OUT_COLS = 256
ATTN_MIN_LAGGED_TILES = 4


def _rms(x, g):
    ms = jnp.mean(x * x, axis=-1, keepdims=True)
    return x * lax.rsqrt(ms + NORM_EPS) * g


def _params(*sem):
    return pltpu.CompilerParams(dimension_semantics=sem, vmem_limit_bytes=VMEM_LIMIT)


def _ffn_kernel(x_ref, gpre_ref, gpost_ref, wg_ref, wu_ref, wo_ref, o_ref, *rest, cast_own):
    own_out, h_ref = rest[:-1], rest[-1]
    j = pl.program_id(1)
    last = pl.num_programs(1) - 1

    def step(first, final):
        wg, wu, wo = wg_ref[...], wu_ref[...], wo_ref[...]
        if cast_own:
            wg, wu, wo = wg.astype(BF16), wu.astype(BF16), wo.astype(BF16)
            for w, dst in zip((wg, wu, wo), own_out):
                dst[...] = w
        group = FFN_ROWS if first else x_ref.shape[0]
        for r in range(0, x_ref.shape[0], group):
            rows = slice(r, r + group)
            if first:
                h = _rms(x_ref[rows], gpre_ref[...]).astype(BF16)
                h_ref[rows] = h
            else:
                h = h_ref[rows]
            gate = jnp.dot(h, wg, preferred_element_type=F32)
            up = jnp.dot(h, wu, preferred_element_type=F32)
            act = (gate * jax.nn.sigmoid(gate) * up).astype(BF16)
            if final:
                for h0 in range(0, group, FFN_POST_ROWS):
                    sub = slice(r + h0, r + h0 + FFN_POST_ROWS)
                    part = jnp.dot(act[h0:h0 + FFN_POST_ROWS], wo, preferred_element_type=F32)
                    o_ref[sub] = x_ref[sub] + 0.5 * _rms(o_ref[sub] + part, gpost_ref[...])
                continue
            part = jnp.dot(act, wo, preferred_element_type=F32)
            if first:
                o_ref[rows] = part
            else:
                o_ref[rows] += part

    pl.when(j == 0)(functools.partial(step, True, False))
    pl.when((j > 0) & (j < last))(functools.partial(step, False, False))
    pl.when(j == last)(functools.partial(step, False, True))


def _ffn(x, norm_g, weights, layer, n_pre, n_post, *, tm, tf, tile_cols=None):
    m, d = x.shape
    cast_own = weights[0].dtype == F32
    g_spec = lambda n: pl.BlockSpec((None, None, 1, d), lambda i, j: (layer, n, 0, 0))
    out_specs = [pl.BlockSpec((tm, d), lambda i, j: (i, 0))]
    out_shape = [jax.ShapeDtypeStruct((m, d), F32)]
    if cast_own:
        w_in, w_out, slot = weights
        f = w_out.shape[2]
        nf, per_tile = f // tf, tile_cols // tf
        assert m == tm and per_tile * tf == tile_cols
        w_specs = [
            pl.BlockSpec((None, None, d, tf), lambda i, j: (layer, slot, 0, j)),
            pl.BlockSpec((None, None, d, tf), lambda i, j: (layer, slot, 0, j + nf)),
            pl.BlockSpec((None, None, tf, d), lambda i, j: (layer, slot, j, 0)),
        ]
        operands = [w_in, w_in, w_out]
        tile_spec = pl.BlockSpec((None, d, tf), lambda i, j: (j // per_tile, 0, j % per_tile))
        out_specs += [tile_spec, tile_spec, pl.BlockSpec((tf, d), lambda i, j: (j, 0))]
        tiles = jax.ShapeDtypeStruct((f // tile_cols, d, tile_cols), BF16)
        out_shape += [tiles, tiles, jax.ShapeDtypeStruct((f, d), BF16)]
    else:
        gate, up, down = operands = list(weights)
        f = down.shape[0]
        assert gate.shape[2] == tf
        tile_spec = pl.BlockSpec((None, d, tf), lambda i, j: (j, 0, 0))
        w_specs = [tile_spec, tile_spec, pl.BlockSpec((tf, d), lambda i, j: (j, 0))]
    outs = pl.pallas_call(
        functools.partial(_ffn_kernel, cast_own=cast_own),
        grid=(m // tm, f // tf),
        in_specs=[pl.BlockSpec((tm, d), lambda i, j: (i, 0)), g_spec(n_pre), g_spec(n_post)] + w_specs,
        out_specs=out_specs,
        out_shape=out_shape,
        scratch_shapes=[pltpu.VMEM((tm, d), BF16)],
        compiler_params=_params("parallel", "arbitrary"),
        name="ffn",
    )(x, norm_g, norm_g, *operands)
    return (outs[0], tuple(outs[1:])) if cast_own else outs[0]


def _conv_kernel(x_ref, gpre_ref, gpost_ref, wb_ref, wc_ref, wh_ref, cw_ref, wo_ref, st_ref,
                 o_ref, ns_ref, h_ref, carry_ref, *, seg, carried):
    i = pl.program_id(0)
    n = pl.program_id(1)
    last = pl.num_programs(1) - 1
    tm = x_ref.shape[0]
    tn = wb_ref.shape[1]
    tc = CONV_COLS
    cols = pl.ds(pl.multiple_of(n * tn, tn), tn)

    if carried:
        @pl.when(i == 0)
        def _():
            carry_ref[:, cols] = st_ref[0, :, cols]

    def gated_conv(c0, b_gate, c_gate, hh):
        gcols = pl.ds(pl.multiple_of(n * tn + c0, tc), tc)
        cw = cw_ref[:, c0:c0 + tc]
        w0, w1, w2 = cw[0:1], cw[1:2], cw[2:3]
        row = lax.broadcasted_iota(jnp.int32, (seg, tc), 0)
        u = c_gate * hh
        convs = []
        for s in range(tm // seg):
            us = u[s * seg:(s + 1) * seg]
            prev = carry_ref[:, gcols] if carried else st_ref[s, :, c0:c0 + tc]
            p1 = jnp.where(row == 0, prev[1:2], pltpu.roll(us, 1, 0))
            p2 = jnp.where(row == 0, prev[0:1], jnp.where(row == 1, prev[1:2], pltpu.roll(us, 2, 0)))
            convs.append(p2 * w0 + p1 * w1 + us * w2)
            if carried:
                carry_ref[:, gcols] = us[seg - 2:seg]
                ns_ref[0, :, gcols] = us[seg - 2:seg]
            else:
                ns_ref[s, :, c0:c0 + tc] = us[seg - 2:seg]
        conv = convs[0] if len(convs) == 1 else jnp.concatenate(convs, axis=0)
        return (b_gate * conv).astype(BF16)

    def out_proj(assign, final, c0, z):
        part = jnp.dot(z, wo_ref[c0:c0 + tc, :], preferred_element_type=F32)
        if assign:
            o_ref[...] = part
        elif final:
            o_ref[...] = x_ref[...] + _rms(o_ref[...] + part, gpost_ref[...])
        else:
            o_ref[...] += part

    def step(first, final):
        if first:
            h = _rms(x_ref[...], gpre_ref[...]).astype(BF16)
            h_ref[...] = h
        else:
            h = h_ref[...]
        pending = None
        for c0 in range(0, tn, tc):
            b_gate = jnp.dot(h, wb_ref[:, c0:c0 + tc], preferred_element_type=F32)
            c_gate = jnp.dot(h, wc_ref[:, c0:c0 + tc], preferred_element_type=F32)
            hh = jnp.dot(h, wh_ref[:, c0:c0 + tc], preferred_element_type=F32)
            if pending is not None:
                out_proj(first and pending[0] == 0, False, *pending)
            pending = (c0, gated_conv(c0, b_gate, c_gate, hh))
        out_proj(first and pending[0] == 0, final, *pending)

    if tn == o_ref.shape[1]:
        step(True, True)
    else:
        pl.when(n == 0)(functools.partial(step, True, False))
        pl.when((n > 0) & (n < last))(functools.partial(step, False, False))
        pl.when(n == last)(functools.partial(step, False, True))


def _conv_mixer(x, state, norm_g, w_in, conv_w, w_out, layer, j, *, tm, tn, seg, carried):
    m, d = x.shape
    nn = d // tn
    nseg = tm // seg
    g_spec = lambda n: pl.BlockSpec((None, None, 1, d), lambda i, c: (layer, n, 0, 0))
    once = dict(pipeline_mode=pl.Buffered(1)) if nn == 1 else {}
    w_spec = lambda sec: pl.BlockSpec((None, d, tn), lambda i, c: (j, 0, c + sec * nn), **once)
    if carried:
        st_spec = pl.BlockSpec((1, 2, d), lambda i, c: (0, 0, 0))
    else:
        st_spec = pl.BlockSpec((nseg, 2, tn), lambda i, c: (i, 0, c))
    return pl.pallas_call(
        functools.partial(_conv_kernel, seg=seg, carried=carried),
        grid=(m // tm, nn),
        in_specs=[
            pl.BlockSpec((tm, d), lambda i, c: (i, 0)),
            g_spec(2),
            g_spec(3),
            w_spec(0),
            w_spec(1),
            w_spec(2),
            pl.BlockSpec((None, 3, tn), lambda i, c: (j, 0, c)),
            pl.BlockSpec((None, tn, d), lambda i, c: (j, c, 0), **once),
            st_spec,
        ],
        out_specs=[pl.BlockSpec((tm, d), lambda i, c: (i, 0)), st_spec],
        out_shape=[
            jax.ShapeDtypeStruct((m, d), F32),
            jax.ShapeDtypeStruct(state.shape, F32),
        ],
        scratch_shapes=[pltpu.VMEM((tm, d), BF16), pltpu.VMEM((2, d), F32)],
        compiler_params=_params("arbitrary", "arbitrary"),
        name="conv_mixer",
    )(x, norm_g, norm_g, w_in, w_in, w_in, conv_w, w_out, state)


def _rope_col(xc, cos, sin_signed):
    lane = lax.broadcasted_iota(jnp.int32, xc.shape, 1)
    first_half = (lane % HEAD_DIM) < HEAD_DIM // 2
    swapped = jnp.where(first_half,
                        pltpu.roll(xc, LANES - HEAD_DIM // 2, 1),
                        pltpu.roll(xc, HEAD_DIM // 2, 1))
    return xc * cos + swapped * sin_signed


def _qkv_kernel(x_ref, g_ref, w_ref, cos_ref, sin_ref, q_ref, k_ref, v_ref, *, chunk):
    n_q = q_ref.shape[0] * LANES
    kv_w = k_ref.shape[1]
    h = _rms(x_ref[...], g_ref[...]).astype(BF16)
    cos = cos_ref[...]
    sin = sin_ref[...]
    for c0 in range(0, n_q, chunk):
        y = jnp.dot(h, w_ref[:, c0:c0 + chunk], preferred_element_type=F32)
        for c in range(chunk // LANES):
            qc = _rope_col(y[:, c * LANES:(c + 1) * LANES], cos, sin) * (HEAD_DIM ** -0.5)
            q_ref[c0 // LANES + c] = qc.astype(BF16)
    y = jnp.dot(h, w_ref[:, n_q:], preferred_element_type=F32)
    for c in range(kv_w // LANES):
        k_ref[:, c * LANES:(c + 1) * LANES] = _rope_col(y[:, c * LANES:(c + 1) * LANES], cos, sin)
    v_ref[...] = y[:, kv_w:]


def _qkv_project(x, norm_g, w_qkv, cos, sin, layer, j, *, tm, n_heads, n_kv):
    m, d = x.shape
    kv_w = n_kv * HEAD_DIM
    n_cols = n_heads // HEADS_PER_COL
    return pl.pallas_call(
        functools.partial(_qkv_kernel, chunk=2 * kv_w),
        grid=(m // tm,),
        in_specs=[
            pl.BlockSpec((tm, d), lambda i: (i, 0)),
            pl.BlockSpec((None, None, 1, d), lambda i: (layer, 2, 0, 0)),
            pl.BlockSpec((None, d, w_qkv.shape[2]), lambda i: (j, 0, 0)),
            pl.BlockSpec((tm, LANES), lambda i: (i, 0)),
            pl.BlockSpec((tm, LANES), lambda i: (i, 0)),
        ],
        out_specs=[
            pl.BlockSpec((n_cols, tm, LANES), lambda i: (0, i, 0)),
            pl.BlockSpec((tm, kv_w), lambda i: (i, 0)),
            pl.BlockSpec((tm, kv_w), lambda i: (i, 0)),
        ],
        out_shape=[
            jax.ShapeDtypeStruct((n_cols, m, LANES), BF16),
            jax.ShapeDtypeStruct((m, kv_w), F32),
            jax.ShapeDtypeStruct((m, kv_w), F32),
        ],
        compiler_params=_params("parallel"),
        name="qkv_rope",
    )(x, norm_g, w_qkv, cos, sin)


def _block_diag(col, rolled, first):
    lane = lax.broadcasted_iota(jnp.int32, col.shape, 1)
    lo = lane < HEAD_DIM
    zero = jnp.zeros_like(col)
    if first:
        return jnp.concatenate([jnp.where(lo, col, zero), jnp.where(lo, zero, rolled)], axis=0)
    return jnp.concatenate([jnp.where(lo, rolled, zero), jnp.where(lo, zero, col)], axis=0)


def _attn_kernel(sink_ref, q_ref, kp_ref, kc_ref, vp_ref, vc_ref, x_ref, g_ref, wo_ref, o_ref,
                 att_ref, *, sq, n_kv, streaming, lag, q_pos0, k_pos0):
    i = pl.program_id(0)
    t = jnp.minimum(i, pl.num_programs(0) - 1 - lag)
    cur = lax.rem(i, 2) if lag else 0
    tq, d = x_ref.shape
    nsub = tq // sq
    nk_real = WINDOW + sq
    nk = -(-nk_real // LANES) * LANES
    assert nk == nk_real or not streaming
    n_cols = q_ref.shape[0]
    cols_per_kv = n_cols // n_kv
    n_blocks = nsub * n_kv
    blocks_per_chunk = n_blocks // (d // ATTN_OUT_COLS)

    if lag:
        @pl.when(i == 0)
        def _():
            att_ref[1] = jnp.zeros((tq, d), BF16)

    if streaming:
        k_all = jnp.concatenate([kp_ref[...], kc_ref[...]], axis=0)
        v_all = jnp.concatenate([vp_ref[...], vc_ref[...]], axis=0)

    row = lax.broadcasted_iota(jnp.int32, (sq, nk), 0)
    colk = lax.broadcasted_iota(jnp.int32, (sq, nk), 1)

    @functools.cache
    def sub_tile(s):
        if streaming:
            q0 = t * tq + s * sq
            k0 = q0 - WINDOW
            ks = k_all[s * sq:s * sq + nk]
            vs = v_all[s * sq:s * sq + nk]
        else:
            q0 = q_pos0
            k0 = k_pos0
            pad = [jnp.zeros((nk - nk_real, kp_ref.shape[2]), F32)] if nk > nk_real else []
            ks = jnp.concatenate([kp_ref[s], kc_ref[s * sq:(s + 1) * sq]] + pad, axis=0)
            vs = jnp.concatenate([vp_ref[s], vc_ref[s * sq:(s + 1) * sq]] + pad, axis=0)
        kpos = k0 + colk
        q_chunk = lax.shift_right_arithmetic(q0 + row, CHUNK_SHIFT)
        k_chunk = lax.shift_right_arithmetic(kpos, CHUNK_SHIFT)
        valid = (k_chunk >= q_chunk - LOOKBACK_CHUNKS) & (k_chunk <= q_chunk)
        if streaming:
            valid = valid & (kpos >= 0)
        if nk > nk_real:
            valid = valid & (colk < nk_real)
        return ks, vs, valid

    def scores(s, kv):
        ks, _, _ = sub_tile(s)
        c, first = divmod(kv, HEADS_PER_COL)
        kcol = ks[:, c * LANES:(c + 1) * LANES]
        bdk = _block_diag(kcol, pltpu.roll(kcol, HEAD_DIM, 1), first == 0).astype(BF16)
        qs = q_ref[kv * cols_per_kv:(kv + 1) * cols_per_kv, s * sq:(s + 1) * sq, :]
        qs = qs.reshape(cols_per_kv * sq, LANES)
        return lax.dot_general(qs, bdk, (((1,), (1,)), ((), ())), preferred_element_type=F32)

    def attend(s, kv, sc):
        _, vs, valid = sub_tile(s)
        c, first = divmod(kv, HEADS_PER_COL)
        vcol = vs[:, c * LANES:(c + 1) * LANES]
        bdv = _block_diag(vcol, pltpu.roll(vcol, HEAD_DIM, 1), first == 0).astype(BF16)
        p_rows = []
        for p in range(cols_per_kv):
            halves = []
            for e in range(HEADS_PER_COL):
                sink = sink_ref[(kv * cols_per_kv + p) * HEADS_PER_COL + e]
                blk = sc[p * sq:(p + 1) * sq, e * nk:(e + 1) * nk]
                blk = jnp.where(valid, blk, -jnp.inf)
                mx = jnp.maximum(jnp.max(blk, axis=-1, keepdims=True), sink)
                ex = jnp.exp(blk - mx)
                den = jnp.sum(ex, axis=-1, keepdims=True) + jnp.exp(sink - mx)
                halves.append((ex * (1.0 / den)).astype(BF16))
            p_rows.append(jnp.concatenate(halves, axis=1))
        pm = jnp.concatenate(p_rows, axis=0)
        ov = jnp.dot(pm, bdv, preferred_element_type=F32)
        for p in range(cols_per_kv):
            cc = kv * cols_per_kv + p
            att_ref[cur, s * sq:(s + 1) * sq, cc * LANES:(cc + 1) * LANES] = (
                ov[p * sq:(p + 1) * sq].astype(BF16))

    blocks = [(s, kv) for s in range(nsub) for kv in range(n_kv)]
    sc_next = scores(*blocks[0])
    for b, (s, kv) in enumerate(blocks):
        sc = sc_next
        if b + 1 < n_blocks:
            sc_next = scores(*blocks[b + 1])
        if lag and (b + 1) % blocks_per_chunk == 0:
            oc = ((b + 1) // blocks_per_chunk - 1) * ATTN_OUT_COLS
            o_ref[:, oc:oc + ATTN_OUT_COLS] = jnp.dot(
                att_ref[1 - cur], wo_ref[:, oc:oc + ATTN_OUT_COLS], preferred_element_type=F32)
        attend(s, kv, sc)

    if not lag:
        o_ref[...] = jnp.dot(att_ref[0], wo_ref[...], preferred_element_type=F32)
    o_ref[...] = x_ref[...] + _rms(o_ref[...], g_ref[...])


def _attention(x, q, k_prev, k, v_prev, v, sinks, norm_g, w_o, layer, j, *, tq, sq, n_kv,
               streaming):
    m, d = x.shape
    n_cols = q.shape[0]
    kv_w = k.shape[1]
    nsub = tq // sq
    n_tiles = m // tq
    lag = int(n_tiles >= ATTN_MIN_LAGGED_TILES)
    att_tile = lambda i: jnp.minimum(i, n_tiles - 1)
    out_tile = lambda i: jnp.maximum(i - lag, 0)
    if streaming:
        per = tq // WINDOW
        prev_spec = pl.BlockSpec((WINDOW, kv_w), lambda i: (jnp.maximum(att_tile(i) * per - 1, 0), 0))
    else:
        prev_spec = pl.BlockSpec((nsub, WINDOW, kv_w), lambda i: (att_tile(i), 0, 0))
    cur_spec = pl.BlockSpec((tq, kv_w), lambda i: (att_tile(i), 0))
    kern = functools.partial(_attn_kernel, sq=sq, n_kv=n_kv, streaming=streaming, lag=lag,
                             q_pos0=PAST_LEN, k_pos0=PAST_LEN - WINDOW)
    return pl.pallas_call(
        kern,
        grid=(n_tiles + lag,),
        in_specs=[
            pl.BlockSpec(memory_space=pltpu.SMEM),
            pl.BlockSpec((n_cols, tq, LANES), lambda i: (0, att_tile(i), 0)),
            prev_spec, cur_spec, prev_spec, cur_spec,
            pl.BlockSpec((tq, d), lambda i: (out_tile(i), 0)),
            pl.BlockSpec((None, None, 1, d), lambda i: (layer, 3, 0, 0)),
            pl.BlockSpec((None, d, d), lambda i: (j, 0, 0)),
        ],
        out_specs=pl.BlockSpec((tq, d), lambda i: (out_tile(i), 0)),
        out_shape=jax.ShapeDtypeStruct((m, d), F32),
        scratch_shapes=[pltpu.VMEM((1 + lag, tq, d), BF16)],
        compiler_params=_params("arbitrary"),
        name="swa_attention",
    )(sinks, q, k_prev, k, v_prev, v, x, norm_g, w_o)


def _rope_tables(pos):
    half = HEAD_DIM // 2
    inv_freq = ROPE_THETA ** (-jnp.arange(half, dtype=F32) / half)
    inv_freq_lanes = jnp.tile(inv_freq, LANES // half)
    sign_lanes = jnp.tile(jnp.concatenate([-jnp.ones(half, F32), jnp.ones(half, F32)]), HEADS_PER_COL)
    ang = pos.astype(F32)[:, None] * inv_freq_lanes[None, :]
    return jnp.cos(ang), sign_lanes[None, :] * jnp.sin(ang)


def kernel(x_prompt, x_sample, state_conv, cache_k, cache_v, norm_g, w_ffn_in, w_ffn_out,
           w_conv_in, w_conv, w_conv_out, w_qkv, w_attn_out, attn_sinks):
    batch, seq, d = x_prompt.shape
    dec_batch, dec_seq, _ = x_sample.shape
    depth = norm_g.shape[0]
    n_heads = attn_sinks.shape[1]
    n_kv = cache_k.shape[3]
    kv_w = n_kv * HEAD_DIM
    assert batch == 1 and dec_seq == CHUNK and cache_k.shape[2] == WINDOW

    g4 = norm_g.reshape(depth, norm_g.shape[1], 1, d)
    w_conv_in_b = w_conv_in.astype(BF16)
    w_conv_out_b = w_conv_out.astype(BF16)
    w_qkv_b = w_qkv.astype(BF16)
    w_attn_out_b = w_attn_out.astype(BF16)

    xp = x_prompt.reshape(seq, d)
    xs = x_sample.reshape(dec_batch * dec_seq, d)
    cos_p, sin_p = _rope_tables(jnp.arange(seq))
    cos_s, sin_s = _rope_tables(jnp.tile(PAST_LEN + jnp.arange(dec_seq), dec_batch))

    tm = ROW_TILE
    conv_p, conv_s, k_p, v_p, k_s, v_s = [], [], [], [], [], []
    for layer in range(depth):
        j = layer // 2
        def ffn_pair(xp, xs, slot, n_pre, n_post):
            ffn = functools.partial(_ffn, norm_g=g4, layer=layer, n_pre=n_pre, n_post=n_post, tm=FFN_TM)
            xs, bf16_w = ffn(xs, weights=(w_ffn_in, w_ffn_out, slot), tf=FFN_TF_F32, tile_cols=FFN_TF)
            return ffn(xp, weights=bf16_w, tf=FFN_TF), xs

        xp, xs = ffn_pair(xp, xs, 0, 0, 1)
        if layer % 2 == 0:
            mix = functools.partial(_conv_mixer, norm_g=g4, w_in=w_conv_in_b, conv_w=w_conv,
                                    w_out=w_conv_out_b, layer=layer, j=j, tm=tm)
            xp, st = mix(xp, jnp.zeros((1, 2, d), F32), seg=tm, carried=True, tn=d)
            conv_p.append(st)
            xs, st = mix(xs, state_conv[j], seg=dec_seq, carried=False, tn=CONV_TN)
            conv_s.append(st)
        else:
            proj = functools.partial(_qkv_project, norm_g=g4, w_qkv=w_qkv_b, layer=layer, j=j,
                                     tm=QKV_TM, n_heads=n_heads, n_kv=n_kv)
            att = functools.partial(_attention, sinks=attn_sinks[j], norm_g=g4, w_o=w_attn_out_b,
                                    layer=layer, j=j, n_kv=n_kv)
            q, k, v = proj(xp, cos=cos_p, sin=sin_p)
            xp = att(xp, q, k, k, v, v, tq=tm, sq=WINDOW, streaming=True)
            k_p.append(k[seq - WINDOW:].reshape(1, WINDOW, n_kv, HEAD_DIM))
            v_p.append(v[seq - WINDOW:].reshape(1, WINDOW, n_kv, HEAD_DIM))
            q, k, v = proj(xs, cos=cos_s, sin=sin_s)
            ck = cache_k[j].reshape(dec_batch, WINDOW, kv_w)
            cv = cache_v[j].reshape(dec_batch, WINDOW, kv_w)
            xs = att(xs, q, ck, k, cv, v, tq=tm, sq=dec_seq, streaming=False)
            k_new = jnp.concatenate([ck, k.reshape(dec_batch, dec_seq, kv_w)], axis=1)[:, -WINDOW:]
            v_new = jnp.concatenate([cv, v.reshape(dec_batch, dec_seq, kv_w)], axis=1)[:, -WINDOW:]
            k_s.append(k_new.reshape(dec_batch, WINDOW, n_kv, HEAD_DIM))
            v_s.append(v_new.reshape(dec_batch, WINDOW, n_kv, HEAD_DIM))
        xp, xs = ffn_pair(xp, xs, 1, 4, 5)

    return (xp.reshape(batch, seq, d), xs.reshape(dec_batch, dec_seq, d),
            jnp.stack(conv_p), jnp.stack(conv_s),
            jnp.stack(k_p), jnp.stack(v_p), jnp.stack(k_s), jnp.stack(v_s))
```

```python
import functools

import jax
import jax.numpy as jnp
from jax import lax
from jax.experimental import pallas as pl
from jax.experimental.pallas import tpu as pltpu

F32 = jnp.float32
BF16 = jnp.bfloat16

CHUNK = 64
CHUNK_SHIFT = CHUNK.bit_length() - 1
assert 1 << CHUNK_SHIFT == CHUNK
LOOKBACK_CHUNKS = 2
WINDOW = LOOKBACK_CHUNKS * CHUNK
HEAD_DIM = 64
PAST_LEN = 2048
ROPE_THETA = 10000.0
NORM_EPS = 1e-6

LANES = 128
HEADS_PER_COL = LANES // HEAD_DIM
VMEM_LIMIT = 62 * 1024 * 1024
ROW_TILE = 512
FFN_TM = 1024
FFN_ROWS = 512
FFN_POST_ROWS = 256
FFN_TF = 512
FFN_TF_F32 = 256
CONV_TN = 512
CONV_COLS = 256
ATTN_OUT_COLS = 256
ATTN_MIN_LAGGED_TILES = 4


def _rms(x, g):
    ms = jnp.mean(x * x, axis=-1, keepdims=True)
    return x * lax.rsqrt(ms + NORM_EPS) * g


def _params(*sem):
    return pltpu.CompilerParams(dimension_semantics=sem, vmem_limit_bytes=VMEM_LIMIT)


def _ffn_kernel(x_ref, gpre_ref, gpost_ref, wg_ref, wu_ref, wo_ref, o_ref, *rest, cast_own):
    own_out, h_ref = rest[:-1], rest[-1]
    j = pl.program_id(1)
    last = pl.num_programs(1) - 1

    def step(first, final):
        wg, wu, wo = wg_ref[...], wu_ref[...], wo_ref[...]
        if cast_own:
            wg, wu, wo = wg.astype(BF16), wu.astype(BF16), wo.astype(BF16)
            for w, dst in zip((wg, wu, wo), own_out):
                dst[...] = w
        group = FFN_ROWS if first else x_ref.shape[0]
        for r in range(0, x_ref.shape[0], group):
            rows = slice(r, r + group)
            if first:
                h = _rms(x_ref[rows], gpre_ref[...]).astype(BF16)
                h_ref[rows] = h
            else:
                h = h_ref[rows]
            gate = jnp.dot(h, wg, preferred_element_type=F32)
            up = jnp.dot(h, wu, preferred_element_type=F32)
            act = (gate * jax.nn.sigmoid(gate) * up).astype(BF16)
            if final:
                for h0 in range(0, group, FFN_POST_ROWS):
                    sub = slice(r + h0, r + h0 + FFN_POST_ROWS)
                    part = jnp.dot(act[h0:h0 + FFN_POST_ROWS], wo, preferred_element_type=F32)
                    o_ref[sub] = x_ref[sub] + 0.5 * _rms(o_ref[sub] + part, gpost_ref[...])
                continue
            part = jnp.dot(act, wo, preferred_element_type=F32)
            if first:
                o_ref[rows] = part
            else:
                o_ref[rows] += part

    pl.when(j == 0)(functools.partial(step, True, False))
    pl.when((j > 0) & (j < last))(functools.partial(step, False, False))
    pl.when(j == last)(functools.partial(step, False, True))


def _ffn(x, norm_g, weights, layer, n_pre, n_post, *, tm, tf, tile_cols=None):
    m, d = x.shape
    cast_own = weights[0].dtype == F32
    g_spec = lambda n: pl.BlockSpec((None, None, 1, d), lambda i, j: (layer, n, 0, 0))
    out_specs = [pl.BlockSpec((tm, d), lambda i, j: (i, 0))]
    out_shape = [jax.ShapeDtypeStruct((m, d), F32)]
    if cast_own:
        w_in, w_out, slot = weights
        f = w_out.shape[2]
        nf, per_tile = f // tf, tile_cols // tf
        assert m == tm and per_tile * tf == tile_cols
        w_specs = [
            pl.BlockSpec((None, None, d, tf), lambda i, j: (layer, slot, 0, j)),
            pl.BlockSpec((None, None, d, tf), lambda i, j: (layer, slot, 0, j + nf)),
            pl.BlockSpec((None, None, tf, d), lambda i, j: (layer, slot, j, 0)),
        ]
        operands = [w_in, w_in, w_out]
        tile_spec = pl.BlockSpec((None, d, tf), lambda i, j: (j // per_tile, 0, j % per_tile))
        out_specs += [tile_spec, tile_spec, pl.BlockSpec((tf, d), lambda i, j: (j, 0))]
        tiles = jax.ShapeDtypeStruct((f // tile_cols, d, tile_cols), BF16)
        out_shape += [tiles, tiles, jax.ShapeDtypeStruct((f, d), BF16)]
    else:
        gate, up, down = operands = list(weights)
        f = down.shape[0]
        assert gate.shape[2] == tf
        tile_spec = pl.BlockSpec((None, d, tf), lambda i, j: (j, 0, 0))
        w_specs = [tile_spec, tile_spec, pl.BlockSpec((tf, d), lambda i, j: (j, 0))]
    outs = pl.pallas_call(
        functools.partial(_ffn_kernel, cast_own=cast_own),
        grid=(m // tm, f // tf),
        in_specs=[pl.BlockSpec((tm, d), lambda i, j: (i, 0)), g_spec(n_pre), g_spec(n_post)] + w_specs,
        out_specs=out_specs,
        out_shape=out_shape,
        scratch_shapes=[pltpu.VMEM((tm, d), BF16)],
        compiler_params=_params("parallel", "arbitrary"),
        name="ffn",
    )(x, norm_g, norm_g, *operands)
    return (outs[0], tuple(outs[1:])) if cast_own else outs[0]


def _conv_kernel(x_ref, gpre_ref, gpost_ref, wb_ref, wc_ref, wh_ref, cw_ref, wo_ref, st_ref,
                 o_ref, ns_ref, h_ref, carry_ref, *, seg, carried):
    i = pl.program_id(0)
    n = pl.program_id(1)
    last = pl.num_programs(1) - 1
    tm = x_ref.shape[0]
    tn = wb_ref.shape[1]
    tc = CONV_COLS
    cols = pl.ds(pl.multiple_of(n * tn, tn), tn)

    if carried:
        @pl.when(i == 0)
        def _():
            carry_ref[:, cols] = st_ref[0, :, cols]

    def gated_conv(c0, b_gate, c_gate, hh):
        gcols = pl.ds(pl.multiple_of(n * tn + c0, tc), tc)
        cw = cw_ref[:, c0:c0 + tc]
        w0, w1, w2 = cw[0:1], cw[1:2], cw[2:3]
        row = lax.broadcasted_iota(jnp.int32, (seg, tc), 0)
        u = c_gate * hh
        convs = []
        for s in range(tm // seg):
            us = u[s * seg:(s + 1) * seg]
            prev = carry_ref[:, gcols] if carried else st_ref[s, :, c0:c0 + tc]
            p1 = jnp.where(row == 0, prev[1:2], pltpu.roll(us, 1, 0))
            p2 = jnp.where(row == 0, prev[0:1], jnp.where(row == 1, prev[1:2], pltpu.roll(us, 2, 0)))
            convs.append(p2 * w0 + p1 * w1 + us * w2)
            if carried:
                carry_ref[:, gcols] = us[seg - 2:seg]
                ns_ref[0, :, gcols] = us[seg - 2:seg]
            else:
                ns_ref[s, :, c0:c0 + tc] = us[seg - 2:seg]
        conv = convs[0] if len(convs) == 1 else jnp.concatenate(convs, axis=0)
        return (b_gate * conv).astype(BF16)

    def out_proj(assign, final, c0, z):
        part = jnp.dot(z, wo_ref[c0:c0 + tc, :], preferred_element_type=F32)
        if assign:
            o_ref[...] = part
        elif final:
            o_ref[...] = x_ref[...] + _rms(o_ref[...] + part, gpost_ref[...])
        else:
            o_ref[...] += part

    def step(first, final):
        if first:
            h = _rms(x_ref[...], gpre_ref[...]).astype(BF16)
            h_ref[...] = h
        else:
            h = h_ref[...]
        pending = None
        for c0 in range(0, tn, tc):
            b_gate = jnp.dot(h, wb_ref[:, c0:c0 + tc], preferred_element_type=F32)
            c_gate = jnp.dot(h, wc_ref[:, c0:c0 + tc], preferred_element_type=F32)
            hh = jnp.dot(h, wh_ref[:, c0:c0 + tc], preferred_element_type=F32)
            if pending is not None:
                out_proj(first and pending[0] == 0, False, *pending)
            pending = (c0, gated_conv(c0, b_gate, c_gate, hh))
        out_proj(first and pending[0] == 0, final, *pending)

    if tn == o_ref.shape[1]:
        step(True, True)
    else:
        pl.when(n == 0)(functools.partial(step, True, False))
        pl.when((n > 0) & (n < last))(functools.partial(step, False, False))
        pl.when(n == last)(functools.partial(step, False, True))


def _conv_mixer(x, state, norm_g, w_in, conv_w, w_out, layer, j, *, tm, tn, seg, carried):
    m, d = x.shape
    nn = d // tn
    nseg = tm // seg
    g_spec = lambda n: pl.BlockSpec((None, None, 1, d), lambda i, c: (layer, n, 0, 0))
    once = dict(pipeline_mode=pl.Buffered(1)) if nn == 1 else {}
    w_spec = lambda sec: pl.BlockSpec((None, d, tn), lambda i, c: (j, 0, c + sec * nn), **once)
    if carried:
        st_spec = pl.BlockSpec((1, 2, d), lambda i, c: (0, 0, 0))
    else:
        st_spec = pl.BlockSpec((nseg, 2, tn), lambda i, c: (i, 0, c))
    return pl.pallas_call(
        functools.partial(_conv_kernel, seg=seg, carried=carried),
        grid=(m // tm, nn),
        in_specs=[
            pl.BlockSpec((tm, d), lambda i, c: (i, 0)),
            g_spec(2),
            g_spec(3),
            w_spec(0),
            w_spec(1),
            w_spec(2),
            pl.BlockSpec((None, 3, tn), lambda i, c: (j, 0, c)),
            pl.BlockSpec((None, tn, d), lambda i, c: (j, c, 0), **once),
            st_spec,
        ],
        out_specs=[pl.BlockSpec((tm, d), lambda i, c: (i, 0)), st_spec],
        out_shape=[
            jax.ShapeDtypeStruct((m, d), F32),
            jax.ShapeDtypeStruct(state.shape, F32),
        ],
        scratch_shapes=[pltpu.VMEM((tm, d), BF16), pltpu.VMEM((2, d), F32)],
        compiler_params=_params("arbitrary", "arbitrary"),
        name="conv_mixer",
    )(x, norm_g, norm_g, w_in, w_in, w_in, conv_w, w_out, state)


def _rope_col(xc, cos, sin_signed):
    lane = lax.broadcasted_iota(jnp.int32, xc.shape, 1)
    first_half = (lane % HEAD_DIM) < HEAD_DIM // 2
    swapped = jnp.where(first_half,
                        pltpu.roll(xc, LANES - HEAD_DIM // 2, 1),
                        pltpu.roll(xc, HEAD_DIM // 2, 1))
    return xc * cos + swapped * sin_signed


def _qkv_kernel(x_ref, g_ref, w_ref, cos_ref, sin_ref, q_ref, k_ref, v_ref, *, chunk):
    n_q = q_ref.shape[0] * LANES
    kv_w = k_ref.shape[1]
    h = _rms(x_ref[...], g_ref[...]).astype(BF16)
    cos = cos_ref[...]
    sin = sin_ref[...]
    for c0 in range(0, n_q, chunk):
        y = jnp.dot(h, w_ref[:, c0:c0 + chunk], preferred_element_type=F32)
        for c in range(chunk // LANES):
            qc = _rope_col(y[:, c * LANES:(c + 1) * LANES], cos, sin) * (HEAD_DIM ** -0.5)
            q_ref[c0 // LANES + c] = qc.astype(BF16)
    y = jnp.dot(h, w_ref[:, n_q:], preferred_element_type=F32)
    for c in range(kv_w // LANES):
        k_ref[:, c * LANES:(c + 1) * LANES] = _rope_col(y[:, c * LANES:(c + 1) * LANES], cos, sin)
    v_ref[...] = y[:, kv_w:]


def _qkv_project(x, norm_g, w_qkv, cos, sin, layer, j, *, tm, n_heads, n_kv):
    m, d = x.shape
    kv_w = n_kv * HEAD_DIM
    n_cols = n_heads // HEADS_PER_COL
    return pl.pallas_call(
        functools.partial(_qkv_kernel, chunk=2 * kv_w),
        grid=(m // tm,),
        in_specs=[
            pl.BlockSpec((tm, d), lambda i: (i, 0)),
            pl.BlockSpec((None, None, 1, d), lambda i: (layer, 2, 0, 0)),
            pl.BlockSpec((None, d, w_qkv.shape[2]), lambda i: (j, 0, 0)),
            pl.BlockSpec((tm, LANES), lambda i: (i, 0)),
            pl.BlockSpec((tm, LANES), lambda i: (i, 0)),
        ],
        out_specs=[
            pl.BlockSpec((n_cols, tm, LANES), lambda i: (0, i, 0)),
            pl.BlockSpec((tm, kv_w), lambda i: (i, 0)),
            pl.BlockSpec((tm, kv_w), lambda i: (i, 0)),
        ],
        out_shape=[
            jax.ShapeDtypeStruct((n_cols, m, LANES), BF16),
            jax.ShapeDtypeStruct((m, kv_w), F32),
            jax.ShapeDtypeStruct((m, kv_w), F32),
        ],
        compiler_params=_params("parallel"),
        name="qkv_rope",
    )(x, norm_g, w_qkv, cos, sin)


def _block_diag(col, rolled, first):
    lane = lax.broadcasted_iota(jnp.int32, col.shape, 1)
    lo = lane < HEAD_DIM
    zero = jnp.zeros_like(col)
    if first:
        return jnp.concatenate([jnp.where(lo, col, zero), jnp.where(lo, zero, rolled)], axis=0)
    return jnp.concatenate([jnp.where(lo, rolled, zero), jnp.where(lo, zero, col)], axis=0)


def _attn_kernel(sink_ref, q_ref, kp_ref, kc_ref, vp_ref, vc_ref, x_ref, g_ref, wo_ref, o_ref,
                 att_ref, *, sq, n_kv, streaming, lag, q_pos0, k_pos0):
    i = pl.program_id(0)
    t = jnp.minimum(i, pl.num_programs(0) - 1 - lag)
    cur = lax.rem(i, 2) if lag else 0
    tq, d = x_ref.shape
    nsub = tq // sq
    nk_real = WINDOW + sq
    nk = -(-nk_real // LANES) * LANES
    assert nk == nk_real or not streaming
    n_cols = q_ref.shape[0]
    cols_per_kv = n_cols // n_kv
    n_blocks = nsub * n_kv
    blocks_per_chunk = n_blocks // (d // ATTN_OUT_COLS)

    if lag:
        @pl.when(i == 0)
        def _():
            att_ref[1] = jnp.zeros((tq, d), BF16)

    if streaming:
        k_all = jnp.concatenate([kp_ref[...], kc_ref[...]], axis=0)
        v_all = jnp.concatenate([vp_ref[...], vc_ref[...]], axis=0)

    row = lax.broadcasted_iota(jnp.int32, (sq, nk), 0)
    colk = lax.broadcasted_iota(jnp.int32, (sq, nk), 1)

    @functools.cache
    def sub_tile(s):
        if streaming:
            q0 = t * tq + s * sq
            k0 = q0 - WINDOW
            ks = k_all[s * sq:s * sq + nk]
            vs = v_all[s * sq:s * sq + nk]
        else:
            q0 = q_pos0
            k0 = k_pos0
            pad = [jnp.zeros((nk - nk_real, kp_ref.shape[2]), F32)] if nk > nk_real else []
            ks = jnp.concatenate([kp_ref[s], kc_ref[s * sq:(s + 1) * sq]] + pad, axis=0)
            vs = jnp.concatenate([vp_ref[s], vc_ref[s * sq:(s + 1) * sq]] + pad, axis=0)
        kpos = k0 + colk
        q_chunk = lax.shift_right_arithmetic(q0 + row, CHUNK_SHIFT)
        k_chunk = lax.shift_right_arithmetic(kpos, CHUNK_SHIFT)
        valid = (k_chunk >= q_chunk - LOOKBACK_CHUNKS) & (k_chunk <= q_chunk)
        if streaming:
            valid = valid & (kpos >= 0)
        if nk > nk_real:
            valid = valid & (colk < nk_real)
        return ks, vs, valid

    def scores(s, kv):
        ks, _, _ = sub_tile(s)
        c, first = divmod(kv, HEADS_PER_COL)
        kcol = ks[:, c * LANES:(c + 1) * LANES]
        bdk = _block_diag(kcol, pltpu.roll(kcol, HEAD_DIM, 1), first == 0).astype(BF16)
        qs = q_ref[kv * cols_per_kv:(kv + 1) * cols_per_kv, s * sq:(s + 1) * sq, :]
        qs = qs.reshape(cols_per_kv * sq, LANES)
        return lax.dot_general(qs, bdk, (((1,), (1,)), ((), ())), preferred_element_type=F32)

    def attend(s, kv, sc):
        _, vs, valid = sub_tile(s)
        c, first = divmod(kv, HEADS_PER_COL)
        vcol = vs[:, c * LANES:(c + 1) * LANES]
        bdv = _block_diag(vcol, pltpu.roll(vcol, HEAD_DIM, 1), first == 0).astype(BF16)
        p_rows = []
        for p in range(cols_per_kv):
            halves = []
            for e in range(HEADS_PER_COL):
                sink = sink_ref[(kv * cols_per_kv + p) * HEADS_PER_COL + e]
                blk = sc[p * sq:(p + 1) * sq, e * nk:(e + 1) * nk]
                blk = jnp.where(valid, blk, -jnp.inf)
                mx = jnp.maximum(jnp.max(blk, axis=-1, keepdims=True), sink)
                ex = jnp.exp(blk - mx)
                den = jnp.sum(ex, axis=-1, keepdims=True) + jnp.exp(sink - mx)
                halves.append((ex * (1.0 / den)).astype(BF16))
            p_rows.append(jnp.concatenate(halves, axis=1))
        pm = jnp.concatenate(p_rows, axis=0)
        ov = jnp.dot(pm, bdv, preferred_element_type=F32)
        for p in range(cols_per_kv):
            cc = kv * cols_per_kv + p
            att_ref[cur, s * sq:(s + 1) * sq, cc * LANES:(cc + 1) * LANES] = (
                ov[p * sq:(p + 1) * sq].astype(BF16))

    blocks = [(s, kv) for s in range(nsub) for kv in range(n_kv)]
    sc_next = scores(*blocks[0])
    for b, (s, kv) in enumerate(blocks):
        sc = sc_next
        if b + 1 < n_blocks:
            sc_next = scores(*blocks[b + 1])
        if lag and (b + 1) % blocks_per_chunk == 0:
            oc = ((b + 1) // blocks_per_chunk - 1) * ATTN_OUT_COLS
            o_ref[:, oc:oc + ATTN_OUT_COLS] = jnp.dot(
                att_ref[1 - cur], wo_ref[:, oc:oc + ATTN_OUT_COLS], preferred_element_type=F32)
        attend(s, kv, sc)

    if not lag:
        o_ref[...] = jnp.dot(att_ref[0], wo_ref[...], preferred_element_type=F32)
    o_ref[...] = x_ref[...] + _rms(o_ref[...], g_ref[...])


def _attention(x, q, k_prev, k, v_prev, v, sinks, norm_g, w_o, layer, j, *, tq, sq, n_kv,
               streaming):
    m, d = x.shape
    n_cols = q.shape[0]
    kv_w = k.shape[1]
    nsub = tq // sq
    n_tiles = m // tq
    lag = int(n_tiles >= ATTN_MIN_LAGGED_TILES)
    att_tile = lambda i: jnp.minimum(i, n_tiles - 1)
    out_tile = lambda i: jnp.maximum(i - lag, 0)
    if streaming:
        per = tq // WINDOW
        prev_spec = pl.BlockSpec((WINDOW, kv_w), lambda i: (jnp.maximum(att_tile(i) * per - 1, 0), 0))
    else:
        prev_spec = pl.BlockSpec((nsub, WINDOW, kv_w), lambda i: (att_tile(i), 0, 0))
    cur_spec = pl.BlockSpec((tq, kv_w), lambda i: (att_tile(i), 0))
    kern = functools.partial(_attn_kernel, sq=sq, n_kv=n_kv, streaming=streaming, lag=lag,
                             q_pos0=PAST_LEN, k_pos0=PAST_LEN - WINDOW)
    return pl.pallas_call(
        kern,
        grid=(n_tiles + lag,),
        in_specs=[
            pl.BlockSpec(memory_space=pltpu.SMEM),
            pl.BlockSpec((n_cols, tq, LANES), lambda i: (0, att_tile(i), 0)),
            prev_spec, cur_spec, prev_spec, cur_spec,
            pl.BlockSpec((tq, d), lambda i: (out_tile(i), 0)),
            pl.BlockSpec((None, None, 1, d), lambda i: (layer, 3, 0, 0)),
            pl.BlockSpec((None, d, d), lambda i: (j, 0, 0)),
        ],
        out_specs=pl.BlockSpec((tq, d), lambda i: (out_tile(i), 0)),
        out_shape=jax.ShapeDtypeStruct((m, d), F32),
        scratch_shapes=[pltpu.VMEM((1 + lag, tq, d), BF16)],
        compiler_params=_params("arbitrary"),
        name="swa_attention",
    )(sinks, q, k_prev, k, v_prev, v, x, norm_g, w_o)


def _rope_tables(tile_pos, row_pos):
    half = HEAD_DIM // 2
    inv_freq = ROPE_THETA ** (-jnp.arange(half, dtype=F32) / half)
    inv_freq_lanes = jnp.tile(inv_freq, LANES // half)
    sign_lanes = jnp.tile(jnp.concatenate([-jnp.ones(half, F32), jnp.ones(half, F32)]), HEADS_PER_COL)
    ang_t = tile_pos.astype(F32)[:, None, None] * inv_freq_lanes
    ang_r = row_pos.astype(F32)[None, :, None] * inv_freq_lanes
    cos = jnp.cos(ang_t) * jnp.cos(ang_r) - jnp.sin(ang_t) * jnp.sin(ang_r)
    sin = jnp.sin(ang_t) * jnp.cos(ang_r) + jnp.cos(ang_t) * jnp.sin(ang_r)
    return cos.reshape(-1, LANES), (sign_lanes * sin).reshape(-1, LANES)


def kernel(x_prompt, x_sample, state_conv, cache_k, cache_v, norm_g, w_ffn_in, w_ffn_out,
           w_conv_in, w_conv, w_conv_out, w_qkv, w_attn_out, attn_sinks):
    batch, seq, d = x_prompt.shape
    dec_batch, dec_seq, _ = x_sample.shape
    depth = norm_g.shape[0]
    n_heads = attn_sinks.shape[1]
    n_kv = cache_k.shape[3]
    kv_w = n_kv * HEAD_DIM
    assert batch == 1 and dec_seq == CHUNK and cache_k.shape[2] == WINDOW

    g4 = norm_g.reshape(depth, norm_g.shape[1], 1, d)
    w_conv_in_b = w_conv_in.astype(BF16)
    w_conv_out_b = w_conv_out.astype(BF16)
    w_qkv_b = w_qkv.astype(BF16)
    w_attn_out_b = w_attn_out.astype(BF16)

    xp = x_prompt.reshape(seq, d)
    xs = x_sample.reshape(dec_batch * dec_seq, d)
    cos_p, sin_p = _rope_tables(jnp.arange(0, seq, ROW_TILE), jnp.arange(ROW_TILE))
    cos_s, sin_s = _rope_tables(jnp.full((dec_batch,), PAST_LEN), jnp.arange(dec_seq))

    tm = ROW_TILE
    conv_p, conv_s, k_p, v_p, k_s, v_s = [], [], [], [], [], []
    for layer in range(depth):
        j = layer // 2
        def ffn_pair(xp, xs, slot, n_pre, n_post):
            ffn = functools.partial(_ffn, norm_g=g4, layer=layer, n_pre=n_pre, n_post=n_post, tm=FFN_TM)
            xs, bf16_w = ffn(xs, weights=(w_ffn_in, w_ffn_out, slot), tf=FFN_TF_F32, tile_cols=FFN_TF)
            return ffn(xp, weights=bf16_w, tf=FFN_TF), xs

        xp, xs = ffn_pair(xp, xs, 0, 0, 1)
        if layer % 2 == 0:
            mix = functools.partial(_conv_mixer, norm_g=g4, w_in=w_conv_in_b, conv_w=w_conv,
                                    w_out=w_conv_out_b, layer=layer, j=j, tm=tm)
            xp, st = mix(xp, jnp.zeros((1, 2, d), F32), seg=tm, carried=True, tn=d)
            conv_p.append(st)
            xs, st = mix(xs, state_conv[j], seg=dec_seq, carried=False, tn=CONV_TN)
            conv_s.append(st)
        else:
            proj = functools.partial(_qkv_project, norm_g=g4, w_qkv=w_qkv_b, layer=layer, j=j,
                                     tm=tm, n_heads=n_heads, n_kv=n_kv)
            att = functools.partial(_attention, sinks=attn_sinks[j], norm_g=g4, w_o=w_attn_out_b,
                                    layer=layer, j=j, n_kv=n_kv)
            q, k, v = proj(xp, cos=cos_p, sin=sin_p)
            xp = att(xp, q, k, k, v, v, tq=tm, sq=WINDOW, streaming=True)
            k_p.append(k[seq - WINDOW:].reshape(1, WINDOW, n_kv, HEAD_DIM))
            v_p.append(v[seq - WINDOW:].reshape(1, WINDOW, n_kv, HEAD_DIM))
            q, k, v = proj(xs, cos=cos_s, sin=sin_s)
            ck = cache_k[j].reshape(dec_batch, WINDOW, kv_w)
            cv = cache_v[j].reshape(dec_batch, WINDOW, kv_w)
            xs = att(xs, q, ck, k, cv, v, tq=tm, sq=dec_seq, streaming=False)
            k_new = jnp.concatenate([ck, k.reshape(dec_batch, dec_seq, kv_w)], axis=1)[:, -WINDOW:]
            v_new = jnp.concatenate([cv, v.reshape(dec_batch, dec_seq, kv_w)], axis=1)[:, -WINDOW:]
            k_s.append(k_new.reshape(dec_batch, WINDOW, n_kv, HEAD_DIM))
            v_s.append(v_new.reshape(dec_batch, WINDOW, n_kv, HEAD_DIM))
        xp, xs = ffn_pair(xp, xs, 1, 4, 5)

    return (xp.reshape(batch, seq, d), xs.reshape(dec_batch, dec_seq, d),
            jnp.stack(conv_p), jnp.stack(conv_s),
            jnp.stack(k_p), jnp.stack(v_p), jnp.stack(k_s), jnp.stack(v_s))
```

```python
import functools

import jax
import jax.numpy as jnp
from jax import lax
from jax.experimental import pallas as pl
from jax.experimental.pallas import tpu as pltpu

F32 = jnp.float32
BF16 = jnp.bfloat16

CHUNK = 64
CHUNK_SHIFT = CHUNK.bit_length() - 1
assert 1 << CHUNK_SHIFT == CHUNK
LOOKBACK_CHUNKS = 2
WINDOW = LOOKBACK_CHUNKS * CHUNK
HEAD_DIM = 64
PAST_LEN = 2048
ROPE_THETA = 10000.0
NORM_EPS = 1e-6

LANES = 128
HEADS_PER_COL = LANES // HEAD_DIM
VMEM_LIMIT = 62 * 1024 * 1024
ROW_TILE = 512
FFN_TM = 1024
FFN_ROWS = 512
FFN_POST_ROWS = 256
FFN_TF = 512
SIDE_BLOCK = (128, 1024)
FFN_TF_F32 = 256
CONV_TN = 512
CONV_COLS = 256
ATTN_OUT_COLS = 256
ATTN_MIN_LAGGED_TILES = 4


def _rms(x, g):
    ms = jnp.mean(x * x, axis=-1, keepdims=True)
    return x * lax.rsqrt(ms + NORM_EPS) * g


def _params(*sem):
    return pltpu.CompilerParams(dimension_semantics=sem, vmem_limit_bytes=VMEM_LIMIT)


def _ffn_kernel(x_ref, gpre_ref, gpost_ref, wg_ref, wu_ref, wo_ref, *rest, cast_own, side_blocks):
    n_side = len(side_blocks)
    side_in, o_ref, h_ref = rest[:n_side], rest[n_side], rest[-1]
    extra_out = rest[n_side + 1:-1]
    own_out, side_out = (extra_out, ()) if cast_own else ((), extra_out)
    j = pl.program_id(1)
    last = pl.num_programs(1) - 1
    step_id = pl.program_id(0) * pl.num_programs(1) + j
    for src, dst, n_blocks in zip(side_in, side_out, side_blocks):
        @pl.when(step_id < n_blocks)
        def _(src=src, dst=dst):
            dst[...] = src[...].astype(BF16)

    def step(first, final):
        wg, wu, wo = wg_ref[...], wu_ref[...], wo_ref[...]
        if cast_own:
            wg, wu, wo = wg.astype(BF16), wu.astype(BF16), wo.astype(BF16)
            for w, dst in zip((wg, wu, wo), own_out):
                dst[...] = w
        group = FFN_ROWS if first else x_ref.shape[0]
        for r in range(0, x_ref.shape[0], group):
            rows = slice(r, r + group)
            if first:
                h = _rms(x_ref[rows], gpre_ref[...]).astype(BF16)
                h_ref[rows] = h
            else:
                h = h_ref[rows]
            gate = jnp.dot(h, wg, preferred_element_type=F32)
            up = jnp.dot(h, wu, preferred_element_type=F32)
            act = (gate * jax.nn.sigmoid(gate) * up).astype(BF16)
            if final:
                for h0 in range(0, group, FFN_POST_ROWS):
                    sub = slice(r + h0, r + h0 + FFN_POST_ROWS)
                    part = jnp.dot(act[h0:h0 + FFN_POST_ROWS], wo, preferred_element_type=F32)
                    o_ref[sub] = x_ref[sub] + 0.5 * _rms(o_ref[sub] + part, gpost_ref[...])
                continue
            part = jnp.dot(act, wo, preferred_element_type=F32)
            if first:
                o_ref[rows] = part
            else:
                o_ref[rows] += part

    pl.when(j == 0)(functools.partial(step, True, False))
    pl.when((j > 0) & (j < last))(functools.partial(step, False, False))
    pl.when(j == last)(functools.partial(step, False, True))


def _side_cast_specs(arr, lead, block, n_steps, nf):
    (rows, cols), (br, bc) = arr.shape[-2:], block
    per_row = cols // bc
    n_blocks = (rows // br) * per_row
    assert rows % br == 0 and cols % bc == 0 and n_blocks <= n_steps
    blk = lambda i, j: jnp.minimum(i * nf + j, n_blocks - 1)
    src = pl.BlockSpec((None,) * len(lead) + block, lambda i, j: lead + (blk(i, j) // per_row, blk(i, j) % per_row))
    dst = pl.BlockSpec(block, lambda i, j: (blk(i, j) // per_row, blk(i, j) % per_row))
    return src, dst, n_blocks


def _ffn(x, norm_g, weights, layer, n_pre, n_post, *, tm, tf, tile_cols=None, side=()):
    m, d = x.shape
    cast_own = weights[0].dtype == F32
    assert not (cast_own and side)
    g_spec = lambda n: pl.BlockSpec((None, None, 1, d), lambda i, j: (layer, n, 0, 0))
    out_specs = [pl.BlockSpec((tm, d), lambda i, j: (i, 0))]
    out_shape = [jax.ShapeDtypeStruct((m, d), F32)]
    if cast_own:
        w_in, w_out, slot = weights
        f = w_out.shape[2]
        nf, per_tile = f // tf, tile_cols // tf
        assert m == tm and per_tile * tf == tile_cols
        w_specs = [
            pl.BlockSpec((None, None, d, tf), lambda i, j: (layer, slot, 0, j)),
            pl.BlockSpec((None, None, d, tf), lambda i, j: (layer, slot, 0, j + nf)),
            pl.BlockSpec((None, None, tf, d), lambda i, j: (layer, slot, j, 0)),
        ]
        operands = [w_in, w_in, w_out]
        tile_spec = pl.BlockSpec((None, d, tf), lambda i, j: (j // per_tile, 0, j % per_tile))
        out_specs += [tile_spec, tile_spec, pl.BlockSpec((tf, d), lambda i, j: (j, 0))]
        tiles = jax.ShapeDtypeStruct((f // tile_cols, d, tile_cols), BF16)
        out_shape += [tiles, tiles, jax.ShapeDtypeStruct((f, d), BF16)]
    else:
        gate, up, down = operands = list(weights)
        f = down.shape[0]
        assert gate.shape[2] == tf
        tile_spec = pl.BlockSpec((None, d, tf), lambda i, j: (j, 0, 0))
        w_specs = [tile_spec, tile_spec, pl.BlockSpec((tf, d), lambda i, j: (j, 0))]
    side_blocks = []
    for arr, lead, block in side:
        src, dst, n_blocks = _side_cast_specs(arr, lead, block, (m // tm) * (f // tf), f // tf)
        w_specs.append(src)
        operands.append(arr)
        out_specs.append(dst)
        out_shape.append(jax.ShapeDtypeStruct(arr.shape[-2:], BF16))
        side_blocks.append(n_blocks)
    outs = pl.pallas_call(
        functools.partial(_ffn_kernel, cast_own=cast_own, side_blocks=tuple(side_blocks)),
        grid=(m // tm, f // tf),
        in_specs=[pl.BlockSpec((tm, d), lambda i, j: (i, 0)), g_spec(n_pre), g_spec(n_post)] + w_specs,
        out_specs=out_specs,
        out_shape=out_shape,
        scratch_shapes=[pltpu.VMEM((tm, d), BF16)],
        compiler_params=_params("arbitrary" if side else "parallel", "arbitrary"),
        name="ffn",
    )(x, norm_g, norm_g, *operands)
    return (outs[0], tuple(outs[1:])) if cast_own or side else outs[0]


def _conv_kernel(x_ref, gpre_ref, gpost_ref, wb_ref, wc_ref, wh_ref, cw_ref, wo_ref, st_ref,
                 o_ref, ns_ref, h_ref, carry_ref, *, seg, carried):
    i = pl.program_id(0)
    n = pl.program_id(1)
    last = pl.num_programs(1) - 1
    tm = x_ref.shape[0]
    tn = wb_ref.shape[1]
    tc = CONV_COLS
    cols = pl.ds(pl.multiple_of(n * tn, tn), tn)

    if carried:
        @pl.when(i == 0)
        def _():
            carry_ref[:, cols] = st_ref[0, :, cols]

    def gated_conv(c0, b_gate, c_gate, hh):
        gcols = pl.ds(pl.multiple_of(n * tn + c0, tc), tc)
        cw = cw_ref[:, c0:c0 + tc]
        w0, w1, w2 = cw[0:1], cw[1:2], cw[2:3]
        row = lax.broadcasted_iota(jnp.int32, (seg, tc), 0)
        u = c_gate * hh
        convs = []
        for s in range(tm // seg):
            us = u[s * seg:(s + 1) * seg]
            prev = carry_ref[:, gcols] if carried else st_ref[s, :, c0:c0 + tc]
            p1 = jnp.where(row == 0, prev[1:2], pltpu.roll(us, 1, 0))
            p2 = jnp.where(row == 0, prev[0:1], jnp.where(row == 1, prev[1:2], pltpu.roll(us, 2, 0)))
            convs.append(p2 * w0 + p1 * w1 + us * w2)
            if carried:
                carry_ref[:, gcols] = us[seg - 2:seg]
                ns_ref[0, :, gcols] = us[seg - 2:seg]
            else:
                ns_ref[s, :, c0:c0 + tc] = us[seg - 2:seg]
        conv = convs[0] if len(convs) == 1 else jnp.concatenate(convs, axis=0)
        return (b_gate * conv).astype(BF16)

    def out_proj(assign, final, c0, z):
        part = jnp.dot(z, wo_ref[c0:c0 + tc, :], preferred_element_type=F32)
        if assign:
            o_ref[...] = part
        elif final:
            o_ref[...] = x_ref[...] + _rms(o_ref[...] + part, gpost_ref[...])
        else:
            o_ref[...] += part

    def step(first, final):
        if first:
            h = _rms(x_ref[...], gpre_ref[...]).astype(BF16)
            h_ref[...] = h
        else:
            h = h_ref[...]
        pending = None
        for c0 in range(0, tn, tc):
            b_gate = jnp.dot(h, wb_ref[:, c0:c0 + tc], preferred_element_type=F32)
            c_gate = jnp.dot(h, wc_ref[:, c0:c0 + tc], preferred_element_type=F32)
            hh = jnp.dot(h, wh_ref[:, c0:c0 + tc], preferred_element_type=F32)
            if pending is not None:
                out_proj(first and pending[0] == 0, False, *pending)
            pending = (c0, gated_conv(c0, b_gate, c_gate, hh))
        out_proj(first and pending[0] == 0, final, *pending)

    if tn == o_ref.shape[1]:
        step(True, True)
    else:
        pl.when(n == 0)(functools.partial(step, True, False))
        pl.when((n > 0) & (n < last))(functools.partial(step, False, False))
        pl.when(n == last)(functools.partial(step, False, True))


def _conv_mixer(x, state, norm_g, w_in, conv_w, w_out, layer, j, *, tm, tn, seg, carried):
    m, d = x.shape
    nn = d // tn
    nseg = tm // seg
    g_spec = lambda n: pl.BlockSpec((None, None, 1, d), lambda i, c: (layer, n, 0, 0))
    once = dict(pipeline_mode=pl.Buffered(1)) if nn == 1 else {}
    w_spec = lambda sec: pl.BlockSpec((d, tn), lambda i, c: (0, c + sec * nn), **once)
    if carried:
        st_spec = pl.BlockSpec((1, 2, d), lambda i, c: (0, 0, 0))
    else:
        st_spec = pl.BlockSpec((nseg, 2, tn), lambda i, c: (i, 0, c))
    return pl.pallas_call(
        functools.partial(_conv_kernel, seg=seg, carried=carried),
        grid=(m // tm, nn),
        in_specs=[
            pl.BlockSpec((tm, d), lambda i, c: (i, 0)),
            g_spec(2),
            g_spec(3),
            w_spec(0),
            w_spec(1),
            w_spec(2),
            pl.BlockSpec((None, 3, tn), lambda i, c: (j, 0, c)),
            pl.BlockSpec((tn, d), lambda i, c: (c, 0), **once),
            st_spec,
        ],
        out_specs=[pl.BlockSpec((tm, d), lambda i, c: (i, 0)), st_spec],
        out_shape=[
            jax.ShapeDtypeStruct((m, d), F32),
            jax.ShapeDtypeStruct(state.shape, F32),
        ],
        scratch_shapes=[pltpu.VMEM((tm, d), BF16), pltpu.VMEM((2, d), F32)],
        compiler_params=_params("arbitrary", "arbitrary"),
        name="conv_mixer",
    )(x, norm_g, norm_g, w_in, w_in, w_in, conv_w, w_out, state)


def _rope_col(xc, cos, sin_signed):
    lane = lax.broadcasted_iota(jnp.int32, xc.shape, 1)
    first_half = (lane % HEAD_DIM) < HEAD_DIM // 2
    swapped = jnp.where(first_half,
                        pltpu.roll(xc, LANES - HEAD_DIM // 2, 1),
                        pltpu.roll(xc, HEAD_DIM // 2, 1))
    return xc * cos + swapped * sin_signed


def _qkv_kernel(x_ref, g_ref, w_ref, cos_ref, sin_ref, q_ref, k_ref, v_ref, *, chunk):
    n_q = q_ref.shape[0] * LANES
    kv_w = k_ref.shape[1]
    h = _rms(x_ref[...], g_ref[...]).astype(BF16)
    cos = cos_ref[...]
    sin = sin_ref[...]
    for c0 in range(0, n_q, chunk):
        y = jnp.dot(h, w_ref[:, c0:c0 + chunk], preferred_element_type=F32)
        for c in range(chunk // LANES):
            qc = _rope_col(y[:, c * LANES:(c + 1) * LANES], cos, sin) * (HEAD_DIM ** -0.5)
            q_ref[c0 // LANES + c] = qc.astype(BF16)
    y = jnp.dot(h, w_ref[:, n_q:], preferred_element_type=F32)
    for c in range(kv_w // LANES):
        k_ref[:, c * LANES:(c + 1) * LANES] = _rope_col(y[:, c * LANES:(c + 1) * LANES], cos, sin)
    v_ref[...] = y[:, kv_w:]


def _qkv_project(x, norm_g, w_qkv, cos, sin, layer, *, tm, n_heads, n_kv):
    m, d = x.shape
    kv_w = n_kv * HEAD_DIM
    n_cols = n_heads // HEADS_PER_COL
    return pl.pallas_call(
        functools.partial(_qkv_kernel, chunk=2 * kv_w),
        grid=(m // tm,),
        in_specs=[
            pl.BlockSpec((tm, d), lambda i: (i, 0)),
            pl.BlockSpec((None, None, 1, d), lambda i: (layer, 2, 0, 0)),
            pl.BlockSpec((d, w_qkv.shape[1]), lambda i: (0, 0)),
            pl.BlockSpec((tm, LANES), lambda i: (i, 0)),
            pl.BlockSpec((tm, LANES), lambda i: (i, 0)),
        ],
        out_specs=[
            pl.BlockSpec((n_cols, tm, LANES), lambda i: (0, i, 0)),
            pl.BlockSpec((tm, kv_w), lambda i: (i, 0)),
            pl.BlockSpec((tm, kv_w), lambda i: (i, 0)),
        ],
        out_shape=[
            jax.ShapeDtypeStruct((n_cols, m, LANES), BF16),
            jax.ShapeDtypeStruct((m, kv_w), F32),
            jax.ShapeDtypeStruct((m, kv_w), F32),
        ],
        compiler_params=_params("parallel"),
        name="qkv_rope",
    )(x, norm_g, w_qkv, cos, sin)


def _block_diag(col, rolled, first):
    lane = lax.broadcasted_iota(jnp.int32, col.shape, 1)
    lo = lane < HEAD_DIM
    zero = jnp.zeros_like(col)
    if first:
        return jnp.concatenate([jnp.where(lo, col, zero), jnp.where(lo, zero, rolled)], axis=0)
    return jnp.concatenate([jnp.where(lo, rolled, zero), jnp.where(lo, zero, col)], axis=0)


def _attn_kernel(sink_ref, q_ref, kp_ref, kc_ref, vp_ref, vc_ref, x_ref, g_ref, wo_ref, o_ref,
                 att_ref, *, sq, n_kv, streaming, lag, q_pos0, k_pos0):
    i = pl.program_id(0)
    t = jnp.minimum(i, pl.num_programs(0) - 1 - lag)
    cur = lax.rem(i, 2) if lag else 0
    tq, d = x_ref.shape
    nsub = tq // sq
    nk_real = WINDOW + sq
    nk = -(-nk_real // LANES) * LANES
    assert nk == nk_real or not streaming
    n_cols = q_ref.shape[0]
    cols_per_kv = n_cols // n_kv
    n_blocks = nsub * n_kv
    blocks_per_chunk = n_blocks // (d // ATTN_OUT_COLS)

    if lag:
        @pl.when(i == 0)
        def _():
            att_ref[1] = jnp.zeros((tq, d), BF16)

    if streaming:
        k_all = jnp.concatenate([kp_ref[...], kc_ref[...]], axis=0)
        v_all = jnp.concatenate([vp_ref[...], vc_ref[...]], axis=0)

    row = lax.broadcasted_iota(jnp.int32, (sq, nk), 0)
    colk = lax.broadcasted_iota(jnp.int32, (sq, nk), 1)

    @functools.cache
    def sub_tile(s):
        if streaming:
            q0 = t * tq + s * sq
            k0 = q0 - WINDOW
            ks = k_all[s * sq:s * sq + nk]
            vs = v_all[s * sq:s * sq + nk]
        else:
            q0 = q_pos0
            k0 = k_pos0
            pad = [jnp.zeros((nk - nk_real, kp_ref.shape[2]), F32)] if nk > nk_real else []
            ks = jnp.concatenate([kp_ref[s], kc_ref[s * sq:(s + 1) * sq]] + pad, axis=0)
            vs = jnp.concatenate([vp_ref[s], vc_ref[s * sq:(s + 1) * sq]] + pad, axis=0)
        kpos = k0 + colk
        q_chunk = lax.shift_right_arithmetic(q0 + row, CHUNK_SHIFT)
        k_chunk = lax.shift_right_arithmetic(kpos, CHUNK_SHIFT)
        valid = (k_chunk >= q_chunk - LOOKBACK_CHUNKS) & (k_chunk <= q_chunk)
        if streaming:
            valid = valid & (kpos >= 0)
        if nk > nk_real:
            valid = valid & (colk < nk_real)
        return ks, vs, valid

    def scores(s, kv):
        ks, _, _ = sub_tile(s)
        c, first = divmod(kv, HEADS_PER_COL)
        kcol = ks[:, c * LANES:(c + 1) * LANES]
        bdk = _block_diag(kcol, pltpu.roll(kcol, HEAD_DIM, 1), first == 0).astype(BF16)
        qs = q_ref[kv * cols_per_kv:(kv + 1) * cols_per_kv, s * sq:(s + 1) * sq, :]
        qs = qs.reshape(cols_per_kv * sq, LANES)
        return lax.dot_general(qs, bdk, (((1,), (1,)), ((), ())), preferred_element_type=F32)

    def attend(s, kv, sc):
        _, vs, valid = sub_tile(s)
        c, first = divmod(kv, HEADS_PER_COL)
        vcol = vs[:, c * LANES:(c + 1) * LANES]
        bdv = _block_diag(vcol, pltpu.roll(vcol, HEAD_DIM, 1), first == 0).astype(BF16)
        p_rows = []
        for p in range(cols_per_kv):
            halves = []
            for e in range(HEADS_PER_COL):
                sink = sink_ref[(kv * cols_per_kv + p) * HEADS_PER_COL + e]
                blk = sc[p * sq:(p + 1) * sq, e * nk:(e + 1) * nk]
                blk = jnp.where(valid, blk, -jnp.inf)
                mx = jnp.maximum(jnp.max(blk, axis=-1, keepdims=True), sink)
                ex = jnp.exp(blk - mx)
                den = jnp.sum(ex, axis=-1, keepdims=True) + jnp.exp(sink - mx)
                halves.append((ex * (1.0 / den)).astype(BF16))
            p_rows.append(jnp.concatenate(halves, axis=1))
        pm = jnp.concatenate(p_rows, axis=0)
        ov = jnp.dot(pm, bdv, preferred_element_type=F32)
        for p in range(cols_per_kv):
            cc = kv * cols_per_kv + p
            att_ref[cur, s * sq:(s + 1) * sq, cc * LANES:(cc + 1) * LANES] = (
                ov[p * sq:(p + 1) * sq].astype(BF16))

    blocks = [(s, kv) for s in range(nsub) for kv in range(n_kv)]
    sc_next = scores(*blocks[0])
    for b, (s, kv) in enumerate(blocks):
        sc = sc_next
        if b + 1 < n_blocks:
            sc_next = scores(*blocks[b + 1])
        if lag and (b + 1) % blocks_per_chunk == 0:
            oc = ((b + 1) // blocks_per_chunk - 1) * ATTN_OUT_COLS
            o_ref[:, oc:oc + ATTN_OUT_COLS] = jnp.dot(
                att_ref[1 - cur], wo_ref[:, oc:oc + ATTN_OUT_COLS], preferred_element_type=F32)
        attend(s, kv, sc)

    if not lag:
        o_ref[...] = jnp.dot(att_ref[0], wo_ref[...], preferred_element_type=F32)
    o_ref[...] = x_ref[...] + _rms(o_ref[...], g_ref[...])


def _attention(x, q, k_prev, k, v_prev, v, sinks, norm_g, w_o, layer, *, tq, sq, n_kv, streaming):
    m, d = x.shape
    n_cols = q.shape[0]
    kv_w = k.shape[1]
    nsub = tq // sq
    n_tiles = m // tq
    lag = int(n_tiles >= ATTN_MIN_LAGGED_TILES)
    att_tile = lambda i: jnp.minimum(i, n_tiles - 1)
    out_tile = lambda i: jnp.maximum(i - lag, 0)
    if streaming:
        per = tq // WINDOW
        prev_spec = pl.BlockSpec((WINDOW, kv_w), lambda i: (jnp.maximum(att_tile(i) * per - 1, 0), 0))
    else:
        prev_spec = pl.BlockSpec((nsub, WINDOW, kv_w), lambda i: (att_tile(i), 0, 0))
    cur_spec = pl.BlockSpec((tq, kv_w), lambda i: (att_tile(i), 0))
    kern = functools.partial(_attn_kernel, sq=sq, n_kv=n_kv, streaming=streaming, lag=lag,
                             q_pos0=PAST_LEN, k_pos0=PAST_LEN - WINDOW)
    return pl.pallas_call(
        kern,
        grid=(n_tiles + lag,),
        in_specs=[
            pl.BlockSpec(memory_space=pltpu.SMEM),
            pl.BlockSpec((n_cols, tq, LANES), lambda i: (0, att_tile(i), 0)),
            prev_spec, cur_spec, prev_spec, cur_spec,
            pl.BlockSpec((tq, d), lambda i: (out_tile(i), 0)),
            pl.BlockSpec((None, None, 1, d), lambda i: (layer, 3, 0, 0)),
            pl.BlockSpec((d, d), lambda i: (0, 0)),
        ],
        out_specs=pl.BlockSpec((tq, d), lambda i: (out_tile(i), 0)),
        out_shape=jax.ShapeDtypeStruct((m, d), F32),
        scratch_shapes=[pltpu.VMEM((1 + lag, tq, d), BF16)],
        compiler_params=_params("arbitrary"),
        name="swa_attention",
    )(sinks, q, k_prev, k, v_prev, v, x, norm_g, w_o)


def _rope_tables(tile_pos, row_pos):
    half = HEAD_DIM // 2
    inv_freq = ROPE_THETA ** (-jnp.arange(half, dtype=F32) / half)
    inv_freq_lanes = jnp.tile(inv_freq, LANES // half)
    sign_lanes = jnp.tile(jnp.concatenate([-jnp.ones(half, F32), jnp.ones(half, F32)]), HEADS_PER_COL)
    ang_t = tile_pos.astype(F32)[:, None, None] * inv_freq_lanes
    ang_r = row_pos.astype(F32)[None, :, None] * inv_freq_lanes
    cos = jnp.cos(ang_t) * jnp.cos(ang_r) - jnp.sin(ang_t) * jnp.sin(ang_r)
    sin = jnp.sin(ang_t) * jnp.cos(ang_r) + jnp.cos(ang_t) * jnp.sin(ang_r)
    return cos.reshape(-1, LANES), (sign_lanes * sin).reshape(-1, LANES)


def kernel(x_prompt, x_sample, state_conv, cache_k, cache_v, norm_g, w_ffn_in, w_ffn_out,
           w_conv_in, w_conv, w_conv_out, w_qkv, w_attn_out, attn_sinks):
    batch, seq, d = x_prompt.shape
    dec_batch, dec_seq, _ = x_sample.shape
    depth = norm_g.shape[0]
    n_heads = attn_sinks.shape[1]
    n_kv = cache_k.shape[3]
    kv_w = n_kv * HEAD_DIM
    assert batch == 1 and dec_seq == CHUNK and cache_k.shape[2] == WINDOW

    g4 = norm_g.reshape(depth, norm_g.shape[1], 1, d)

    xp = x_prompt.reshape(seq, d)
    xs = x_sample.reshape(dec_batch * dec_seq, d)
    cos_p, sin_p = _rope_tables(jnp.arange(0, seq, ROW_TILE), jnp.arange(ROW_TILE))
    cos_s, sin_s = _rope_tables(jnp.full((dec_batch,), PAST_LEN), jnp.arange(dec_seq))

    tm = ROW_TILE
    conv_p, conv_s, k_p, v_p, k_s, v_s = [], [], [], [], [], []
    for layer in range(depth):
        j = layer // 2
        def ffn_pair(xp, xs, slot, n_pre, n_post, side=()):
            ffn = functools.partial(_ffn, norm_g=g4, layer=layer, n_pre=n_pre, n_post=n_post, tm=FFN_TM)
            xs, bf16_w = ffn(xs, weights=(w_ffn_in, w_ffn_out, slot), tf=FFN_TF_F32, tile_cols=FFN_TF)
            return ffn(xp, weights=bf16_w, tf=FFN_TF, side=side), xs

        if layer % 2 == 0:
            side = ((w_conv_in, (j,), SIDE_BLOCK), (w_conv_out, (j,), SIDE_BLOCK))
        else:
            side = ((w_qkv, (j,), (SIDE_BLOCK[0], SIDE_BLOCK[1] // 2)), (w_attn_out, (j,), SIDE_BLOCK))
        (xp, (w_mix_in_b, w_mix_out_b)), xs = ffn_pair(xp, xs, 0, 0, 1, side)
        if layer % 2 == 0:
            mix = functools.partial(_conv_mixer, norm_g=g4, w_in=w_mix_in_b, conv_w=w_conv,
                                    w_out=w_mix_out_b, layer=layer, j=j, tm=tm)
            xp, st = mix(xp, jnp.zeros((1, 2, d), F32), seg=tm, carried=True, tn=d)
            conv_p.append(st)
            xs, st = mix(xs, state_conv[j], seg=dec_seq, carried=False, tn=CONV_TN)
            conv_s.append(st)
        else:
            proj = functools.partial(_qkv_project, norm_g=g4, w_qkv=w_mix_in_b, layer=layer,
                                     tm=tm, n_heads=n_heads, n_kv=n_kv)
            att = functools.partial(_attention, sinks=attn_sinks[j], norm_g=g4, w_o=w_mix_out_b,
                                    layer=layer, n_kv=n_kv)
            q, k, v = proj(xp, cos=cos_p, sin=sin_p)
            xp = att(xp, q, k, k, v, v, tq=tm, sq=WINDOW, streaming=True)
            k_p.append(k[seq - WINDOW:].reshape(1, WINDOW, n_kv, HEAD_DIM))
            v_p.append(v[seq - WINDOW:].reshape(1, WINDOW, n_kv, HEAD_DIM))
            q, k, v = proj(xs, cos=cos_s, sin=sin_s)
            ck = cache_k[j].reshape(dec_batch, WINDOW, kv_w)
            cv = cache_v[j].reshape(dec_batch, WINDOW, kv_w)
            xs = att(xs, q, ck, k, cv, v, tq=tm, sq=dec_seq, streaming=False)
            k_new = jnp.concatenate([ck, k.reshape(dec_batch, dec_seq, kv_w)], axis=1)[:, -WINDOW:]
            v_new = jnp.concatenate([cv, v.reshape(dec_batch, dec_seq, kv_w)], axis=1)[:, -WINDOW:]
            k_s.append(k_new.reshape(dec_batch, WINDOW, n_kv, HEAD_DIM))
            v_s.append(v_new.reshape(dec_batch, WINDOW, n_kv, HEAD_DIM))
        xp, xs = ffn_pair(xp, xs, 1, 4, 5)

    return (xp.reshape(batch, seq, d), xs.reshape(dec_batch, dec_seq, d),
            jnp.stack(conv_p), jnp.stack(conv_s),
            jnp.stack(k_p), jnp.stack(v_p), jnp.stack(k_s), jnp.stack(v_s))
```

```python
import functools

import jax
import jax.numpy as jnp
from jax import lax
from jax.experimental import pallas as pl
from jax.experimental.pallas import tpu as pltpu

F32 = jnp.float32
BF16 = jnp.bfloat16

CHUNK = 64
CHUNK_SHIFT = CHUNK.bit_length() - 1
assert 1 << CHUNK_SHIFT == CHUNK
LOOKBACK_CHUNKS = 2
WINDOW = LOOKBACK_CHUNKS * CHUNK
HEAD_DIM = 64
PAST_LEN = 2048
ROPE_THETA = 10000.0
NORM_EPS = 1e-6

LANES = 128
HEADS_PER_COL = LANES // HEAD_DIM
VMEM_LIMIT = 62 * 1024 * 1024
ROW_TILE = 512
FFN_TM = 1024
FFN_ROWS = 512
FFN_POST_ROWS = 256
FFN_TF = 512
SIDE_BLOCK = (128, 1024)
FFN_TF_F32 = 256
CONV_TN = 512
CONV_COLS = 256
ATTN_OUT_COLS = 256
ATTN_MIN_LAGGED_TILES = 4


def _rms(x, g):
    ms = jnp.mean(x * x, axis=-1, keepdims=True)
    return x * lax.rsqrt(ms + NORM_EPS) * g


def _params(*sem):
    return pltpu.CompilerParams(dimension_semantics=sem, vmem_limit_bytes=VMEM_LIMIT)


def _ffn_kernel(x_ref, gpre_ref, gpost_ref, wg_ref, wu_ref, wo_ref, *rest, cast_own, side_blocks):
    n_side = len(side_blocks)
    side_in, o_ref, h_ref = rest[:n_side], rest[n_side], rest[-1]
    extra_out = rest[n_side + 1:-1]
    own_out, side_out = (extra_out, ()) if cast_own else ((), extra_out)
    j = pl.program_id(1)
    last = pl.num_programs(1) - 1
    step_id = pl.program_id(0) * pl.num_programs(1) + j
    for src, dst, n_blocks in zip(side_in, side_out, side_blocks):
        @pl.when(step_id < n_blocks)
        def _(src=src, dst=dst):
            dst[...] = src[...].astype(BF16)

    def step(first, final):
        wg, wu, wo = wg_ref[...], wu_ref[...], wo_ref[...]
        if cast_own:
            wg, wu, wo = wg.astype(BF16), wu.astype(BF16), wo.astype(BF16)
            for w, dst in zip((wg, wu, wo), own_out):
                dst[...] = w
        group = x_ref.shape[0]
        for r in range(0, x_ref.shape[0], group):
            rows = slice(r, r + group)
            if first:
                h = _rms(x_ref[rows], gpre_ref[...]).astype(BF16)
                h_ref[rows] = h
            else:
                h = h_ref[rows]
            gate = jnp.dot(h, wg, preferred_element_type=F32)
            up = jnp.dot(h, wu, preferred_element_type=F32)
            act = (gate * jax.nn.sigmoid(gate) * up).astype(BF16)
            if final:
                for h0 in range(0, group, FFN_POST_ROWS):
                    sub = slice(r + h0, r + h0 + FFN_POST_ROWS)
                    part = jnp.dot(act[h0:h0 + FFN_POST_ROWS], wo, preferred_element_type=F32)
                    o_ref[sub] = x_ref[sub] + 0.5 * _rms(o_ref[sub] + part, gpost_ref[...])
                continue
            part = jnp.dot(act, wo, preferred_element_type=F32)
            if first:
                o_ref[rows] = part
            else:
                o_ref[rows] += part

    pl.when(j == 0)(functools.partial(step, True, False))
    pl.when((j > 0) & (j < last))(functools.partial(step, False, False))
    pl.when(j == last)(functools.partial(step, False, True))


def _side_cast_specs(arr, lead, block, n_steps, nf):
    (rows, cols), (br, bc) = arr.shape[-2:], block
    per_row = cols // bc
    n_blocks = (rows // br) * per_row
    assert rows % br == 0 and cols % bc == 0 and n_blocks <= n_steps
    blk = lambda i, j: jnp.minimum(i * nf + j, n_blocks - 1)
    src = pl.BlockSpec((None,) * len(lead) + block, lambda i, j: lead + (blk(i, j) // per_row, blk(i, j) % per_row))
    dst = pl.BlockSpec(block, lambda i, j: (blk(i, j) // per_row, blk(i, j) % per_row))
    return src, dst, n_blocks


def _ffn(x, norm_g, weights, layer, n_pre, n_post, *, tm, tf, tile_cols=None, side=()):
    m, d = x.shape
    cast_own = weights[0].dtype == F32
    assert not (cast_own and side)
    g_spec = lambda n: pl.BlockSpec((None, None, 1, d), lambda i, j: (layer, n, 0, 0))
    out_specs = [pl.BlockSpec((tm, d), lambda i, j: (i, 0))]
    out_shape = [jax.ShapeDtypeStruct((m, d), F32)]
    if cast_own:
        w_in, w_out, slot = weights
        f = w_out.shape[2]
        nf, per_tile = f // tf, tile_cols // tf
        assert m == tm and per_tile * tf == tile_cols
        w_specs = [
            pl.BlockSpec((None, None, d, tf), lambda i, j: (layer, slot, 0, j)),
            pl.BlockSpec((None, None, d, tf), lambda i, j: (layer, slot, 0, j + nf)),
            pl.BlockSpec((None, None, tf, d), lambda i, j: (layer, slot, j, 0)),
        ]
        operands = [w_in, w_in, w_out]
        tile_spec = pl.BlockSpec((None, d, tf), lambda i, j: (j // per_tile, 0, j % per_tile))
        out_specs += [tile_spec, tile_spec, pl.BlockSpec((tf, d), lambda i, j: (j, 0))]
        tiles = jax.ShapeDtypeStruct((f // tile_cols, d, tile_cols), BF16)
        out_shape += [tiles, tiles, jax.ShapeDtypeStruct((f, d), BF16)]
    else:
        gate, up, down = operands = list(weights)
        f = down.shape[0]
        assert gate.shape[2] == tf
        tile_spec = pl.BlockSpec((None, d, tf), lambda i, j: (j, 0, 0))
        w_specs = [tile_spec, tile_spec, pl.BlockSpec((tf, d), lambda i, j: (j, 0))]
    side_blocks = []
    for arr, lead, block in side:
        src, dst, n_blocks = _side_cast_specs(arr, lead, block, (m // tm) * (f // tf), f // tf)
        w_specs.append(src)
        operands.append(arr)
        out_specs.append(dst)
        out_shape.append(jax.ShapeDtypeStruct(arr.shape[-2:], BF16))
        side_blocks.append(n_blocks)
    outs = pl.pallas_call(
        functools.partial(_ffn_kernel, cast_own=cast_own, side_blocks=tuple(side_blocks)),
        grid=(m // tm, f // tf),
        in_specs=[pl.BlockSpec((tm, d), lambda i, j: (i, 0)), g_spec(n_pre), g_spec(n_post)] + w_specs,
        out_specs=out_specs,
        out_shape=out_shape,
        scratch_shapes=[pltpu.VMEM((tm, d), BF16)],
        compiler_params=_params("arbitrary" if side else "parallel", "arbitrary"),
        name="ffn",
    )(x, norm_g, norm_g, *operands)
    return (outs[0], tuple(outs[1:])) if cast_own or side else outs[0]


def _conv_kernel(x_ref, gpre_ref, gpost_ref, wb_ref, wc_ref, wh_ref, cw_ref, wo_ref, st_ref,
                 o_ref, ns_ref, h_ref, carry_ref, *, seg, carried):
    i = pl.program_id(0)
    n = pl.program_id(1)
    last = pl.num_programs(1) - 1
    tm = x_ref.shape[0]
    tn = wb_ref.shape[1]
    tc = CONV_COLS
    cols = pl.ds(pl.multiple_of(n * tn, tn), tn)

    if carried:
        @pl.when(i == 0)
        def _():
            carry_ref[:, cols] = st_ref[0, :, cols]

    def gated_conv(c0, b_gate, c_gate, hh):
        gcols = pl.ds(pl.multiple_of(n * tn + c0, tc), tc)
        cw = cw_ref[:, c0:c0 + tc]
        w0, w1, w2 = cw[0:1], cw[1:2], cw[2:3]
        row = lax.broadcasted_iota(jnp.int32, (seg, tc), 0)
        u = c_gate * hh
        convs = []
        for s in range(tm // seg):
            us = u[s * seg:(s + 1) * seg]
            prev = carry_ref[:, gcols] if carried else st_ref[s, :, c0:c0 + tc]
            p1 = jnp.where(row == 0, prev[1:2], pltpu.roll(us, 1, 0))
            p2 = jnp.where(row == 0, prev[0:1], jnp.where(row == 1, prev[1:2], pltpu.roll(us, 2, 0)))
            convs.append(p2 * w0 + p1 * w1 + us * w2)
            if carried:
                carry_ref[:, gcols] = us[seg - 2:seg]
                ns_ref[0, :, gcols] = us[seg - 2:seg]
            else:
                ns_ref[s, :, c0:c0 + tc] = us[seg - 2:seg]
        conv = convs[0] if len(convs) == 1 else jnp.concatenate(convs, axis=0)
        return (b_gate * conv).astype(BF16)

    def out_proj(assign, final, c0, z):
        part = jnp.dot(z, wo_ref[c0:c0 + tc, :], preferred_element_type=F32)
        if assign:
            o_ref[...] = part
        elif final:
            o_ref[...] = x_ref[...] + _rms(o_ref[...] + part, gpost_ref[...])
        else:
            o_ref[...] += part

    def step(first, final):
        if first:
            h = _rms(x_ref[...], gpre_ref[...]).astype(BF16)
            h_ref[...] = h
        else:
            h = h_ref[...]
        pending = None
        for c0 in range(0, tn, tc):
            b_gate = jnp.dot(h, wb_ref[:, c0:c0 + tc], preferred_element_type=F32)
            c_gate = jnp.dot(h, wc_ref[:, c0:c0 + tc], preferred_element_type=F32)
            hh = jnp.dot(h, wh_ref[:, c0:c0 + tc], preferred_element_type=F32)
            if pending is not None:
                out_proj(first and pending[0] == 0, False, *pending)
            pending = (c0, gated_conv(c0, b_gate, c_gate, hh))
        out_proj(first and pending[0] == 0, final, *pending)

    if tn == o_ref.shape[1]:
        step(True, True)
    else:
        pl.when(n == 0)(functools.partial(step, True, False))
        pl.when((n > 0) & (n < last))(functools.partial(step, False, False))
        pl.when(n == last)(functools.partial(step, False, True))


def _conv_mixer(x, state, norm_g, w_in, conv_w, w_out, layer, j, *, tm, tn, seg, carried):
    m, d = x.shape
    nn = d // tn
    nseg = tm // seg
    g_spec = lambda n: pl.BlockSpec((None, None, 1, d), lambda i, c: (layer, n, 0, 0))
    once = dict(pipeline_mode=pl.Buffered(1)) if nn == 1 else {}
    w_spec = lambda sec: pl.BlockSpec((d, tn), lambda i, c: (0, c + sec * nn), **once)
    if carried:
        st_spec = pl.BlockSpec((1, 2, d), lambda i, c: (0, 0, 0))
    else:
        st_spec = pl.BlockSpec((nseg, 2, tn), lambda i, c: (i, 0, c))
    return pl.pallas_call(
        functools.partial(_conv_kernel, seg=seg, carried=carried),
        grid=(m // tm, nn),
        in_specs=[
            pl.BlockSpec((tm, d), lambda i, c: (i, 0)),
            g_spec(2),
            g_spec(3),
            w_spec(0),
            w_spec(1),
            w_spec(2),
            pl.BlockSpec((None, 3, tn), lambda i, c: (j, 0, c)),
            pl.BlockSpec((tn, d), lambda i, c: (c, 0), **once),
            st_spec,
        ],
        out_specs=[pl.BlockSpec((tm, d), lambda i, c: (i, 0)), st_spec],
        out_shape=[
            jax.ShapeDtypeStruct((m, d), F32),
            jax.ShapeDtypeStruct(state.shape, F32),
        ],
        scratch_shapes=[pltpu.VMEM((tm, d), BF16), pltpu.VMEM((2, d), F32)],
        compiler_params=_params("arbitrary", "arbitrary"),
        name="conv_mixer",
    )(x, norm_g, norm_g, w_in, w_in, w_in, conv_w, w_out, state)


def _rope_col(xc, cos, sin_signed):
    lane = lax.broadcasted_iota(jnp.int32, xc.shape, 1)
    first_half = (lane % HEAD_DIM) < HEAD_DIM // 2
    swapped = jnp.where(first_half,
                        pltpu.roll(xc, LANES - HEAD_DIM // 2, 1),
                        pltpu.roll(xc, HEAD_DIM // 2, 1))
    return xc * cos + swapped * sin_signed


def _qkv_kernel(x_ref, g_ref, w_ref, cos_ref, sin_ref, q_ref, k_ref, v_ref, *, chunk):
    n_q = q_ref.shape[0] * LANES
    kv_w = k_ref.shape[1]
    h = _rms(x_ref[...], g_ref[...]).astype(BF16)
    cos = cos_ref[...]
    sin = sin_ref[...]
    for c0 in range(0, n_q, chunk):
        y = jnp.dot(h, w_ref[:, c0:c0 + chunk], preferred_element_type=F32)
        for c in range(chunk // LANES):
            qc = _rope_col(y[:, c * LANES:(c + 1) * LANES], cos, sin) * (HEAD_DIM ** -0.5)
            q_ref[c0 // LANES + c] = qc.astype(BF16)
    y = jnp.dot(h, w_ref[:, n_q:], preferred_element_type=F32)
    for c in range(kv_w // LANES):
        k_ref[:, c * LANES:(c + 1) * LANES] = _rope_col(y[:, c * LANES:(c + 1) * LANES], cos, sin)
    v_ref[...] = y[:, kv_w:]


def _qkv_project(x, norm_g, w_qkv, cos, sin, layer, *, tm, n_heads, n_kv):
    m, d = x.shape
    kv_w = n_kv * HEAD_DIM
    n_cols = n_heads // HEADS_PER_COL
    return pl.pallas_call(
        functools.partial(_qkv_kernel, chunk=2 * kv_w),
        grid=(m // tm,),
        in_specs=[
            pl.BlockSpec((tm, d), lambda i: (i, 0)),
            pl.BlockSpec((None, None, 1, d), lambda i: (layer, 2, 0, 0)),
            pl.BlockSpec((d, w_qkv.shape[1]), lambda i: (0, 0)),
            pl.BlockSpec((tm, LANES), lambda i: (i, 0)),
            pl.BlockSpec((tm, LANES), lambda i: (i, 0)),
        ],
        out_specs=[
            pl.BlockSpec((n_cols, tm, LANES), lambda i: (0, i, 0)),
            pl.BlockSpec((tm, kv_w), lambda i: (i, 0)),
            pl.BlockSpec((tm, kv_w), lambda i: (i, 0)),
        ],
        out_shape=[
            jax.ShapeDtypeStruct((n_cols, m, LANES), BF16),
            jax.ShapeDtypeStruct((m, kv_w), F32),
            jax.ShapeDtypeStruct((m, kv_w), F32),
        ],
        compiler_params=_params("parallel"),
        name="qkv_rope",
    )(x, norm_g, w_qkv, cos, sin)


def _block_diag(col, rolled, first):
    lane = lax.broadcasted_iota(jnp.int32, col.shape, 1)
    lo = lane < HEAD_DIM
    zero = jnp.zeros_like(col)
    if first:
        return jnp.concatenate([jnp.where(lo, col, zero), jnp.where(lo, zero, rolled)], axis=0)
    return jnp.concatenate([jnp.where(lo, rolled, zero), jnp.where(lo, zero, col)], axis=0)


def _attn_kernel(sink_ref, q_ref, kp_ref, kc_ref, vp_ref, vc_ref, x_ref, g_ref, wo_ref, o_ref,
                 att_ref, *, sq, n_kv, streaming, lag, q_pos0, k_pos0):
    i = pl.program_id(0)
    t = jnp.minimum(i, pl.num_programs(0) - 1 - lag)
    cur = lax.rem(i, 2) if lag else 0
    tq, d = x_ref.shape
    nsub = tq // sq
    nk_real = WINDOW + sq
    nk = -(-nk_real // LANES) * LANES
    assert nk == nk_real or not streaming
    n_cols = q_ref.shape[0]
    cols_per_kv = n_cols // n_kv
    n_blocks = nsub * n_kv
    blocks_per_chunk = n_blocks // (d // ATTN_OUT_COLS)

    if lag:
        @pl.when(i == 0)
        def _():
            att_ref[1] = jnp.zeros((tq, d), BF16)

    if streaming:
        k_all = jnp.concatenate([kp_ref[...], kc_ref[...]], axis=0)
        v_all = jnp.concatenate([vp_ref[...], vc_ref[...]], axis=0)

    row = lax.broadcasted_iota(jnp.int32, (sq, nk), 0)
    colk = lax.broadcasted_iota(jnp.int32, (sq, nk), 1)

    @functools.cache
    def sub_tile(s):
        if streaming:
            q0 = t * tq + s * sq
            k0 = q0 - WINDOW
            ks = k_all[s * sq:s * sq + nk]
            vs = v_all[s * sq:s * sq + nk]
        else:
            q0 = q_pos0
            k0 = k_pos0
            pad = [jnp.zeros((nk - nk_real, kp_ref.shape[2]), F32)] if nk > nk_real else []
            ks = jnp.concatenate([kp_ref[s], kc_ref[s * sq:(s + 1) * sq]] + pad, axis=0)
            vs = jnp.concatenate([vp_ref[s], vc_ref[s * sq:(s + 1) * sq]] + pad, axis=0)
        kpos = k0 + colk
        q_chunk = lax.shift_right_arithmetic(q0 + row, CHUNK_SHIFT)
        k_chunk = lax.shift_right_arithmetic(kpos, CHUNK_SHIFT)
        valid = (k_chunk >= q_chunk - LOOKBACK_CHUNKS) & (k_chunk <= q_chunk)
        if streaming:
            valid = valid & (kpos >= 0)
        if nk > nk_real:
            valid = valid & (colk < nk_real)
        return ks, vs, valid

    def scores(s, kv):
        ks, _, _ = sub_tile(s)
        c, first = divmod(kv, HEADS_PER_COL)
        kcol = ks[:, c * LANES:(c + 1) * LANES]
        bdk = _block_diag(kcol, pltpu.roll(kcol, HEAD_DIM, 1), first == 0).astype(BF16)
        qs = q_ref[kv * cols_per_kv:(kv + 1) * cols_per_kv, s * sq:(s + 1) * sq, :]
        qs = qs.reshape(cols_per_kv * sq, LANES)
        return lax.dot_general(qs, bdk, (((1,), (1,)), ((), ())), preferred_element_type=F32)

    def attend(s, kv, sc):
        _, vs, valid = sub_tile(s)
        c, first = divmod(kv, HEADS_PER_COL)
        vcol = vs[:, c * LANES:(c + 1) * LANES]
        bdv = _block_diag(vcol, pltpu.roll(vcol, HEAD_DIM, 1), first == 0).astype(BF16)
        p_rows = []
        for p in range(cols_per_kv):
            halves = []
            for e in range(HEADS_PER_COL):
                sink = sink_ref[(kv * cols_per_kv + p) * HEADS_PER_COL + e]
                blk = sc[p * sq:(p + 1) * sq, e * nk:(e + 1) * nk]
                blk = jnp.where(valid, blk, -jnp.inf)
                mx = jnp.maximum(jnp.max(blk, axis=-1, keepdims=True), sink)
                ex = jnp.exp(blk - mx)
                den = jnp.sum(ex, axis=-1, keepdims=True) + jnp.exp(sink - mx)
                halves.append((ex * (1.0 / den)).astype(BF16))
            p_rows.append(jnp.concatenate(halves, axis=1))
        pm = jnp.concatenate(p_rows, axis=0)
        ov = jnp.dot(pm, bdv, preferred_element_type=F32)
        for p in range(cols_per_kv):
            cc = kv * cols_per_kv + p
            att_ref[cur, s * sq:(s + 1) * sq, cc * LANES:(cc + 1) * LANES] = (
                ov[p * sq:(p + 1) * sq].astype(BF16))

    blocks = [(s, kv) for s in range(nsub) for kv in range(n_kv)]
    sc_next = scores(*blocks[0])
    for b, (s, kv) in enumerate(blocks):
        sc = sc_next
        if b + 1 < n_blocks:
            sc_next = scores(*blocks[b + 1])
        if lag and (b + 1) % blocks_per_chunk == 0:
            oc = ((b + 1) // blocks_per_chunk - 1) * ATTN_OUT_COLS
            o_ref[:, oc:oc + ATTN_OUT_COLS] = jnp.dot(
                att_ref[1 - cur], wo_ref[:, oc:oc + ATTN_OUT_COLS], preferred_element_type=F32)
        attend(s, kv, sc)

    if not lag:
        o_ref[...] = jnp.dot(att_ref[0], wo_ref[...], preferred_element_type=F32)
    o_ref[...] = x_ref[...] + _rms(o_ref[...], g_ref[...])


def _attention(x, q, k_prev, k, v_prev, v, sinks, norm_g, w_o, layer, *, tq, sq, n_kv, streaming):
    m, d = x.shape
    n_cols = q.shape[0]
    kv_w = k.shape[1]
    nsub = tq // sq
    n_tiles = m // tq
    lag = int(n_tiles >= ATTN_MIN_LAGGED_TILES)
    att_tile = lambda i: jnp.minimum(i, n_tiles - 1)
    out_tile = lambda i: jnp.maximum(i - lag, 0)
    if streaming:
        per = tq // WINDOW
        prev_spec = pl.BlockSpec((WINDOW, kv_w), lambda i: (jnp.maximum(att_tile(i) * per - 1, 0), 0))
    else:
        prev_spec = pl.BlockSpec((nsub, WINDOW, kv_w), lambda i: (att_tile(i), 0, 0))
    cur_spec = pl.BlockSpec((tq, kv_w), lambda i: (att_tile(i), 0))
    kern = functools.partial(_attn_kernel, sq=sq, n_kv=n_kv, streaming=streaming, lag=lag,
                             q_pos0=PAST_LEN, k_pos0=PAST_LEN - WINDOW)
    return pl.pallas_call(
        kern,
        grid=(n_tiles + lag,),
        in_specs=[
            pl.BlockSpec(memory_space=pltpu.SMEM),
            pl.BlockSpec((n_cols, tq, LANES), lambda i: (0, att_tile(i), 0)),
            prev_spec, cur_spec, prev_spec, cur_spec,
            pl.BlockSpec((tq, d), lambda i: (out_tile(i), 0)),
            pl.BlockSpec((None, None, 1, d), lambda i: (layer, 3, 0, 0)),
            pl.BlockSpec((d, d), lambda i: (0, 0)),
        ],
        out_specs=pl.BlockSpec((tq, d), lambda i: (out_tile(i), 0)),
        out_shape=jax.ShapeDtypeStruct((m, d), F32),
        scratch_shapes=[pltpu.VMEM((1 + lag, tq, d), BF16)],
        compiler_params=_params("arbitrary"),
        name="swa_attention",
    )(sinks, q, k_prev, k, v_prev, v, x, norm_g, w_o)


def _rope_tables(tile_pos, row_pos):
    half = HEAD_DIM // 2
    inv_freq = ROPE_THETA ** (-jnp.arange(half, dtype=F32) / half)
    inv_freq_lanes = jnp.tile(inv_freq, LANES // half)
    sign_lanes = jnp.tile(jnp.concatenate([-jnp.ones(half, F32), jnp.ones(half, F32)]), HEADS_PER_COL)
    ang_t = tile_pos.astype(F32)[:, None, None] * inv_freq_lanes
    ang_r = row_pos.astype(F32)[None, :, None] * inv_freq_lanes
    cos = jnp.cos(ang_t) * jnp.cos(ang_r) - jnp.sin(ang_t) * jnp.sin(ang_r)
    sin = jnp.sin(ang_t) * jnp.cos(ang_r) + jnp.cos(ang_t) * jnp.sin(ang_r)
    return cos.reshape(-1, LANES), (sign_lanes * sin).reshape(-1, LANES)


def kernel(x_prompt, x_sample, state_conv, cache_k, cache_v, norm_g, w_ffn_in, w_ffn_out,
           w_conv_in, w_conv, w_conv_out, w_qkv, w_attn_out, attn_sinks):
    batch, seq, d = x_prompt.shape
    dec_batch, dec_seq, _ = x_sample.shape
    depth = norm_g.shape[0]
    n_heads = attn_sinks.shape[1]
    n_kv = cache_k.shape[3]
    kv_w = n_kv * HEAD_DIM
    assert batch == 1 and dec_seq == CHUNK and cache_k.shape[2] == WINDOW

    g4 = norm_g.reshape(depth, norm_g.shape[1], 1, d)

    xp = x_prompt.reshape(seq, d)
    xs = x_sample.reshape(dec_batch * dec_seq, d)
    cos_p, sin_p = _rope_tables(jnp.arange(0, seq, ROW_TILE), jnp.arange(ROW_TILE))
    cos_s, sin_s = _rope_tables(jnp.full((dec_batch,), PAST_LEN), jnp.arange(dec_seq))

    tm = ROW_TILE
    conv_p, conv_s, k_p, v_p, k_s, v_s = [], [], [], [], [], []
    for layer in range(depth):
        j = layer // 2
        def ffn_pair(xp, xs, slot, n_pre, n_post, side=()):
            ffn = functools.partial(_ffn, norm_g=g4, layer=layer, n_pre=n_pre, n_post=n_post, tm=FFN_TM)
            xs, bf16_w = ffn(xs, weights=(w_ffn_in, w_ffn_out, slot), tf=FFN_TF_F32, tile_cols=FFN_TF)
            return ffn(xp, weights=bf16_w, tf=FFN_TF, side=side), xs

        if layer % 2 == 0:
            side = ((w_conv_in, (j,), SIDE_BLOCK), (w_conv_out, (j,), SIDE_BLOCK))
        else:
            side = ((w_qkv, (j,), (SIDE_BLOCK[0], SIDE_BLOCK[1] // 2)), (w_attn_out, (j,), SIDE_BLOCK))
        (xp, (w_mix_in_b, w_mix_out_b)), xs = ffn_pair(xp, xs, 0, 0, 1, side)
        if layer % 2 == 0:
            mix = functools.partial(_conv_mixer, norm_g=g4, w_in=w_mix_in_b, conv_w=w_conv,
                                    w_out=w_mix_out_b, layer=layer, j=j, tm=tm)
            xp, st = mix(xp, jnp.zeros((1, 2, d), F32), seg=tm, carried=True, tn=d)
            conv_p.append(st)
            xs, st = mix(xs, state_conv[j], seg=dec_seq, carried=False, tn=CONV_TN)
            conv_s.append(st)
        else:
            proj = functools.partial(_qkv_project, norm_g=g4, w_qkv=w_mix_in_b, layer=layer,
                                     tm=tm, n_heads=n_heads, n_kv=n_kv)
            att = functools.partial(_attention, sinks=attn_sinks[j], norm_g=g4, w_o=w_mix_out_b,
                                    layer=layer, n_kv=n_kv)
            q, k, v = proj(xp, cos=cos_p, sin=sin_p)
            xp = att(xp, q, k, k, v, v, tq=tm, sq=WINDOW, streaming=True)
            k_p.append(k[seq - WINDOW:].reshape(1, WINDOW, n_kv, HEAD_DIM))
            v_p.append(v[seq - WINDOW:].reshape(1, WINDOW, n_kv, HEAD_DIM))
            q, k, v = proj(xs, cos=cos_s, sin=sin_s)
            ck = cache_k[j].reshape(dec_batch, WINDOW, kv_w)
            cv = cache_v[j].reshape(dec_batch, WINDOW, kv_w)
            xs = att(xs, q, ck, k, cv, v, tq=tm, sq=dec_seq, streaming=False)
            k_new = jnp.concatenate([ck, k.reshape(dec_batch, dec_seq, kv_w)], axis=1)[:, -WINDOW:]
            v_new = jnp.concatenate([cv, v.reshape(dec_batch, dec_seq, kv_w)], axis=1)[:, -WINDOW:]
            k_s.append(k_new.reshape(dec_batch, WINDOW, n_kv, HEAD_DIM))
            v_s.append(v_new.reshape(dec_batch, WINDOW, n_kv, HEAD_DIM))
        xp, xs = ffn_pair(xp, xs, 1, 4, 5)

    return (xp.reshape(batch, seq, d), xs.reshape(dec_batch, dec_seq, d),
            jnp.stack(conv_p), jnp.stack(conv_s),
            jnp.stack(k_p), jnp.stack(v_p), jnp.stack(k_s), jnp.stack(v_s))
```

```python
import functools

import jax
import jax.numpy as jnp
from jax import lax
from jax.experimental import pallas as pl
from jax.experimental.pallas import tpu as pltpu

F32 = jnp.float32
BF16 = jnp.bfloat16

CHUNK = 64
CHUNK_SHIFT = CHUNK.bit_length() - 1
assert 1 << CHUNK_SHIFT == CHUNK
LOOKBACK_CHUNKS = 2
WINDOW = LOOKBACK_CHUNKS * CHUNK
HEAD_DIM = 64
PAST_LEN = 2048
ROPE_THETA = 10000.0
NORM_EPS = 1e-6

LANES = 128
HEADS_PER_COL = LANES // HEAD_DIM
VMEM_LIMIT = 62 * 1024 * 1024
ROW_TILE = 512
FFN_TM = 1024
FFN_POST_ROWS = 512
FFN_TF = 512
SIDE_BLOCK = (128, 1024)
FFN_TF_F32 = 256
CONV_TN = 512
CONV_COLS = 256
ATTN_OUT_COLS = 256
ATTN_MIN_LAGGED_TILES = 4


def _rms(x, g):
    ms = jnp.mean(x * x, axis=-1, keepdims=True)
    return x * lax.rsqrt(ms + NORM_EPS) * g


def _params(*sem):
    return pltpu.CompilerParams(dimension_semantics=sem, vmem_limit_bytes=VMEM_LIMIT)


def _ffn_kernel(x_ref, gpre_ref, gpost_ref, wg_ref, wu_ref, wo_ref, *rest, cast_own, side_blocks):
    n_side = len(side_blocks)
    side_in, o_ref, h_ref = rest[:n_side], rest[n_side], rest[-1]
    extra_out = rest[n_side + 1:-1]
    own_out, side_out = (extra_out, ()) if cast_own else ((), extra_out)
    j = pl.program_id(1)
    last = pl.num_programs(1) - 1
    step_id = pl.program_id(0) * pl.num_programs(1) + j
    for src, dst, n_blocks in zip(side_in, side_out, side_blocks):
        @pl.when(step_id < n_blocks)
        def _(src=src, dst=dst):
            dst[...] = src[...].astype(BF16)

    def step(first, final):
        wg, wu, wo = wg_ref[...], wu_ref[...], wo_ref[...]
        if cast_own:
            wg, wu, wo = wg.astype(BF16), wu.astype(BF16), wo.astype(BF16)
            for w, dst in zip((wg, wu, wo), own_out):
                dst[...] = w
        if first:
            h = _rms(x_ref[...], gpre_ref[...]).astype(BF16)
            h_ref[...] = h
        else:
            h = h_ref[...]
        gate = jnp.dot(h, wg, preferred_element_type=F32)
        up = jnp.dot(h, wu, preferred_element_type=F32)
        act = (gate * jax.nn.sigmoid(gate) * up).astype(BF16)
        if final:
            for r in range(0, x_ref.shape[0], FFN_POST_ROWS):
                sub = slice(r, r + FFN_POST_ROWS)
                part = jnp.dot(act[sub], wo, preferred_element_type=F32)
                o_ref[sub] = x_ref[sub] + 0.5 * _rms(o_ref[sub] + part, gpost_ref[...])
            return
        part = jnp.dot(act, wo, preferred_element_type=F32)
        if first:
            o_ref[...] = part
        else:
            o_ref[...] += part

    pl.when(j == 0)(functools.partial(step, True, False))
    pl.when((j > 0) & (j < last))(functools.partial(step, False, False))
    pl.when(j == last)(functools.partial(step, False, True))


def _side_cast_specs(arr, lead, block, n_steps, nf):
    (rows, cols), (br, bc) = arr.shape[-2:], block
    per_row = cols // bc
    n_blocks = (rows // br) * per_row
    assert rows % br == 0 and cols % bc == 0 and n_blocks <= n_steps
    blk = lambda i, j: jnp.minimum(i * nf + j, n_blocks - 1)
    src = pl.BlockSpec((None,) * len(lead) + block, lambda i, j: lead + (blk(i, j) // per_row, blk(i, j) % per_row))
    dst = pl.BlockSpec(block, lambda i, j: (blk(i, j) // per_row, blk(i, j) % per_row))
    return src, dst, n_blocks


def _ffn(x, norm_g, weights, layer, n_pre, n_post, *, tm, tf, tile_cols=None, side=()):
    m, d = x.shape
    cast_own = weights[0].dtype == F32
    assert not (cast_own and side)
    g_spec = lambda n: pl.BlockSpec((None, None, 1, d), lambda i, j: (layer, n, 0, 0))
    out_specs = [pl.BlockSpec((tm, d), lambda i, j: (i, 0))]
    out_shape = [jax.ShapeDtypeStruct((m, d), F32)]
    if cast_own:
        w_in, w_out, slot = weights
        f = w_out.shape[2]
        nf, per_tile = f // tf, tile_cols // tf
        assert m == tm and per_tile * tf == tile_cols
        w_specs = [
            pl.BlockSpec((None, None, d, tf), lambda i, j: (layer, slot, 0, j)),
            pl.BlockSpec((None, None, d, tf), lambda i, j: (layer, slot, 0, j + nf)),
            pl.BlockSpec((None, None, tf, d), lambda i, j: (layer, slot, j, 0)),
        ]
        operands = [w_in, w_in, w_out]
        tile_spec = pl.BlockSpec((None, d, tf), lambda i, j: (j // per_tile, 0, j % per_tile))
        out_specs += [tile_spec, tile_spec, pl.BlockSpec((tf, d), lambda i, j: (j, 0))]
        tiles = jax.ShapeDtypeStruct((f // tile_cols, d, tile_cols), BF16)
        out_shape += [tiles, tiles, jax.ShapeDtypeStruct((f, d), BF16)]
    else:
        gate, up, down = operands = list(weights)
        f = down.shape[0]
        assert gate.shape[2] == tf
        tile_spec = pl.BlockSpec((None, d, tf), lambda i, j: (j, 0, 0))
        w_specs = [tile_spec, tile_spec, pl.BlockSpec((tf, d), lambda i, j: (j, 0))]
    side_blocks = []
    for arr, lead, block in side:
        src, dst, n_blocks = _side_cast_specs(arr, lead, block, (m // tm) * (f // tf), f // tf)
        w_specs.append(src)
        operands.append(arr)
        out_specs.append(dst)
        out_shape.append(jax.ShapeDtypeStruct(arr.shape[-2:], BF16))
        side_blocks.append(n_blocks)
    outs = pl.pallas_call(
        functools.partial(_ffn_kernel, cast_own=cast_own, side_blocks=tuple(side_blocks)),
        grid=(m // tm, f // tf),
        in_specs=[pl.BlockSpec((tm, d), lambda i, j: (i, 0)), g_spec(n_pre), g_spec(n_post)] + w_specs,
        out_specs=out_specs,
        out_shape=out_shape,
        scratch_shapes=[pltpu.VMEM((tm, d), BF16)],
        compiler_params=_params("arbitrary" if side else "parallel", "arbitrary"),
        name="ffn",
    )(x, norm_g, norm_g, *operands)
    return (outs[0], tuple(outs[1:])) if cast_own or side else outs[0]


def _conv_kernel(x_ref, gpre_ref, gpost_ref, wb_ref, wc_ref, wh_ref, cw_ref, wo_ref, st_ref,
                 o_ref, ns_ref, h_ref, carry_ref, *, seg, carried):
    i = pl.program_id(0)
    n = pl.program_id(1)
    last = pl.num_programs(1) - 1
    tm = x_ref.shape[0]
    tn = wb_ref.shape[1]
    tc = CONV_COLS
    cols = pl.ds(pl.multiple_of(n * tn, tn), tn)

    if carried:
        @pl.when(i == 0)
        def _():
            carry_ref[:, cols] = st_ref[0, :, cols]

    def gated_conv(c0, b_gate, c_gate, hh):
        gcols = pl.ds(pl.multiple_of(n * tn + c0, tc), tc)
        cw = cw_ref[:, c0:c0 + tc]
        w0, w1, w2 = cw[0:1], cw[1:2], cw[2:3]
        row = lax.broadcasted_iota(jnp.int32, (seg, tc), 0)
        u = c_gate * hh
        convs = []
        for s in range(tm // seg):
            us = u[s * seg:(s + 1) * seg]
            prev = carry_ref[:, gcols] if carried else st_ref[s, :, c0:c0 + tc]
            p1 = jnp.where(row == 0, prev[1:2], pltpu.roll(us, 1, 0))
            p2 = jnp.where(row == 0, prev[0:1], jnp.where(row == 1, prev[1:2], pltpu.roll(us, 2, 0)))
            convs.append(p2 * w0 + p1 * w1 + us * w2)
            if carried:
                carry_ref[:, gcols] = us[seg - 2:seg]
                ns_ref[0, :, gcols] = us[seg - 2:seg]
            else:
                ns_ref[s, :, c0:c0 + tc] = us[seg - 2:seg]
        conv = convs[0] if len(convs) == 1 else jnp.concatenate(convs, axis=0)
        return (b_gate * conv).astype(BF16)

    def out_proj(assign, final, c0, z):
        part = jnp.dot(z, wo_ref[c0:c0 + tc, :], preferred_element_type=F32)
        if assign:
            o_ref[...] = part
        elif final:
            o_ref[...] = x_ref[...] + _rms(o_ref[...] + part, gpost_ref[...])
        else:
            o_ref[...] += part

    def step(first, final):
        if first:
            h = _rms(x_ref[...], gpre_ref[...]).astype(BF16)
            h_ref[...] = h
        else:
            h = h_ref[...]
        pending = None
        for c0 in range(0, tn, tc):
            b_gate = jnp.dot(h, wb_ref[:, c0:c0 + tc], preferred_element_type=F32)
            c_gate = jnp.dot(h, wc_ref[:, c0:c0 + tc], preferred_element_type=F32)
            hh = jnp.dot(h, wh_ref[:, c0:c0 + tc], preferred_element_type=F32)
            if pending is not None:
                out_proj(first and pending[0] == 0, False, *pending)
            pending = (c0, gated_conv(c0, b_gate, c_gate, hh))
        out_proj(first and pending[0] == 0, final, *pending)

    if tn == o_ref.shape[1]:
        step(True, True)
    else:
        pl.when(n == 0)(functools.partial(step, True, False))
        pl.when((n > 0) & (n < last))(functools.partial(step, False, False))
        pl.when(n == last)(functools.partial(step, False, True))


def _conv_mixer(x, state, norm_g, w_in, conv_w, w_out, layer, j, *, tm, tn, seg, carried):
    m, d = x.shape
    nn = d // tn
    nseg = tm // seg
    g_spec = lambda n: pl.BlockSpec((None, None, 1, d), lambda i, c: (layer, n, 0, 0))
    once = dict(pipeline_mode=pl.Buffered(1)) if nn == 1 else {}
    w_spec = lambda sec: pl.BlockSpec((d, tn), lambda i, c: (0, c + sec * nn), **once)
    if carried:
        st_spec = pl.BlockSpec((1, 2, d), lambda i, c: (0, 0, 0))
    else:
        st_spec = pl.BlockSpec((nseg, 2, tn), lambda i, c: (i, 0, c))
    return pl.pallas_call(
        functools.partial(_conv_kernel, seg=seg, carried=carried),
        grid=(m // tm, nn),
        in_specs=[
            pl.BlockSpec((tm, d), lambda i, c: (i, 0)),
            g_spec(2),
            g_spec(3),
            w_spec(0),
            w_spec(1),
            w_spec(2),
            pl.BlockSpec((None, 3, tn), lambda i, c: (j, 0, c)),
            pl.BlockSpec((tn, d), lambda i, c: (c, 0), **once),
            st_spec,
        ],
        out_specs=[pl.BlockSpec((tm, d), lambda i, c: (i, 0)), st_spec],
        out_shape=[
            jax.ShapeDtypeStruct((m, d), F32),
            jax.ShapeDtypeStruct(state.shape, F32),
        ],
        scratch_shapes=[pltpu.VMEM((tm, d), BF16), pltpu.VMEM((2, d), F32)],
        compiler_params=_params("arbitrary", "arbitrary"),
        name="conv_mixer",
    )(x, norm_g, norm_g, w_in, w_in, w_in, conv_w, w_out, state)


def _rope_col(xc, cos, sin_signed):
    lane = lax.broadcasted_iota(jnp.int32, xc.shape, 1)
    first_half = (lane % HEAD_DIM) < HEAD_DIM // 2
    swapped = jnp.where(first_half,
                        pltpu.roll(xc, LANES - HEAD_DIM // 2, 1),
                        pltpu.roll(xc, HEAD_DIM // 2, 1))
    return xc * cos + swapped * sin_signed


def _qkv_kernel(x_ref, g_ref, w_ref, cos_ref, sin_ref, q_ref, k_ref, v_ref, *, chunk):
    n_q = q_ref.shape[0] * LANES
    kv_w = k_ref.shape[1]
    h = _rms(x_ref[...], g_ref[...]).astype(BF16)
    cos = cos_ref[...]
    sin = sin_ref[...]
    for c0 in range(0, n_q, chunk):
        y = jnp.dot(h, w_ref[:, c0:c0 + chunk], preferred_element_type=F32)
        for c in range(chunk // LANES):
            qc = _rope_col(y[:, c * LANES:(c + 1) * LANES], cos, sin) * (HEAD_DIM ** -0.5)
            q_ref[c0 // LANES + c] = qc.astype(BF16)
    y = jnp.dot(h, w_ref[:, n_q:], preferred_element_type=F32)
    for c in range(kv_w // LANES):
        k_ref[:, c * LANES:(c + 1) * LANES] = _rope_col(y[:, c * LANES:(c + 1) * LANES], cos, sin)
    v_ref[...] = y[:, kv_w:]


def _qkv_project(x, norm_g, w_qkv, cos, sin, layer, *, tm, n_heads, n_kv):
    m, d = x.shape
    kv_w = n_kv * HEAD_DIM
    n_cols = n_heads // HEADS_PER_COL
    return pl.pallas_call(
        functools.partial(_qkv_kernel, chunk=2 * kv_w),
        grid=(m // tm,),
        in_specs=[
            pl.BlockSpec((tm, d), lambda i: (i, 0)),
            pl.BlockSpec((None, None, 1, d), lambda i: (layer, 2, 0, 0)),
            pl.BlockSpec((d, w_qkv.shape[1]), lambda i: (0, 0)),
            pl.BlockSpec((tm, LANES), lambda i: (i, 0)),
            pl.BlockSpec((tm, LANES), lambda i: (i, 0)),
        ],
        out_specs=[
            pl.BlockSpec((n_cols, tm, LANES), lambda i: (0, i, 0)),
            pl.BlockSpec((tm, kv_w), lambda i: (i, 0)),
            pl.BlockSpec((tm, kv_w), lambda i: (i, 0)),
        ],
        out_shape=[
            jax.ShapeDtypeStruct((n_cols, m, LANES), BF16),
            jax.ShapeDtypeStruct((m, kv_w), F32),
            jax.ShapeDtypeStruct((m, kv_w), F32),
        ],
        compiler_params=_params("parallel"),
        name="qkv_rope",
    )(x, norm_g, w_qkv, cos, sin)


def _block_diag(col, rolled, first):
    lane = lax.broadcasted_iota(jnp.int32, col.shape, 1)
    lo = lane < HEAD_DIM
    zero = jnp.zeros_like(col)
    if first:
        return jnp.concatenate([jnp.where(lo, col, zero), jnp.where(lo, zero, rolled)], axis=0)
    return jnp.concatenate([jnp.where(lo, rolled, zero), jnp.where(lo, zero, col)], axis=0)


def _attn_kernel(sink_ref, q_ref, kp_ref, kc_ref, vp_ref, vc_ref, x_ref, g_ref, wo_ref, o_ref,
                 att_ref, *, sq, n_kv, streaming, lag, q_pos0, k_pos0):
    i = pl.program_id(0)
    t = jnp.minimum(i, pl.num_programs(0) - 1 - lag)
    cur = lax.rem(i, 2) if lag else 0
    tq, d = x_ref.shape
    nsub = tq // sq
    nk_real = WINDOW + sq
    nk = -(-nk_real // LANES) * LANES
    assert nk == nk_real or not streaming
    n_cols = q_ref.shape[0]
    cols_per_kv = n_cols // n_kv
    n_blocks = nsub * n_kv
    blocks_per_chunk = n_blocks // (d // ATTN_OUT_COLS)

    if lag:
        @pl.when(i == 0)
        def _():
            att_ref[1] = jnp.zeros((tq, d), BF16)

    if streaming:
        k_all = jnp.concatenate([kp_ref[...], kc_ref[...]], axis=0)
        v_all = jnp.concatenate([vp_ref[...], vc_ref[...]], axis=0)

    row = lax.broadcasted_iota(jnp.int32, (sq, nk), 0)
    colk = lax.broadcasted_iota(jnp.int32, (sq, nk), 1)

    @functools.cache
    def sub_tile(s):
        if streaming:
            q0 = t * tq + s * sq
            k0 = q0 - WINDOW
            ks = k_all[s * sq:s * sq + nk]
            vs = v_all[s * sq:s * sq + nk]
        else:
            q0 = q_pos0
            k0 = k_pos0
            pad = [jnp.zeros((nk - nk_real, kp_ref.shape[2]), F32)] if nk > nk_real else []
            ks = jnp.concatenate([kp_ref[s], kc_ref[s * sq:(s + 1) * sq]] + pad, axis=0)
            vs = jnp.concatenate([vp_ref[s], vc_ref[s * sq:(s + 1) * sq]] + pad, axis=0)
        kpos = k0 + colk
        q_chunk = lax.shift_right_arithmetic(q0 + row, CHUNK_SHIFT)
        k_chunk = lax.shift_right_arithmetic(kpos, CHUNK_SHIFT)
        valid = (k_chunk >= q_chunk - LOOKBACK_CHUNKS) & (k_chunk <= q_chunk)
        if streaming:
            valid = valid & (kpos >= 0)
        if nk > nk_real:
            valid = valid & (colk < nk_real)
        return ks, vs, valid

    def scores(s, kv):
        ks, _, _ = sub_tile(s)
        c, first = divmod(kv, HEADS_PER_COL)
        kcol = ks[:, c * LANES:(c + 1) * LANES]
        bdk = _block_diag(kcol, pltpu.roll(kcol, HEAD_DIM, 1), first == 0).astype(BF16)
        qs = q_ref[kv * cols_per_kv:(kv + 1) * cols_per_kv, s * sq:(s + 1) * sq, :]
        qs = qs.reshape(cols_per_kv * sq, LANES)
        return lax.dot_general(qs, bdk, (((1,), (1,)), ((), ())), preferred_element_type=F32)

    def attend(s, kv, sc):
        _, vs, valid = sub_tile(s)
        c, first = divmod(kv, HEADS_PER_COL)
        vcol = vs[:, c * LANES:(c + 1) * LANES]
        bdv = _block_diag(vcol, pltpu.roll(vcol, HEAD_DIM, 1), first == 0).astype(BF16)
        p_rows = []
        for p in range(cols_per_kv):
            halves = []
            for e in range(HEADS_PER_COL):
                sink = sink_ref[(kv * cols_per_kv + p) * HEADS_PER_COL + e]
                blk = sc[p * sq:(p + 1) * sq, e * nk:(e + 1) * nk]
                blk = jnp.where(valid, blk, -jnp.inf)
                mx = jnp.maximum(jnp.max(blk, axis=-1, keepdims=True), sink)
                ex = jnp.exp(blk - mx)
                den = jnp.sum(ex, axis=-1, keepdims=True) + jnp.exp(sink - mx)
                halves.append((ex * (1.0 / den)).astype(BF16))
            p_rows.append(jnp.concatenate(halves, axis=1))
        pm = jnp.concatenate(p_rows, axis=0)
        ov = jnp.dot(pm, bdv, preferred_element_type=F32)
        for p in range(cols_per_kv):
            cc = kv * cols_per_kv + p
            att_ref[cur, s * sq:(s + 1) * sq, cc * LANES:(cc + 1) * LANES] = (
                ov[p * sq:(p + 1) * sq].astype(BF16))

    blocks = [(s, kv) for s in range(nsub) for kv in range(n_kv)]
    sc_next = scores(*blocks[0])
    for b, (s, kv) in enumerate(blocks):
        sc = sc_next
        if b + 1 < n_blocks:
            sc_next = scores(*blocks[b + 1])
        if lag and (b + 1) % blocks_per_chunk == 0:
            oc = ((b + 1) // blocks_per_chunk - 1) * ATTN_OUT_COLS
            o_ref[:, oc:oc + ATTN_OUT_COLS] = jnp.dot(
                att_ref[1 - cur], wo_ref[:, oc:oc + ATTN_OUT_COLS], preferred_element_type=F32)
        attend(s, kv, sc)

    if not lag:
        o_ref[...] = jnp.dot(att_ref[0], wo_ref[...], preferred_element_type=F32)
    o_ref[...] = x_ref[...] + _rms(o_ref[...], g_ref[...])


def _attention(x, q, k_prev, k, v_prev, v, sinks, norm_g, w_o, layer, *, tq, sq, n_kv, streaming):
    m, d = x.shape
    n_cols = q.shape[0]
    kv_w = k.shape[1]
    nsub = tq // sq
    n_tiles = m // tq
    lag = int(n_tiles >= ATTN_MIN_LAGGED_TILES)
    att_tile = lambda i: jnp.minimum(i, n_tiles - 1)
    out_tile = lambda i: jnp.maximum(i - lag, 0)
    if streaming:
        per = tq // WINDOW
        prev_spec = pl.BlockSpec((WINDOW, kv_w), lambda i: (jnp.maximum(att_tile(i) * per - 1, 0), 0))
    else:
        prev_spec = pl.BlockSpec((nsub, WINDOW, kv_w), lambda i: (att_tile(i), 0, 0))
    cur_spec = pl.BlockSpec((tq, kv_w), lambda i: (att_tile(i), 0))
    kern = functools.partial(_attn_kernel, sq=sq, n_kv=n_kv, streaming=streaming, lag=lag,
                             q_pos0=PAST_LEN, k_pos0=PAST_LEN - WINDOW)
    return pl.pallas_call(
        kern,
        grid=(n_tiles + lag,),
        in_specs=[
            pl.BlockSpec(memory_space=pltpu.SMEM),
            pl.BlockSpec((n_cols, tq, LANES), lambda i: (0, att_tile(i), 0)),
            prev_spec, cur_spec, prev_spec, cur_spec,
            pl.BlockSpec((tq, d), lambda i: (out_tile(i), 0)),
            pl.BlockSpec((None, None, 1, d), lambda i: (layer, 3, 0, 0)),
            pl.BlockSpec((d, d), lambda i: (0, 0)),
        ],
        out_specs=pl.BlockSpec((tq, d), lambda i: (out_tile(i), 0)),
        out_shape=jax.ShapeDtypeStruct((m, d), F32),
        scratch_shapes=[pltpu.VMEM((1 + lag, tq, d), BF16)],
        compiler_params=_params("arbitrary"),
        name="swa_attention",
    )(sinks, q, k_prev, k, v_prev, v, x, norm_g, w_o)


def _rope_tables(tile_pos, row_pos):
    half = HEAD_DIM // 2
    inv_freq = ROPE_THETA ** (-jnp.arange(half, dtype=F32) / half)
    inv_freq_lanes = jnp.tile(inv_freq, LANES // half)
    sign_lanes = jnp.tile(jnp.concatenate([-jnp.ones(half, F32), jnp.ones(half, F32)]), HEADS_PER_COL)
    ang_t = tile_pos.astype(F32)[:, None, None] * inv_freq_lanes
    ang_r = row_pos.astype(F32)[None, :, None] * inv_freq_lanes
    cos = jnp.cos(ang_t) * jnp.cos(ang_r) - jnp.sin(ang_t) * jnp.sin(ang_r)
    sin = jnp.sin(ang_t) * jnp.cos(ang_r) + jnp.cos(ang_t) * jnp.sin(ang_r)
    return cos.reshape(-1, LANES), (sign_lanes * sin).reshape(-1, LANES)


def kernel(x_prompt, x_sample, state_conv, cache_k, cache_v, norm_g, w_ffn_in, w_ffn_out,
           w_conv_in, w_conv, w_conv_out, w_qkv, w_attn_out, attn_sinks):
    batch, seq, d = x_prompt.shape
    dec_batch, dec_seq, _ = x_sample.shape
    depth = norm_g.shape[0]
    n_heads = attn_sinks.shape[1]
    n_kv = cache_k.shape[3]
    kv_w = n_kv * HEAD_DIM
    assert batch == 1 and dec_seq == CHUNK and cache_k.shape[2] == WINDOW

    g4 = norm_g.reshape(depth, norm_g.shape[1], 1, d)

    xp = x_prompt.reshape(seq, d)
    xs = x_sample.reshape(dec_batch * dec_seq, d)
    cos_p, sin_p = _rope_tables(jnp.arange(0, seq, ROW_TILE), jnp.arange(ROW_TILE))
    cos_s, sin_s = _rope_tables(jnp.full((dec_batch,), PAST_LEN), jnp.arange(dec_seq))

    tm = ROW_TILE
    conv_p, conv_s, k_p, v_p, k_s, v_s = [], [], [], [], [], []
    for layer in range(depth):
        j = layer // 2
        def ffn_pair(xp, xs, slot, n_pre, n_post, side=()):
            ffn = functools.partial(_ffn, norm_g=g4, layer=layer, n_pre=n_pre, n_post=n_post, tm=FFN_TM)
            xs, bf16_w = ffn(xs, weights=(w_ffn_in, w_ffn_out, slot), tf=FFN_TF_F32, tile_cols=FFN_TF)
            return ffn(xp, weights=bf16_w, tf=FFN_TF, side=side), xs

        if layer % 2 == 0:
            side = ((w_conv_in, (j,), SIDE_BLOCK), (w_conv_out, (j,), SIDE_BLOCK))
        else:
            side = ((w_qkv, (j,), (SIDE_BLOCK[0], SIDE_BLOCK[1] // 2)), (w_attn_out, (j,), SIDE_BLOCK))
        (xp, (w_mix_in_b, w_mix_out_b)), xs = ffn_pair(xp, xs, 0, 0, 1, side)
        if layer % 2 == 0:
            mix = functools.partial(_conv_mixer, norm_g=g4, w_in=w_mix_in_b, conv_w=w_conv,
                                    w_out=w_mix_out_b, layer=layer, j=j, tm=tm)
            xp, st = mix(xp, jnp.zeros((1, 2, d), F32), seg=tm, carried=True, tn=d)
            conv_p.append(st)
            xs, st = mix(xs, state_conv[j], seg=dec_seq, carried=False, tn=CONV_TN)
            conv_s.append(st)
        else:
            proj = functools.partial(_qkv_project, norm_g=g4, w_qkv=w_mix_in_b, layer=layer,
                                     tm=tm, n_heads=n_heads, n_kv=n_kv)
            att = functools.partial(_attention, sinks=attn_sinks[j], norm_g=g4, w_o=w_mix_out_b,
                                    layer=layer, n_kv=n_kv)
            q, k, v = proj(xp, cos=cos_p, sin=sin_p)
            xp = att(xp, q, k, k, v, v, tq=tm, sq=WINDOW, streaming=True)
            k_p.append(k[seq - WINDOW:].reshape(1, WINDOW, n_kv, HEAD_DIM))
            v_p.append(v[seq - WINDOW:].reshape(1, WINDOW, n_kv, HEAD_DIM))
            q, k, v = proj(xs, cos=cos_s, sin=sin_s)
            ck = cache_k[j].reshape(dec_batch, WINDOW, kv_w)
            cv = cache_v[j].reshape(dec_batch, WINDOW, kv_w)
            xs = att(xs, q, ck, k, cv, v, tq=tm, sq=dec_seq, streaming=False)
            k_new = jnp.concatenate([ck, k.reshape(dec_batch, dec_seq, kv_w)], axis=1)[:, -WINDOW:]
            v_new = jnp.concatenate([cv, v.reshape(dec_batch, dec_seq, kv_w)], axis=1)[:, -WINDOW:]
            k_s.append(k_new.reshape(dec_batch, WINDOW, n_kv, HEAD_DIM))
            v_s.append(v_new.reshape(dec_batch, WINDOW, n_kv, HEAD_DIM))
        xp, xs = ffn_pair(xp, xs, 1, 4, 5)

    return (xp.reshape(batch, seq, d), xs.reshape(dec_batch, dec_seq, d),
            jnp.stack(conv_p), jnp.stack(conv_s),
            jnp.stack(k_p), jnp.stack(v_p), jnp.stack(k_s), jnp.stack(v_s))
```

```python
import functools

import jax
import jax.numpy as jnp
from jax import lax
from jax.experimental import pallas as pl
from jax.experimental.pallas import tpu as pltpu

F32 = jnp.float32
BF16 = jnp.bfloat16

CHUNK = 64
CHUNK_SHIFT = CHUNK.bit_length() - 1
assert 1 << CHUNK_SHIFT == CHUNK
LOOKBACK_CHUNKS = 2
WINDOW = LOOKBACK_CHUNKS * CHUNK
HEAD_DIM = 64
PAST_LEN = 2048
ROPE_THETA = 10000.0
NORM_EPS = 1e-6

LANES = 128
HEADS_PER_COL = LANES // HEAD_DIM
VMEM_LIMIT = 62 * 1024 * 1024
ROW_TILE = 512
FFN_TM = 1024
FFN_POST_ROWS = 256
FFN_TF = 512
SIDE_BLOCK = (128, 1024)
FFN_TF_F32 = 256
CONV_TN = 512
CONV_COLS = 256
ATTN_OUT_COLS = 256
ATTN_MIN_LAGGED_TILES = 4


def _rms(x, g):
    ms = jnp.mean(x * x, axis=-1, keepdims=True)
    return x * lax.rsqrt(ms + NORM_EPS) * g


def _params(*sem):
    return pltpu.CompilerParams(dimension_semantics=sem, vmem_limit_bytes=VMEM_LIMIT)


def _ffn_kernel(x_ref, gpre_ref, gpost_ref, wg_ref, wu_ref, wo_ref, *rest, cast_own, side_blocks):
    n_side = len(side_blocks)
    side_in, o_ref, h_ref = rest[:n_side], rest[n_side], rest[-1]
    extra_out = rest[n_side + 1:-1]
    own_out, side_out = (extra_out, ()) if cast_own else ((), extra_out)
    j = pl.program_id(1)
    last = pl.num_programs(1) - 1
    step_id = pl.program_id(0) * pl.num_programs(1) + j
    for src, dst, n_blocks in zip(side_in, side_out, side_blocks):
        @pl.when(step_id < n_blocks)
        def _(src=src, dst=dst):
            dst[...] = src[...].astype(BF16)

    def step(first, final):
        wg, wu, wo = wg_ref[...], wu_ref[...], wo_ref[...]
        if cast_own:
            wg, wu, wo = wg.astype(BF16), wu.astype(BF16), wo.astype(BF16)
            for w, dst in zip((wg, wu, wo), own_out):
                dst[...] = w
        if first:
            h = _rms(x_ref[...], gpre_ref[...]).astype(BF16)
            h_ref[...] = h
        else:
            h = h_ref[...]
        gate = jnp.dot(h, wg, preferred_element_type=F32)
        up = jnp.dot(h, wu, preferred_element_type=F32)
        act = (gate * jax.nn.sigmoid(gate) * up).astype(BF16)
        if final:
            for r in range(0, x_ref.shape[0], FFN_POST_ROWS):
                sub = slice(r, r + FFN_POST_ROWS)
                part = jnp.dot(act[sub], wo, preferred_element_type=F32)
                o_ref[sub] = x_ref[sub] + 0.5 * _rms(o_ref[sub] + part, gpost_ref[...])
            return
        part = jnp.dot(act, wo, preferred_element_type=F32)
        if first:
            o_ref[...] = part
        else:
            o_ref[...] += part

    pl.when(j == 0)(functools.partial(step, True, False))
    pl.when((j > 0) & (j < last))(functools.partial(step, False, False))
    pl.when(j == last)(functools.partial(step, False, True))


def _side_cast_specs(arr, lead, block, n_steps, nf):
    (rows, cols), (br, bc) = arr.shape[-2:], block
    per_row = cols // bc
    n_blocks = (rows // br) * per_row
    assert rows % br == 0 and cols % bc == 0 and n_blocks <= n_steps
    blk = lambda i, j: jnp.minimum(i * nf + j, n_blocks - 1)
    src = pl.BlockSpec((None,) * len(lead) + block, lambda i, j: lead + (blk(i, j) // per_row, blk(i, j) % per_row))
    dst = pl.BlockSpec(block, lambda i, j: (blk(i, j) // per_row, blk(i, j) % per_row))
    return src, dst, n_blocks


def _ffn(x, norm_g, weights, layer, n_pre, n_post, *, tm, tf, tile_cols=None, side=()):
    m, d = x.shape
    cast_own = weights[0].dtype == F32
    assert not (cast_own and side)
    g_spec = lambda n: pl.BlockSpec((None, None, 1, d), lambda i, j: (layer, n, 0, 0))
    out_specs = [pl.BlockSpec((tm, d), lambda i, j: (i, 0))]
    out_shape = [jax.ShapeDtypeStruct((m, d), F32)]
    if cast_own:
        w_in, w_out, slot = weights
        f = w_out.shape[2]
        nf, per_tile = f // tf, tile_cols // tf
        assert m == tm and per_tile * tf == tile_cols
        w_specs = [
            pl.BlockSpec((None, None, d, tf), lambda i, j: (layer, slot, 0, j)),
            pl.BlockSpec((None, None, d, tf), lambda i, j: (layer, slot, 0, j + nf)),
            pl.BlockSpec((None, None, tf, d), lambda i, j: (layer, slot, j, 0)),
        ]
        operands = [w_in, w_in, w_out]
        tile_spec = pl.BlockSpec((None, d, tf), lambda i, j: (j // per_tile, 0, j % per_tile))
        out_specs += [tile_spec, tile_spec, pl.BlockSpec((tf, d), lambda i, j: (j, 0))]
        tiles = jax.ShapeDtypeStruct((f // tile_cols, d, tile_cols), BF16)
        out_shape += [tiles, tiles, jax.ShapeDtypeStruct((f, d), BF16)]
    else:
        gate, up, down = operands = list(weights)
        f = down.shape[0]
        assert gate.shape[2] == tf
        tile_spec = pl.BlockSpec((None, d, tf), lambda i, j: (j, 0, 0))
        w_specs = [tile_spec, tile_spec, pl.BlockSpec((tf, d), lambda i, j: (j, 0))]
    side_blocks = []
    for arr, lead, block in side:
        src, dst, n_blocks = _side_cast_specs(arr, lead, block, (m // tm) * (f // tf), f // tf)
        w_specs.append(src)
        operands.append(arr)
        out_specs.append(dst)
        out_shape.append(jax.ShapeDtypeStruct(arr.shape[-2:], BF16))
        side_blocks.append(n_blocks)
    outs = pl.pallas_call(
        functools.partial(_ffn_kernel, cast_own=cast_own, side_blocks=tuple(side_blocks)),
        grid=(m // tm, f // tf),
        in_specs=[pl.BlockSpec((tm, d), lambda i, j: (i, 0)), g_spec(n_pre), g_spec(n_post)] + w_specs,
        out_specs=out_specs,
        out_shape=out_shape,
        scratch_shapes=[pltpu.VMEM((tm, d), BF16)],
        compiler_params=_params("arbitrary" if side else "parallel", "arbitrary"),
        name="ffn",
    )(x, norm_g, norm_g, *operands)
    return (outs[0], tuple(outs[1:])) if cast_own or side else outs[0]


def _conv_kernel(x_ref, gpre_ref, gpost_ref, wb_ref, wc_ref, wh_ref, cw_ref, wo_ref, st_ref,
                 o_ref, ns_ref, h_ref, carry_ref, *, seg, carried):
    i = pl.program_id(0)
    n = pl.program_id(1)
    last = pl.num_programs(1) - 1
    tm = x_ref.shape[0]
    tn = wb_ref.shape[1]
    tc = CONV_COLS
    cols = pl.ds(pl.multiple_of(n * tn, tn), tn)

    if carried:
        @pl.when(i == 0)
        def _():
            carry_ref[:, cols] = st_ref[0, :, cols]

    def gated_conv(c0, b_gate, c_gate, hh):
        gcols = pl.ds(pl.multiple_of(n * tn + c0, tc), tc)
        cw = cw_ref[:, c0:c0 + tc]
        w0, w1, w2 = cw[0:1], cw[1:2], cw[2:3]
        row = lax.broadcasted_iota(jnp.int32, (seg, tc), 0)
        u = c_gate * hh
        convs = []
        for s in range(tm // seg):
            us = u[s * seg:(s + 1) * seg]
            prev = carry_ref[:, gcols] if carried else st_ref[s, :, c0:c0 + tc]
            p1 = jnp.where(row == 0, prev[1:2], pltpu.roll(us, 1, 0))
            p2 = jnp.where(row == 0, prev[0:1], jnp.where(row == 1, prev[1:2], pltpu.roll(us, 2, 0)))
            convs.append(p2 * w0 + p1 * w1 + us * w2)
            if carried:
                carry_ref[:, gcols] = us[seg - 2:seg]
                ns_ref[0, :, gcols] = us[seg - 2:seg]
            else:
                ns_ref[s, :, c0:c0 + tc] = us[seg - 2:seg]
        conv = convs[0] if len(convs) == 1 else jnp.concatenate(convs, axis=0)
        return (b_gate * conv).astype(BF16)

    def out_proj(assign, final, c0, z):
        part = jnp.dot(z, wo_ref[c0:c0 + tc, :], preferred_element_type=F32)
        if assign:
            o_ref[...] = part
        elif final:
            o_ref[...] = x_ref[...] + _rms(o_ref[...] + part, gpost_ref[...])
        else:
            o_ref[...] += part

    def step(first, final):
        if first:
            h = _rms(x_ref[...], gpre_ref[...]).astype(BF16)
            h_ref[...] = h
        else:
            h = h_ref[...]
        pending = None
        for c0 in range(0, tn, tc):
            b_gate = jnp.dot(h, wb_ref[:, c0:c0 + tc], preferred_element_type=F32)
            c_gate = jnp.dot(h, wc_ref[:, c0:c0 + tc], preferred_element_type=F32)
            hh = jnp.dot(h, wh_ref[:, c0:c0 + tc], preferred_element_type=F32)
            if pending is not None:
                out_proj(first and pending[0] == 0, False, *pending)
            pending = (c0, gated_conv(c0, b_gate, c_gate, hh))
        out_proj(first and pending[0] == 0, final, *pending)

    if tn == o_ref.shape[1]:
        step(True, True)
    else:
        pl.when(n == 0)(functools.partial(step, True, False))
        pl.when((n > 0) & (n < last))(functools.partial(step, False, False))
        pl.when(n == last)(functools.partial(step, False, True))


def _conv_mixer(x, state, norm_g, w_in, conv_w, w_out, layer, j, *, tm, tn, seg, carried):
    m, d = x.shape
    nn = d // tn
    nseg = tm // seg
    g_spec = lambda n: pl.BlockSpec((None, None, 1, d), lambda i, c: (layer, n, 0, 0))
    once = dict(pipeline_mode=pl.Buffered(1)) if nn == 1 else {}
    w_spec = lambda sec: pl.BlockSpec((d, tn), lambda i, c: (0, c + sec * nn), **once)
    if carried:
        st_spec = pl.BlockSpec((1, 2, d), lambda i, c: (0, 0, 0))
    else:
        st_spec = pl.BlockSpec((nseg, 2, tn), lambda i, c: (i, 0, c))
    return pl.pallas_call(
        functools.partial(_conv_kernel, seg=seg, carried=carried),
        grid=(m // tm, nn),
        in_specs=[
            pl.BlockSpec((tm, d), lambda i, c: (i, 0)),
            g_spec(2),
            g_spec(3),
            w_spec(0),
            w_spec(1),
            w_spec(2),
            pl.BlockSpec((None, 3, tn), lambda i, c: (j, 0, c)),
            pl.BlockSpec((tn, d), lambda i, c: (c, 0), **once),
            st_spec,
        ],
        out_specs=[pl.BlockSpec((tm, d), lambda i, c: (i, 0)), st_spec],
        out_shape=[
            jax.ShapeDtypeStruct((m, d), F32),
            jax.ShapeDtypeStruct(state.shape, F32),
        ],
        scratch_shapes=[pltpu.VMEM((tm, d), BF16), pltpu.VMEM((2, d), F32)],
        compiler_params=_params("arbitrary", "arbitrary"),
        name="conv_mixer",
    )(x, norm_g, norm_g, w_in, w_in, w_in, conv_w, w_out, state)


def _rope_col(xc, cos, sin_signed):
    lane = lax.broadcasted_iota(jnp.int32, xc.shape, 1)
    first_half = (lane % HEAD_DIM) < HEAD_DIM // 2
    swapped = jnp.where(first_half,
                        pltpu.roll(xc, LANES - HEAD_DIM // 2, 1),
                        pltpu.roll(xc, HEAD_DIM // 2, 1))
    return xc * cos + swapped * sin_signed


def _qkv_kernel(x_ref, g_ref, w_ref, cos_ref, sin_ref, q_ref, k_ref, v_ref, *, chunk):
    n_q = q_ref.shape[0] * LANES
    kv_w = k_ref.shape[1]
    h = _rms(x_ref[...], g_ref[...]).astype(BF16)
    cos = cos_ref[...]
    sin = sin_ref[...]
    for c0 in range(0, n_q, chunk):
        y = jnp.dot(h, w_ref[:, c0:c0 + chunk], preferred_element_type=F32)
        for c in range(chunk // LANES):
            qc = _rope_col(y[:, c * LANES:(c + 1) * LANES], cos, sin) * (HEAD_DIM ** -0.5)
            q_ref[c0 // LANES + c] = qc.astype(BF16)
    y = jnp.dot(h, w_ref[:, n_q:], preferred_element_type=F32)
    for c in range(kv_w // LANES):
        k_ref[:, c * LANES:(c + 1) * LANES] = _rope_col(y[:, c * LANES:(c + 1) * LANES], cos, sin)
    v_ref[...] = y[:, kv_w:]


def _qkv_project(x, norm_g, w_qkv, cos, sin, layer, *, tm, n_heads, n_kv):
    m, d = x.shape
    kv_w = n_kv * HEAD_DIM
    n_cols = n_heads // HEADS_PER_COL
    return pl.pallas_call(
        functools.partial(_qkv_kernel, chunk=2 * kv_w),
        grid=(m // tm,),
        in_specs=[
            pl.BlockSpec((tm, d), lambda i: (i, 0)),
            pl.BlockSpec((None, None, 1, d), lambda i: (layer, 2, 0, 0)),
            pl.BlockSpec((d, w_qkv.shape[1]), lambda i: (0, 0)),
            pl.BlockSpec((tm, LANES), lambda i: (i, 0)),
            pl.BlockSpec((tm, LANES), lambda i: (i, 0)),
        ],
        out_specs=[
            pl.BlockSpec((n_cols, tm, LANES), lambda i: (0, i, 0)),
            pl.BlockSpec((tm, kv_w), lambda i: (i, 0)),
            pl.BlockSpec((tm, kv_w), lambda i: (i, 0)),
        ],
        out_shape=[
            jax.ShapeDtypeStruct((n_cols, m, LANES), BF16),
            jax.ShapeDtypeStruct((m, kv_w), F32),
            jax.ShapeDtypeStruct((m, kv_w), F32),
        ],
        compiler_params=_params("parallel"),
        name="qkv_rope",
    )(x, norm_g, w_qkv, cos, sin)


def _block_diag(col, rolled, first):
    lane = lax.broadcasted_iota(jnp.int32, col.shape, 1)
    lo = lane < HEAD_DIM
    zero = jnp.zeros_like(col)
    if first:
        return jnp.concatenate([jnp.where(lo, col, zero), jnp.where(lo, zero, rolled)], axis=0)
    return jnp.concatenate([jnp.where(lo, rolled, zero), jnp.where(lo, zero, col)], axis=0)


def _attn_kernel(sink_ref, q_ref, kp_ref, kc_ref, vp_ref, vc_ref, x_ref, g_ref, wo_ref, o_ref,
                 att_ref, *, sq, n_kv, streaming, lag, q_pos0, k_pos0):
    i = pl.program_id(0)
    t = jnp.minimum(i, pl.num_programs(0) - 1 - lag)
    cur = lax.rem(i, 2) if lag else 0
    tq, d = x_ref.shape
    nsub = tq // sq
    nk_real = WINDOW + sq
    nk = -(-nk_real // LANES) * LANES
    assert nk == nk_real or not streaming
    n_cols = q_ref.shape[0]
    cols_per_kv = n_cols // n_kv
    n_blocks = nsub * n_kv
    blocks_per_chunk = n_blocks // (d // ATTN_OUT_COLS)

    if lag:
        @pl.when(i == 0)
        def _():
            att_ref[1] = jnp.zeros((tq, d), BF16)

    if streaming:
        pass

    row = lax.broadcasted_iota(jnp.int32, (sq, nk), 0)
    colk = lax.broadcasted_iota(jnp.int32, (sq, nk), 1)

    @functools.cache
    def sub_tile(s):
        if streaming:
            q0 = t * tq + s * sq
            k0 = q0 - WINDOW
            start = s * sq - WINDOW
            if start >= 0:
                ks, vs = kc_ref[start:start + nk], vc_ref[start:start + nk]
            else:
                ks = jnp.concatenate([kp_ref[WINDOW + start:], kc_ref[:start + nk]], axis=0)
                vs = jnp.concatenate([vp_ref[WINDOW + start:], vc_ref[:start + nk]], axis=0)
        else:
            q0 = q_pos0
            k0 = k_pos0
            pad = [jnp.zeros((nk - nk_real, kp_ref.shape[2]), F32)] if nk > nk_real else []
            ks = jnp.concatenate([kp_ref[s], kc_ref[s * sq:(s + 1) * sq]] + pad, axis=0)
            vs = jnp.concatenate([vp_ref[s], vc_ref[s * sq:(s + 1) * sq]] + pad, axis=0)
        kpos = k0 + colk
        q_chunk = lax.shift_right_arithmetic(q0 + row, CHUNK_SHIFT)
        k_chunk = lax.shift_right_arithmetic(kpos, CHUNK_SHIFT)
        valid = (k_chunk >= q_chunk - LOOKBACK_CHUNKS) & (k_chunk <= q_chunk)
        if streaming:
            valid = valid & (kpos >= 0)
        if nk > nk_real:
            valid = valid & (colk < nk_real)
        return ks, vs, valid

    def scores(s, kv):
        ks, _, _ = sub_tile(s)
        c, first = divmod(kv, HEADS_PER_COL)
        kcol = ks[:, c * LANES:(c + 1) * LANES]
        bdk = _block_diag(kcol, pltpu.roll(kcol, HEAD_DIM, 1), first == 0).astype(BF16)
        qs = q_ref[kv * cols_per_kv:(kv + 1) * cols_per_kv, s * sq:(s + 1) * sq, :]
        qs = qs.reshape(cols_per_kv * sq, LANES)
        return lax.dot_general(qs, bdk, (((1,), (1,)), ((), ())), preferred_element_type=F32)

    def attend(s, kv, sc):
        _, vs, valid = sub_tile(s)
        c, first = divmod(kv, HEADS_PER_COL)
        vcol = vs[:, c * LANES:(c + 1) * LANES]
        bdv = _block_diag(vcol, pltpu.roll(vcol, HEAD_DIM, 1), first == 0).astype(BF16)
        p_rows = []
        for p in range(cols_per_kv):
            halves = []
            for e in range(HEADS_PER_COL):
                sink = sink_ref[(kv * cols_per_kv + p) * HEADS_PER_COL + e]
                blk = sc[p * sq:(p + 1) * sq, e * nk:(e + 1) * nk]
                blk = jnp.where(valid, blk, -jnp.inf)
                mx = jnp.maximum(jnp.max(blk, axis=-1, keepdims=True), sink)
                ex = jnp.exp(blk - mx)
                den = jnp.sum(ex, axis=-1, keepdims=True) + jnp.exp(sink - mx)
                halves.append((ex * (1.0 / den)).astype(BF16))
            p_rows.append(jnp.concatenate(halves, axis=1))
        pm = jnp.concatenate(p_rows, axis=0)
        ov = jnp.dot(pm, bdv, preferred_element_type=F32)
        for p in range(cols_per_kv):
            cc = kv * cols_per_kv + p
            att_ref[cur, s * sq:(s + 1) * sq, cc * LANES:(cc + 1) * LANES] = (
                ov[p * sq:(p + 1) * sq].astype(BF16))

    blocks = [(s, kv) for s in range(nsub) for kv in range(n_kv)]
    sc_next = scores(*blocks[0])
    for b, (s, kv) in enumerate(blocks):
        sc = sc_next
        if b + 1 < n_blocks:
            sc_next = scores(*blocks[b + 1])
        if lag and (b + 1) % blocks_per_chunk == 0:
            oc = ((b + 1) // blocks_per_chunk - 1) * ATTN_OUT_COLS
            o_ref[:, oc:oc + ATTN_OUT_COLS] = jnp.dot(
                att_ref[1 - cur], wo_ref[:, oc:oc + ATTN_OUT_COLS], preferred_element_type=F32)
        attend(s, kv, sc)

    if not lag:
        o_ref[...] = jnp.dot(att_ref[0], wo_ref[...], preferred_element_type=F32)
    o_ref[...] = x_ref[...] + _rms(o_ref[...], g_ref[...])


def _attention(x, q, k_prev, k, v_prev, v, sinks, norm_g, w_o, layer, *, tq, sq, n_kv, streaming):
    m, d = x.shape
    n_cols = q.shape[0]
    kv_w = k.shape[1]
    nsub = tq // sq
    n_tiles = m // tq
    lag = int(n_tiles >= ATTN_MIN_LAGGED_TILES)
    att_tile = lambda i: jnp.minimum(i, n_tiles - 1)
    out_tile = lambda i: jnp.maximum(i - lag, 0)
    if streaming:
        per = tq // WINDOW
        prev_spec = pl.BlockSpec((WINDOW, kv_w), lambda i: (jnp.maximum(att_tile(i) * per - 1, 0), 0))
    else:
        prev_spec = pl.BlockSpec((nsub, WINDOW, kv_w), lambda i: (att_tile(i), 0, 0))
    cur_spec = pl.BlockSpec((tq, kv_w), lambda i: (att_tile(i), 0))
    kern = functools.partial(_attn_kernel, sq=sq, n_kv=n_kv, streaming=streaming, lag=lag,
                             q_pos0=PAST_LEN, k_pos0=PAST_LEN - WINDOW)
    return pl.pallas_call(
        kern,
        grid=(n_tiles + lag,),
        in_specs=[
            pl.BlockSpec(memory_space=pltpu.SMEM),
            pl.BlockSpec((n_cols, tq, LANES), lambda i: (0, att_tile(i), 0)),
            prev_spec, cur_spec, prev_spec, cur_spec,
            pl.BlockSpec((tq, d), lambda i: (out_tile(i), 0)),
            pl.BlockSpec((None, None, 1, d), lambda i: (layer, 3, 0, 0)),
            pl.BlockSpec((d, d), lambda i: (0, 0)),
        ],
        out_specs=pl.BlockSpec((tq, d), lambda i: (out_tile(i), 0)),
        out_shape=jax.ShapeDtypeStruct((m, d), F32),
        scratch_shapes=[pltpu.VMEM((1 + lag, tq, d), BF16)],
        compiler_params=_params("arbitrary"),
        name="swa_attention",
    )(sinks, q, k_prev, k, v_prev, v, x, norm_g, w_o)


def _rope_tables(tile_pos, row_pos):
    half = HEAD_DIM // 2
    inv_freq = ROPE_THETA ** (-jnp.arange(half, dtype=F32) / half)
    inv_freq_lanes = jnp.tile(inv_freq, LANES // half)
    sign_lanes = jnp.tile(jnp.concatenate([-jnp.ones(half, F32), jnp.ones(half, F32)]), HEADS_PER_COL)
    ang_t = tile_pos.astype(F32)[:, None, None] * inv_freq_lanes
    ang_r = row_pos.astype(F32)[None, :, None] * inv_freq_lanes
    cos = jnp.cos(ang_t) * jnp.cos(ang_r) - jnp.sin(ang_t) * jnp.sin(ang_r)
    sin = jnp.sin(ang_t) * jnp.cos(ang_r) + jnp.cos(ang_t) * jnp.sin(ang_r)
    return cos.reshape(-1, LANES), (sign_lanes * sin).reshape(-1, LANES)


def kernel(x_prompt, x_sample, state_conv, cache_k, cache_v, norm_g, w_ffn_in, w_ffn_out,
           w_conv_in, w_conv, w_conv_out, w_qkv, w_attn_out, attn_sinks):
    batch, seq, d = x_prompt.shape
    dec_batch, dec_seq, _ = x_sample.shape
    depth = norm_g.shape[0]
    n_heads = attn_sinks.shape[1]
    n_kv = cache_k.shape[3]
    kv_w = n_kv * HEAD_DIM
    assert batch == 1 and dec_seq == CHUNK and cache_k.shape[2] == WINDOW

    g4 = norm_g.reshape(depth, norm_g.shape[1], 1, d)

    xp = x_prompt.reshape(seq, d)
    xs = x_sample.reshape(dec_batch * dec_seq, d)
    cos_p, sin_p = _rope_tables(jnp.arange(0, seq, ROW_TILE), jnp.arange(ROW_TILE))
    cos_s, sin_s = _rope_tables(jnp.full((dec_batch,), PAST_LEN), jnp.arange(dec_seq))

    tm = ROW_TILE
    conv_p, conv_s, k_p, v_p, k_s, v_s = [], [], [], [], [], []
    for layer in range(depth):
        j = layer // 2
        def ffn_pair(xp, xs, slot, n_pre, n_post, side=()):
            ffn = functools.partial(_ffn, norm_g=g4, layer=layer, n_pre=n_pre, n_post=n_post, tm=FFN_TM)
            xs, bf16_w = ffn(xs, weights=(w_ffn_in, w_ffn_out, slot), tf=FFN_TF_F32, tile_cols=FFN_TF)
            return ffn(xp, weights=bf16_w, tf=FFN_TF, side=side), xs

        if layer % 2 == 0:
            side = ((w_conv_in, (j,), SIDE_BLOCK), (w_conv_out, (j,), SIDE_BLOCK))
        else:
            side = ((w_qkv, (j,), (SIDE_BLOCK[0], SIDE_BLOCK[1] // 2)), (w_attn_out, (j,), SIDE_BLOCK))
        (xp, (w_mix_in_b, w_mix_out_b)), xs = ffn_pair(xp, xs, 0, 0, 1, side)
        if layer % 2 == 0:
            mix = functools.partial(_conv_mixer, norm_g=g4, w_in=w_mix_in_b, conv_w=w_conv,
                                    w_out=w_mix_out_b, layer=layer, j=j, tm=tm)
            xp, st = mix(xp, jnp.zeros((1, 2, d), F32), seg=tm, carried=True, tn=d)
            conv_p.append(st)
            xs, st = mix(xs, state_conv[j], seg=dec_seq, carried=False, tn=CONV_TN)
            conv_s.append(st)
        else:
            proj = functools.partial(_qkv_project, norm_g=g4, w_qkv=w_mix_in_b, layer=layer,
                                     tm=tm, n_heads=n_heads, n_kv=n_kv)
            att = functools.partial(_attention, sinks=attn_sinks[j], norm_g=g4, w_o=w_mix_out_b,
                                    layer=layer, n_kv=n_kv)
            q, k, v = proj(xp, cos=cos_p, sin=sin_p)
            xp = att(xp, q, k, k, v, v, tq=tm, sq=WINDOW, streaming=True)
            k_p.append(k[seq - WINDOW:].reshape(1, WINDOW, n_kv, HEAD_DIM))
            v_p.append(v[seq - WINDOW:].reshape(1, WINDOW, n_kv, HEAD_DIM))
            q, k, v = proj(xs, cos=cos_s, sin=sin_s)
            ck = cache_k[j].reshape(dec_batch, WINDOW, kv_w)
            cv = cache_v[j].reshape(dec_batch, WINDOW, kv_w)
            xs = att(xs, q, ck, k, cv, v, tq=tm, sq=dec_seq, streaming=False)
            k_new = jnp.concatenate([ck, k.reshape(dec_batch, dec_seq, kv_w)], axis=1)[:, -WINDOW:]
            v_new = jnp.concatenate([cv, v.reshape(dec_batch, dec_seq, kv_w)], axis=1)[:, -WINDOW:]
            k_s.append(k_new.reshape(dec_batch, WINDOW, n_kv, HEAD_DIM))
            v_s.append(v_new.reshape(dec_batch, WINDOW, n_kv, HEAD_DIM))
        xp, xs = ffn_pair(xp, xs, 1, 4, 5)

    return (xp.reshape(batch, seq, d), xs.reshape(dec_batch, dec_seq, d),
            jnp.stack(conv_p), jnp.stack(conv_s),
            jnp.stack(k_p), jnp.stack(v_p), jnp.stack(k_s), jnp.stack(v_s))
```
